```python
import math
import jax, jax.numpy as jnp
from jax import lax
import numpy as np

D_MODEL = 1024
BATCH = 4
SEQ = 8192
DEPTH = 4
DEC_BATCH = 8
DEC_SEQ = 2048
PAST_LEN = 128

HEAD_DIM = 64
DA_HEADS = 4
DA_VDIM = 2 * HEAD_DIM
SW_Q_HEADS = 8
SW_KV_HEADS = 2
SW_GROUP = SW_Q_HEADS // SW_KV_HEADS
WINDOW = 128
BLOCK = 128
N_EXPERTS = 16
EC_CAPACITY_FACTOR = 2
D_EXPERT = 1024
PLE_DIM = 256
ROPE_THETA = 10000.0
NORM_EPS = 1e-6
MASK_VALUE = -1e30

DA_Q = DA_HEADS * 2 * HEAD_DIM
DA_K = DA_HEADS * 2 * HEAD_DIM
DA_V = DA_HEADS * DA_VDIM
SW_Q = SW_Q_HEADS * HEAD_DIM
SW_K = SW_KV_HEADS * HEAD_DIM
SW_V = SW_KV_HEADS * HEAD_DIM
D_IN_PROJ = DA_Q + DA_K + DA_V + SW_Q + SW_K + SW_V
D_MIX = DA_V + SW_Q
SPLITS = (DA_Q, DA_Q + DA_K, DA_Q + DA_K + DA_V, DA_Q + DA_K + DA_V + SW_Q, DA_Q + DA_K + DA_V + SW_Q + SW_K)

kernel_name = "hymba_diffattn_swa_ec_moe_encoder"


def rms_norm(x, g):
    xf = x.astype(jnp.float32)
    y = xf * lax.rsqrt(jnp.mean(xf * xf, axis=-1, keepdims=True) + NORM_EPS)
    return (y * g.astype(jnp.float32)).astype(x.dtype)


def rope_tables(seq_len):
    pos = jnp.arange(seq_len, dtype=jnp.float32)
    inv = ROPE_THETA ** (-jnp.arange(0, HEAD_DIM, 2, dtype=jnp.float32) / HEAD_DIM)
    ang = pos[:, None] * inv[None, :]
    ang = jnp.concatenate([ang, ang], axis=-1)
    return jnp.cos(ang), jnp.sin(ang)


def apply_rope(x, cos, sin):
    shape = (cos.shape[0],) + (1,) * (x.ndim - 3) + (HEAD_DIM,)
    c = cos.reshape(shape)
    s = sin.reshape(shape)
    xf = x.astype(jnp.float32)
    x1, x2 = jnp.split(xf, 2, axis=-1)
    rot = jnp.concatenate([-x2, x1], axis=-1)
    return (xf * c + rot * s).astype(x.dtype)


def diff_attention(q, k, v, lam):
    B, S, H = q.shape[0], q.shape[1], q.shape[2]
    nb = S // BLOCK
    scale = HEAD_DIM ** -0.5
    qb = jnp.moveaxis(q.reshape(B, nb, BLOCK, H, 2, HEAD_DIM), 1, 0)

    def one_block(qblk):
        s = jnp.einsum('bqhcd,bkhcd->bhcqk', qblk, k).astype(jnp.float32) * scale
        p = jax.nn.softmax(s, axis=-1)
        a = p[:, :, 0] - lam * p[:, :, 1]
        return jnp.einsum('bhqk,bkhe->bqhe', a.astype(v.dtype), v)

    o = lax.map(one_block, qb)
    return jnp.moveaxis(o, 0, 1).reshape(B, S, H, DA_VDIM)


def sliding_window_attention(q, k, v, sink):
    B, S = q.shape[0], q.shape[1]
    nb = S // BLOCK
    scale = HEAD_DIM ** -0.5
    qb = q.reshape(B, nb, BLOCK, SW_KV_HEADS, SW_GROUP, HEAD_DIM)
    pad = ((0, 0), (BLOCK, BLOCK), (0, 0), (0, 0))
    kp = jnp.pad(k, pad).reshape(B, nb + 2, BLOCK, SW_KV_HEADS, HEAD_DIM)
    vp = jnp.pad(v, pad).reshape(B, nb + 2, BLOCK, SW_KV_HEADS, HEAD_DIM)
    kb = jnp.concatenate([kp[:, :-2], kp[:, 1:-1], kp[:, 2:]], axis=2)
    vb = jnp.concatenate([vp[:, :-2], vp[:, 1:-1], vp[:, 2:]], axis=2)
    s = jnp.einsum('bnqhgd,bnkhd->bnhgqk', qb, kb).astype(jnp.float32) * scale
    qi = jnp.arange(BLOCK)
    kj = jnp.arange(3 * BLOCK)
    rel = kj[None, :] - BLOCK - qi[:, None]
    kpos = jnp.arange(nb)[:, None] * BLOCK - BLOCK + kj[None, :]
    mask = (jnp.abs(rel) <= WINDOW)[None, :, :] & ((kpos >= 0) & (kpos < S))[:, None, :]
    s = jnp.where(mask[None, :, None, None], s, MASK_VALUE)
    sink_l = sink.astype(jnp.float32).reshape(SW_KV_HEADS, SW_GROUP)[None, None, :, :, None, None]
    m = jnp.maximum(jnp.max(s, axis=-1, keepdims=True), sink_l)
    e = jnp.exp(s - m)
    pr = e / (jnp.sum(e, axis=-1, keepdims=True) + jnp.exp(sink_l - m))
    o = jnp.einsum('bnhgqk,bnkhd->bnqhgd', pr.astype(v.dtype), vb)
    return o.reshape(B, S, SW_Q)


def ec_moe(h, w_router, w_gate, w_up, w_down):
    B, S, D = h.shape
    hf = h.reshape(B * S, D)
    n_tok = hf.shape[0]
    cap = EC_CAPACITY_FACTOR * n_tok // N_EXPERTS
    aff = jax.nn.softmax((hf @ w_router).astype(jnp.float32), axis=-1)
    vals, idx = lax.top_k(aff.T, cap)
    xe = hf[idx]
    g = jnp.einsum('ecd,edf->ecf', xe, w_gate)
    u = jnp.einsum('ecd,edf->ecf', xe, w_up)
    ye = jnp.einsum('ecf,efd->ecd', jax.nn.silu(g) * u, w_down)
    ye = ye * vals[..., None].astype(ye.dtype)
    out = jnp.zeros_like(hf).at[idx.reshape(-1)].add(ye.reshape(-1, D))
    return out.reshape(B, S, D)


def encoder(x, p, attn_norm, w_in, q_norm_a, k_norm_a, lambda_q1, lambda_k1, lambda_q2, lambda_k2,
            subln_a, q_norm_b, k_norm_b, sink_b, w_out, ffn_norm, w_router, w_gate_e, w_up_e, w_down_e,
            ple_norm, w_ple_gate, w_ple_proj):
    B, S, _ = x.shape
    cos, sin = rope_tables(S)
    for i in range(DEPTH):
        lam_init = 0.8 - 0.6 * math.exp(-0.3 * i)
        h = rms_norm(x, attn_norm[i])
        proj = h @ w_in[i]
        qa, ka, va, qb, kb, vb = jnp.split(proj, SPLITS, axis=-1)
        qa = apply_rope(rms_norm(qa.reshape(B, S, DA_HEADS, 2, HEAD_DIM), q_norm_a[i]), cos, sin)
        ka = apply_rope(rms_norm(ka.reshape(B, S, DA_HEADS, 2, HEAD_DIM), k_norm_a[i]), cos, sin)
        va = va.reshape(B, S, DA_HEADS, DA_VDIM)
        lam = (jnp.exp(jnp.sum(lambda_q1[i].astype(jnp.float32) * lambda_k1[i].astype(jnp.float32)))
               - jnp.exp(jnp.sum(lambda_q2[i].astype(jnp.float32) * lambda_k2[i].astype(jnp.float32)))
               + lam_init)
        oa = diff_attention(qa, ka, va, lam)
        oa = rms_norm(oa, subln_a[i]) * (1.0 - lam_init)
        qb = apply_rope(rms_norm(qb.reshape(B, S, SW_Q_HEADS, HEAD_DIM), q_norm_b[i]), cos, sin)
        kb = apply_rope(rms_norm(kb.reshape(B, S, SW_KV_HEADS, HEAD_DIM), k_norm_b[i]), cos, sin)
        vb = vb.reshape(B, S, SW_KV_HEADS, HEAD_DIM)
        ob = sliding_window_attention(qb, kb, vb, sink_b[i])
        mix = jnp.concatenate([oa.reshape(B, S, DA_V), ob], axis=-1)
        x = x + mix @ w_out[i]
        x = x + ec_moe(rms_norm(x, ffn_norm[i]), w_router[i], w_gate_e[i], w_up_e[i], w_down_e[i])
        gate = jax.nn.sigmoid(rms_norm(x, ple_norm[i]) @ w_ple_gate[i])
        x = x + gate * (p[i] @ w_ple_proj[i])
    return x


def setup_inputs(seed: int = 0) -> dict:
    key = jax.random.key(seed)
    ks = jax.random.split(key, 24)
    f32 = jnp.float32

    def nrm(k, shape, scale):
        return jax.random.normal(k, shape, f32) * scale

    def gain(k, shape):
        return 1.0 + 0.05 * jax.random.normal(k, shape, f32)

    return {
        "x_prompt": nrm(ks[0], (BATCH, SEQ, D_MODEL), 1.0),
        "x_sample": nrm(ks[1], (DEC_BATCH, DEC_SEQ, D_MODEL), 1.0),
        "p_prompt": nrm(ks[2], (DEPTH, BATCH, SEQ, PLE_DIM), 1.0),
        "p_sample": nrm(ks[3], (DEPTH, DEC_BATCH, DEC_SEQ, PLE_DIM), 1.0),
        "attn_norm": gain(ks[4], (DEPTH, D_MODEL)),
        "w_in": nrm(ks[5], (DEPTH, D_MODEL, D_IN_PROJ), D_MODEL ** -0.5),
        "q_norm_a": gain(ks[6], (DEPTH, HEAD_DIM)),
        "k_norm_a": gain(ks[7], (DEPTH, HEAD_DIM)),
        "lambda_q1": nrm(ks[8], (DEPTH, HEAD_DIM), 0.1),
        "lambda_k1": nrm(ks[9], (DEPTH, HEAD_DIM), 0.1),
        "lambda_q2": nrm(ks[10], (DEPTH, HEAD_DIM), 0.1),
        "lambda_k2": nrm(ks[11], (DEPTH, HEAD_DIM), 0.1),
        "subln_a": gain(ks[12], (DEPTH, DA_VDIM)),
        "q_norm_b": gain(ks[13], (DEPTH, HEAD_DIM)),
        "k_norm_b": gain(ks[14], (DEPTH, HEAD_DIM)),
        "sink_b": nrm(ks[15], (DEPTH, SW_Q_HEADS), 0.5),
        "w_out": nrm(ks[16], (DEPTH, D_MIX, D_MODEL), D_MIX ** -0.5),
        "ffn_norm": gain(ks[17], (DEPTH, D_MODEL)),
        "w_router": nrm(ks[18], (DEPTH, D_MODEL, N_EXPERTS), D_MODEL ** -0.5),
        "w_gate_e": nrm(ks[19], (DEPTH, N_EXPERTS, D_MODEL, D_EXPERT), D_MODEL ** -0.5),
        "w_up_e": nrm(ks[20], (DEPTH, N_EXPERTS, D_MODEL, D_EXPERT), D_MODEL ** -0.5),
        "w_down_e": nrm(ks[21], (DEPTH, N_EXPERTS, D_EXPERT, D_MODEL), D_EXPERT ** -0.5),
        "ple_norm": gain(ks[22], (DEPTH, D_MODEL)),
        "w_ple_gate": nrm(jax.random.fold_in(ks[23], 0), (DEPTH, D_MODEL, D_MODEL), D_MODEL ** -0.5),
        "w_ple_proj": nrm(jax.random.fold_in(ks[23], 1), (DEPTH, PLE_DIM, D_MODEL), PLE_DIM ** -0.5),
    }


def reference(x_prompt, x_sample, p_prompt, p_sample, attn_norm, w_in, q_norm_a, k_norm_a,
              lambda_q1, lambda_k1, lambda_q2, lambda_k2, subln_a, q_norm_b, k_norm_b, sink_b,
              w_out, ffn_norm, w_router, w_gate_e, w_up_e, w_down_e, ple_norm, w_ple_gate, w_ple_proj):
    y_prompt = encoder(x_prompt, p_prompt, attn_norm, w_in, q_norm_a, k_norm_a, lambda_q1, lambda_k1,
                       lambda_q2, lambda_k2, subln_a, q_norm_b, k_norm_b, sink_b, w_out, ffn_norm,
                       w_router, w_gate_e, w_up_e, w_down_e, ple_norm, w_ple_gate, w_ple_proj)
    y_sample = encoder(x_sample, p_sample, attn_norm, w_in, q_norm_a, k_norm_a, lambda_q1, lambda_k1,
                       lambda_q2, lambda_k2, subln_a, q_norm_b, k_norm_b, sink_b, w_out, ffn_norm,
                       w_router, w_gate_e, w_up_e, w_down_e, ple_norm, w_ple_gate, w_ple_proj)
    return (y_prompt, y_sample)
```

```python
import functools
import math

import jax
import jax.numpy as jnp
from jax import lax
from jax.experimental import pallas as pl
from jax.experimental.pallas import tpu as pltpu

F32 = jnp.float32
BF16 = jnp.bfloat16

D_MODEL = 1024
DEPTH = 4
HEAD_DIM = 64
DA_HEADS = 4
SW_Q_HEADS = 8
SW_KV_HEADS = 2
SW_GROUP = SW_Q_HEADS // SW_KV_HEADS
WINDOW = 128
N_EXPERTS = 16
EC_CAPACITY_FACTOR = 2
D_EXPERT = 1024
PLE_DIM = 256
ROPE_THETA = 10000.0
NORM_EPS = 1e-6
MASK_VALUE = -1e30

LANES = 128
DA_W = DA_HEADS * 2 * HEAD_DIM
SWX_W = SW_Q_HEADS * LANES
KVB_W = 2 * LANES
COL_QA, COL_KA, COL_QBX, COL_KVB, COL_VA = 0, 512, 1024, 2048, 2304
NX = 2816
N_ROPE_TILES = (COL_KVB + LANES) // LANES
VMEM_LIMIT = 48 * 1024 * 1024


def _cparams(sem):
    return pltpu.CompilerParams(dimension_semantics=sem, vmem_limit_bytes=VMEM_LIMIT)


def _inproj_kernel(x_ref, gat_ref, w_ref, gain_ref, cos_ref, sa_ref, sb_ref, bd_ref,
                   qa_ref, ka_ref, qbx_ref, kvb_ref, va_ref, xn_ref):
    x = x_ref[...]
    ms = jnp.mean(x * x, axis=-1, keepdims=True)
    xn_ref[...] = (x * lax.rsqrt(ms + NORM_EPS) * gat_ref[...]).astype(BF16)
    cos = cos_ref[...]
    sa = sa_ref[...]
    sb = sb_ref[...]
    outs = ((qa_ref, COL_QA), (ka_ref, COL_KA), (qbx_ref, COL_QBX), (kvb_ref, COL_KVB), (va_ref, COL_VA))

    def out_for(col):
        for ref, base in reversed(outs):
            if col >= base:
                return ref, col - base
        raise AssertionError

    for c in range(NX // 256):
        c0 = c * 256
        p = jnp.dot(xn_ref[...], w_ref[:, c0:c0 + 256], preferred_element_type=F32)
        normed = [(c0 + t * LANES) // LANES < N_ROPE_TILES for t in range(2)]
        if any(normed):
            ss = jnp.dot((p * p).astype(BF16), bd_ref[...], preferred_element_type=F32)
        for t in range(2):
            col = c0 + t * LANES
            y = p[:, t * LANES:(t + 1) * LANES]
            if normed[t]:
                sst = ss[:, t * LANES:(t + 1) * LANES]
                y = y * lax.rsqrt(sst * (1.0 / HEAD_DIM) + NORM_EPS) * gain_ref[:, col:col + LANES]
                y = y * cos + pltpu.roll(y, 96, 1) * sa + pltpu.roll(y, 32, 1) * sb
            ref, off = out_for(col)
            ref[:, off:off + LANES] = y.astype(BF16)


def _inproj(x, g_attn, w, gain, cos, sa, sb, bd, seq, tm):
    n = x.shape[0]
    nblk_s = seq // tm
    tok = lambda i: (i, 0)
    full = lambda i: (0, 0)
    rope = lambda i: (i % nblk_s, 0)
    return pl.pallas_call(
        _inproj_kernel,
        grid=(n // tm,),
        in_specs=[
            pl.BlockSpec((tm, D_MODEL), tok),
            pl.BlockSpec((1, D_MODEL), full),
            pl.BlockSpec((D_MODEL, NX), full),
            pl.BlockSpec((1, NX), full),
            pl.BlockSpec((tm, LANES), rope),
            pl.BlockSpec((tm, LANES), rope),
            pl.BlockSpec((tm, LANES), rope),
            pl.BlockSpec((256, 256), full),
        ],
        out_specs=[
            pl.BlockSpec((tm, DA_W), tok),
            pl.BlockSpec((tm, DA_W), tok),
            pl.BlockSpec((tm, SWX_W), tok),
            pl.BlockSpec((tm, KVB_W), tok),
            pl.BlockSpec((tm, DA_W), tok),
        ],
        out_shape=[
            jax.ShapeDtypeStruct((n, DA_W), BF16),
            jax.ShapeDtypeStruct((n, DA_W), BF16),
            jax.ShapeDtypeStruct((n, SWX_W), BF16),
            jax.ShapeDtypeStruct((n, KVB_W), BF16),
            jax.ShapeDtypeStruct((n, DA_W), BF16),
        ],
        scratch_shapes=[pltpu.VMEM((tm, D_MODEL), BF16)],
        compiler_params=_cparams(("parallel",)),
        name="inproj",
    )(x, g_attn, w, gain, cos, sa, sb, bd)


def _diffattn_kernel(q_ref, k_ref, v_ref, lam4_ref, g_ref, o_ref, *, tq, tk, nk, lam_init):
    q = q_ref[0]
    lane = lax.broadcasted_iota(jnp.int32, q.shape, 1)
    zero = jnp.zeros_like(q)
    qs = jnp.concatenate([jnp.where(lane < HEAD_DIM, q, zero), jnp.where(lane >= HEAD_DIM, q, zero)], axis=0)

    def body(j, carry):
        m, l, acc = carry
        k0 = pl.multiple_of(j * tk, tk)
        kj = k_ref[0, pl.ds(k0, tk), :]
        vj = v_ref[0, pl.ds(k0, tk), :]
        s = lax.dot_general(qs, kj, (((1,), (1,)), ((), ())), preferred_element_type=F32)
        m_new = jnp.maximum(m, jnp.max(s, axis=-1, keepdims=True))
        alpha = jnp.exp(m - m_new)
        p = jnp.exp(s - m_new)
        l = alpha * l + jnp.sum(p, axis=-1, keepdims=True)
        acc = alpha * acc + jnp.dot(p.astype(BF16), vj, preferred_element_type=F32)
        return m_new, l, acc

    init = (jnp.full((2 * tq, 1), -jnp.inf, F32), jnp.zeros((2 * tq, 1), F32), jnp.zeros((2 * tq, LANES), F32))
    _, l, acc = lax.fori_loop(0, nk, body, init)
    o = acc / l
    lam4 = lam4_ref[...]
    lam = (jnp.exp(jnp.sum(lam4[0:1] * lam4[1:2], axis=-1, keepdims=True))
           - jnp.exp(jnp.sum(lam4[2:3] * lam4[3:4], axis=-1, keepdims=True)) + lam_init)
    o = o[:tq] - lam * o[tq:]
    ms = jnp.mean(o * o, axis=-1, keepdims=True)
    o = o * lax.rsqrt(ms + NORM_EPS) * g_ref[...] * (1.0 - lam_init)
    o_ref[0] = o.astype(BF16)


def _diffattn(qa, ka, va, lam4, g_sub, lam_init, tq, tk):
    b, s, _ = qa.shape
    kern = functools.partial(_diffattn_kernel, tq=tq, tk=tk, nk=s // tk, lam_init=lam_init)
    return pl.pallas_call(
        kern,
        grid=(b, DA_HEADS, s // tq),
        in_specs=[
            pl.BlockSpec((1, tq, LANES), lambda bi, h, i: (bi, i, h)),
            pl.BlockSpec((1, s, LANES), lambda bi, h, i: (bi, 0, h)),
            pl.BlockSpec((1, s, LANES), lambda bi, h, i: (bi, 0, h)),
            pl.BlockSpec((4, HEAD_DIM), lambda bi, h, i: (0, 0)),
            pl.BlockSpec((1, LANES), lambda bi, h, i: (0, 0)),
        ],
        out_specs=pl.BlockSpec((1, tq, LANES), lambda bi, h, i: (bi, i, h)),
        out_shape=jax.ShapeDtypeStruct((b, s, DA_W), BF16),
        compiler_params=_cparams(("parallel", "parallel", "parallel")),
        name="diffattn",
    )(qa, ka, va, lam4, g_sub)


def _swa_kernel(sink_ref, q_ref, kv_ref, o_ref, *, tq, tw, seq):
    i = pl.program_id(1)
    ws = jnp.clip(i * tq - WINDOW, 0, seq - tw)
    ws = pl.multiple_of(ws, LANES)
    kv = kv_ref[0, pl.ds(ws, tw), :]
    k2 = kv[:, 0:LANES]
    v2 = kv[:, LANES:2 * LANES]
    qpos = i * tq + lax.broadcasted_iota(jnp.int32, (tq, tw), 0)
    kpos = ws + lax.broadcasted_iota(jnp.int32, (tq, tw), 1)
    mask = jnp.abs(kpos - qpos) <= WINDOW
    for h in range(SW_Q_HEADS):
        qh = q_ref[0, :, h * LANES:(h + 1) * LANES]
        s = lax.dot_general(qh, k2, (((1,), (1,)), ((), ())), preferred_element_type=F32)
        s = jnp.where(mask, s, MASK_VALUE)
        sink = sink_ref[h]
        m = jnp.maximum(jnp.max(s, axis=-1, keepdims=True), sink)
        e = jnp.exp(s - m)
        den = jnp.sum(e, axis=-1, keepdims=True) + jnp.exp(sink - m)
        pr = (e / den).astype(BF16)
        o_ref[0, :, h * LANES:(h + 1) * LANES] = jnp.dot(pr, v2, preferred_element_type=F32).astype(BF16)


def _swa(sink, qbx, kvb, tq):
    b, s, _ = qbx.shape
    tw = tq + 2 * WINDOW
    kern = functools.partial(_swa_kernel, tq=tq, tw=tw, seq=s)
    return pl.pallas_call(
        kern,
        grid_spec=pltpu.PrefetchScalarGridSpec(
            num_scalar_prefetch=1,
            grid=(b, s // tq),
            in_specs=[
                pl.BlockSpec((1, tq, SWX_W), lambda bi, i, sk: (bi, i, 0)),
                pl.BlockSpec((1, s, KVB_W), lambda bi, i, sk: (bi, 0, 0)),
            ],
            out_specs=pl.BlockSpec((1, tq, SWX_W), lambda bi, i, sk: (bi, i, 0)),
        ),
        out_shape=jax.ShapeDtypeStruct((b, s, SWX_W), BF16),
        compiler_params=_cparams(("parallel", "parallel")),
        name="swa",
    )(sink, qbx, kvb)


def _outproj_kernel(x_ref, oa_ref, obx_ref, woa_ref, wobx_ref, g_ref, wrt_ref, x1_ref, h2_ref, aff_ref):
    x1 = (x_ref[...] + jnp.dot(oa_ref[...], woa_ref[...], preferred_element_type=F32)
          + jnp.dot(obx_ref[...], wobx_ref[...], preferred_element_type=F32))
    x1_ref[...] = x1
    ms = jnp.mean(x1 * x1, axis=-1, keepdims=True)
    h2 = x1 * lax.rsqrt(ms + NORM_EPS) * g_ref[...]
    h2_ref[...] = h2
    logits = lax.dot_general(wrt_ref[...], h2, (((1,), (1,)), ((), ())),
                             preferred_element_type=F32, precision=lax.Precision.HIGHEST)
    mx = jnp.max(logits, axis=0, keepdims=True)
    e = jnp.exp(logits - mx)
    aff_ref[...] = e / jnp.sum(e, axis=0, keepdims=True)


def _outproj(x, oa, obx, w_oa, w_obx, g_ffn, w_rt, tm):
    n = x.shape[0]
    tok = lambda i: (i, 0)
    full = lambda i: (0, 0)
    return pl.pallas_call(
        _outproj_kernel,
        grid=(n // tm,),
        in_specs=[
            pl.BlockSpec((tm, D_MODEL), tok),
            pl.BlockSpec((tm, DA_W), tok),
            pl.BlockSpec((tm, SWX_W), tok),
            pl.BlockSpec((DA_W, D_MODEL), full),
            pl.BlockSpec((SWX_W, D_MODEL), full),
            pl.BlockSpec((1, D_MODEL), full),
            pl.BlockSpec((N_EXPERTS, D_MODEL), full),
        ],
        out_specs=[
            pl.BlockSpec((tm, D_MODEL), tok),
            pl.BlockSpec((tm, D_MODEL), tok),
            pl.BlockSpec((N_EXPERTS, tm), lambda i: (0, i)),
        ],
        out_shape=[
            jax.ShapeDtypeStruct((n, D_MODEL), F32),
            jax.ShapeDtypeStruct((n, D_MODEL), F32),
            jax.ShapeDtypeStruct((N_EXPERTS, n), F32),
        ],
        compiler_params=_cparams(("parallel",)),
        name="outproj",
    )(x, oa, obx, w_oa, w_obx, g_ffn, w_rt)


def _route_kernel(aff_ref, idx_ref, vals_ref, *, cap, nchunk, pblk):
    a = aff_ref[0]
    bits = pltpu.bitcast(a, jnp.int32)
    def count(mask):
        per_lane = jnp.sum(jnp.where(mask, 1.0, 0.0), axis=0, keepdims=True)
        return jnp.sum(per_lane, axis=1, keepdims=True)

    thr = jnp.zeros((1, 1), jnp.int32)
    for bit in range(30, -1, -1):
        cand = thr | (1 << bit)
        thr = jnp.where(count(bits >= cand) >= cap, cand, thr)
    gt = bits > thr
    eq = bits == thr
    need = cap - count(gt)

    r128 = lax.broadcasted_iota(jnp.int32, (LANES, LANES), 0)
    c128 = lax.broadcasted_iota(jnp.int32, (LANES, LANES), 1)
    u_incl = jnp.where(r128 <= c128, 1.0, 0.0).astype(BF16)
    ones = jnp.ones((LANES, LANES), BF16)
    rc = lax.broadcasted_iota(jnp.int32, (nchunk, nchunk), 0)
    cc = lax.broadcasted_iota(jnp.int32, (nchunk, nchunk), 1)
    l_strict = jnp.where(cc < rc, 1.0, 0.0).astype(BF16)
    u_strict = jnp.where(rc < cc, 1.0, 0.0).astype(BF16)

    def chunk_scan(mask_bf):
        incl = jnp.dot(mask_bf, u_incl, preferred_element_type=F32)
        tot_b = jnp.dot(mask_bf, ones, preferred_element_type=F32)
        cexcl_b = jnp.dot(l_strict, tot_b.astype(BF16), preferred_element_type=F32)
        return incl, cexcl_b

    eq_bf = jnp.where(eq, 1.0, 0.0).astype(BF16)
    incl_eq, cexcl_eq = chunk_scan(eq_bf)
    rank_eq = cexcl_eq + incl_eq - eq_bf.astype(F32)
    sel = gt | (eq & (rank_eq < need))
    sel_bf = jnp.where(sel, 1.0, 0.0).astype(BF16)
    lc, cexcl_b = chunk_scan(sel_bf)

    tot_row = lax.dot_general(jnp.ones((8, LANES), BF16), sel_bf, (((1,), (1,)), ((), ())),
                              preferred_element_type=F32)
    cexcl_row = jnp.dot(tot_row.astype(BF16), u_strict, preferred_element_type=F32)[0:1]
    cincl_row = cexcl_row + tot_row[0:1]

    a_hi = a.astype(BF16)
    r1 = a - a_hi.astype(F32)
    a_mid = r1.astype(BF16)
    a_lo = (r1 - a_mid.astype(F32)).astype(BF16)
    kidx = lax.broadcasted_iota(jnp.int32, (nchunk, LANES), 0).astype(F32)
    cex_hi = jnp.floor(cexcl_b * (1.0 / LANES))
    cex_lo = cexcl_b - cex_hi * LANES
    table = jnp.concatenate([lc.astype(BF16), a_hi, a_mid, a_lo, kidx.astype(BF16),
                             cex_hi.astype(BF16), cex_lo.astype(BF16)], axis=1)

    lane_f = lax.broadcasted_iota(jnp.int32, (pblk, LANES), 1).astype(F32)
    eye = r128 == c128

    def slot_block(bi, _):
        base = bi * pblk
        pc = (base + lax.broadcasted_iota(jnp.int32, (pblk, nchunk), 0)).astype(F32)
        onehot_k = jnp.where((cexcl_row <= pc) & (pc < cincl_row), 1.0, 0.0).astype(BF16)
        r = jnp.dot(onehot_k, table, preferred_element_type=F32)
        r_lc = r[:, 0:LANES]
        r_aff = (r[:, LANES:2 * LANES] + r[:, 2 * LANES:3 * LANES]) + r[:, 3 * LANES:4 * LANES]
        r_k = r[:, 4 * LANES:5 * LANES]
        r_cex = r[:, 5 * LANES:6 * LANES] * LANES + r[:, 6 * LANES:7 * LANES]
        p_loc = (base + lax.broadcasted_iota(jnp.int32, (pblk, LANES), 0)).astype(F32) - r_cex
        below = jnp.where(r_lc <= p_loc, 1.0, 0.0).astype(BF16)
        t_loc = jnp.dot(below, ones, preferred_element_type=F32)
        val = jnp.sum(jnp.where(lane_f == t_loc, r_aff, 0.0), axis=-1, keepdims=True)
        vals_ref[0, pl.ds(pl.multiple_of(base, pblk), pblk), :] = jnp.broadcast_to(val, (pblk, LANES))
        tok = r_k * LANES + t_loc
        for sb in range(pblk // LANES):
            blk = tok[sb * LANES:(sb + 1) * LANES]
            row = jnp.sum(jnp.where(eye, blk, 0.0), axis=0, keepdims=True)
            idx_ref[0, :, pl.ds(pl.multiple_of(base + sb * LANES, LANES), LANES)] = row.astype(jnp.int32)
        return 0

    lax.fori_loop(0, cap // pblk, slot_block, 0)


def _route(aff3, cap):
    e, nchunk, _ = aff3.shape
    pblk = min(512, cap)
    kern = functools.partial(_route_kernel, cap=cap, nchunk=nchunk, pblk=pblk)
    return pl.pallas_call(
        kern,
        grid=(e,),
        in_specs=[pl.BlockSpec((1, nchunk, LANES), lambda ei: (ei, 0, 0))],
        out_specs=[
            pl.BlockSpec((1, 1, cap), lambda ei: (ei, 0, 0)),
            pl.BlockSpec((1, cap, LANES), lambda ei: (ei, 0, 0)),
        ],
        out_shape=[
            jax.ShapeDtypeStruct((e, 1, cap), jnp.int32),
            jax.ShapeDtypeStruct((e, cap, LANES), F32),
        ],
        compiler_params=_cparams(("parallel",)),
        name="route",
    )(aff3)


def _expert_kernel(idx_ref, vals_ref, h2_hbm, xin_hbm, wg_ref, wu_ref, wd_ref, x_hbm,
                   xbuf, obuf, sem_x, sem_o, sem_s, *, tc):
    del xin_hbm

    def gather_x(r):
        return pltpu.make_async_copy(h2_hbm.at[pl.ds(idx_ref[0, 0, r], 1), :], xbuf.at[pl.ds(r, 1), :], sem_x)

    def gather_o(r):
        return pltpu.make_async_copy(x_hbm.at[pl.ds(idx_ref[0, 0, r], 1), :], obuf.at[pl.ds(r, 1), :], sem_o)

    def scatter_o(r):
        return pltpu.make_async_copy(obuf.at[pl.ds(r, 1), :], x_hbm.at[pl.ds(idx_ref[0, 0, r], 1), :], sem_s)

    def issue_gather(r, _):
        gather_x(r).start()
        gather_o(r).start()
        return 0

    lax.fori_loop(0, tc, issue_gather, 0)

    def wait_x(r, _):
        gather_x(r).wait()
        return 0

    lax.fori_loop(0, tc, wait_x, 0)
    x = xbuf[...].astype(BF16)
    g = jnp.dot(x, wg_ref[0], preferred_element_type=F32)
    u = jnp.dot(x, wu_ref[0], preferred_element_type=F32)
    hmid = (g * jax.nn.sigmoid(g) * u).astype(BF16)
    y = jnp.dot(hmid, wd_ref[0], preferred_element_type=F32)
    y = y * jnp.tile(vals_ref[0], (1, D_MODEL // LANES))

    def wait_o(r, _):
        gather_o(r).wait()
        return 0

    lax.fori_loop(0, tc, wait_o, 0)
    obuf[...] = obuf[...] + y

    def issue_scatter(r, _):
        scatter_o(r).start()
        return 0

    lax.fori_loop(0, tc, issue_scatter, 0)

    def wait_s(r, _):
        scatter_o(r).wait()
        return 0

    lax.fori_loop(0, tc, wait_s, 0)


def _experts(idx, vals_b, h2, x1, wg, wu, wd, tc):
    e, _, cap = idx.shape
    n = h2.shape[0]
    kern = functools.partial(_expert_kernel, tc=tc)
    wspec = pl.BlockSpec((1, D_MODEL, D_EXPERT), lambda ei, j: (ei, 0, 0))
    return pl.pallas_call(
        kern,
        grid=(e, cap // tc),
        in_specs=[
            pl.BlockSpec((1, 1, tc), lambda ei, j: (ei, 0, j), memory_space=pltpu.SMEM),
            pl.BlockSpec((1, tc, LANES), lambda ei, j: (ei, j, 0)),
            pl.BlockSpec(memory_space=pl.ANY),
            pl.BlockSpec(memory_space=pl.ANY),
            wspec, wspec,
            pl.BlockSpec((1, D_EXPERT, D_MODEL), lambda ei, j: (ei, 0, 0)),
        ],
        out_specs=pl.BlockSpec(memory_space=pl.ANY),
        out_shape=jax.ShapeDtypeStruct((n, D_MODEL), F32),
        scratch_shapes=[
            pltpu.VMEM((tc, D_MODEL), F32),
            pltpu.VMEM((tc, D_MODEL), F32),
            pltpu.SemaphoreType.DMA(()),
            pltpu.SemaphoreType.DMA(()),
            pltpu.SemaphoreType.DMA(()),
        ],
        input_output_aliases={3: 0},
        compiler_params=_cparams(("arbitrary", "arbitrary")),
        name="experts",
    )(idx, vals_b, h2, x1, wg, wu, wd)


def _ple_kernel(x_ref, p_ref, g_ref, wg_ref, wp_ref, o_ref):
    x = x_ref[...]
    ms = jnp.mean(x * x, axis=-1, keepdims=True)
    hn = (x * lax.rsqrt(ms + NORM_EPS) * g_ref[...]).astype(BF16)
    gate = jax.nn.sigmoid(jnp.dot(hn, wg_ref[...], preferred_element_type=F32))
    emb = jnp.dot(p_ref[0].astype(BF16), wp_ref[...], preferred_element_type=F32)
    o_ref[...] = x + gate * emb


def _ple(x, p_all, layer, g_ple, w_g, w_p, tm):
    n = x.shape[0]
    tok = lambda i: (i, 0)
    full = lambda i: (0, 0)
    return pl.pallas_call(
        _ple_kernel,
        grid=(n // tm,),
        in_specs=[
            pl.BlockSpec((tm, D_MODEL), tok),
            pl.BlockSpec((1, tm, PLE_DIM), lambda i: (layer, i, 0)),
            pl.BlockSpec((1, D_MODEL), full),
            pl.BlockSpec((D_MODEL, D_MODEL), full),
            pl.BlockSpec((PLE_DIM, D_MODEL), full),
        ],
        out_specs=pl.BlockSpec((tm, D_MODEL), tok),
        out_shape=jax.ShapeDtypeStruct((n, D_MODEL), F32),
        compiler_params=_cparams(("parallel",)),
        name="ple",
    )(x, p_all, g_ple, w_g, w_p)


def _rope_tables(seq):
    pos = jnp.arange(seq, dtype=F32)
    inv = ROPE_THETA ** (-jnp.arange(0, HEAD_DIM, 2, dtype=F32) / HEAD_DIM)
    ang = pos[:, None] * inv[None, :]
    ang = jnp.concatenate([ang, ang, ang, ang], axis=-1)
    cos, sin = jnp.cos(ang), jnp.sin(ang)
    first_half = (jnp.arange(LANES) % HEAD_DIM) < HEAD_DIM // 2
    sin_a = jnp.where(first_half[None, :], -sin, 0.0)
    sin_b = jnp.where(first_half[None, :], 0.0, sin)
    return cos, sin_a, sin_b


def _expand_sw_cols(w):
    lead = w.shape[:-1]
    w = w.reshape(lead + (SW_Q_HEADS, HEAD_DIM))
    z = jnp.zeros_like(w)
    kvh = (jnp.arange(SW_Q_HEADS) // SW_GROUP)[:, None]
    lo = jnp.where(kvh == 0, w, z)
    hi = jnp.where(kvh == 1, w, z)
    return jnp.concatenate([lo, hi], axis=-1).reshape(lead + (SWX_W,))


def _layer_params(i, attn_norm, w_in, q_norm_a, k_norm_a, subln_a, q_norm_b, k_norm_b, w_out):
    wi = w_in[i]
    w_qa, w_ka, w_va = wi[:, 0:512], wi[:, 512:1024], wi[:, 1024:1536]
    w_qb, w_kb, w_vb = wi[:, 1536:2048], wi[:, 2048:2176], wi[:, 2176:2304]
    w = jnp.concatenate([w_qa, w_ka, _expand_sw_cols(w_qb), w_kb, w_vb, w_va], axis=1).astype(BF16)
    scale = HEAD_DIM ** -0.5
    t2 = lambda g, n: jnp.tile(g, n)
    gain = jnp.concatenate([
        t2(q_norm_a[i], 8) * scale, t2(k_norm_a[i], 8), t2(q_norm_b[i], 16) * scale, t2(k_norm_b[i], 2),
        jnp.ones((NX - COL_KVB - LANES,), F32)])[None, :]
    wo = w_out[i]
    w_oa = wo[0:DA_W].astype(BF16)
    w_obx = _expand_sw_cols(wo[DA_W:].T).T.astype(BF16)
    lam_init = 0.8 - 0.6 * math.exp(-0.3 * i)
    return dict(g_attn=attn_norm[i][None, :], w=w, gain=gain, w_oa=w_oa, w_obx=w_obx,
                g_sub=subln_a[i][None, :], lam_init=lam_init)


def _pick(n, pref):
    t = pref
    while n % t:
        t //= 2
    return t


def _encoder(x, p, attn_norm, w_in, q_norm_a, k_norm_a, lambda_q1, lambda_k1, lambda_q2, lambda_k2,
             subln_a, q_norm_b, k_norm_b, sink_b, w_out, ffn_norm, w_router, w_gate_e, w_up_e, w_down_e,
             ple_norm, w_ple_gate, w_ple_proj):
    b, s, _ = x.shape
    n = b * s
    cap = EC_CAPACITY_FACTOR * n // N_EXPERTS
    tm = _pick(s, 512)
    tq = _pick(s, 256)
    tk = _pick(s, 512)
    tq_sw = _pick(s, 512)
    while tq_sw + 2 * WINDOW > s:
        tq_sw //= 2
    tc = _pick(cap, 512)
    cos, sin_a, sin_b = _rope_tables(s)
    r = lax.broadcasted_iota(jnp.int32, (256, 256), 0) // HEAD_DIM
    c = lax.broadcasted_iota(jnp.int32, (256, 256), 1) // HEAD_DIM
    bd = (r == c).astype(BF16)
    xf = x.reshape(n, D_MODEL)
    p_all = p.reshape(DEPTH, n, PLE_DIM)
    for i in range(DEPTH):
        lp = _layer_params(i, attn_norm, w_in, q_norm_a, k_norm_a, subln_a, q_norm_b, k_norm_b, w_out)
        qa, ka, qbx, kvb, va = _inproj(xf, lp["g_attn"], lp["w"], lp["gain"], cos, sin_a, sin_b, bd, s, tm)
        lam4 = jnp.stack([lambda_q1[i], lambda_k1[i], lambda_q2[i], lambda_k2[i]]).astype(F32)
        oa = _diffattn(qa.reshape(b, s, DA_W), ka.reshape(b, s, DA_W), va.reshape(b, s, DA_W), lam4,
                       lp["g_sub"], lp["lam_init"], tq, tk)
        obx = _swa(sink_b[i].astype(F32), qbx.reshape(b, s, SWX_W), kvb.reshape(b, s, KVB_W), tq_sw)
        x1, h2, aff_t = _outproj(xf, oa.reshape(n, DA_W), obx.reshape(n, SWX_W), lp["w_oa"], lp["w_obx"],
                                 ffn_norm[i][None, :], w_router[i].T, tm)
        idx, vals_b = _route(aff_t.reshape(N_EXPERTS, n // LANES, LANES), cap)
        x2 = _experts(idx, vals_b, h2, x1, w_gate_e[i].astype(BF16), w_up_e[i].astype(BF16),
                      w_down_e[i].astype(BF16), tc)
        xf = _ple(x2, p_all, i, ple_norm[i][None, :], w_ple_gate[i].astype(BF16),
                  w_ple_proj[i].astype(BF16), tm)
    return xf.reshape(b, s, D_MODEL)


def kernel(x_prompt, x_sample, p_prompt, p_sample, attn_norm, w_in, q_norm_a, k_norm_a, lambda_q1, lambda_k1,
           lambda_q2, lambda_k2, subln_a, q_norm_b, k_norm_b, sink_b, w_out, ffn_norm, w_router, w_gate_e,
           w_up_e, w_down_e, ple_norm, w_ple_gate, w_ple_proj):
    ws = (attn_norm, w_in, q_norm_a, k_norm_a, lambda_q1, lambda_k1, lambda_q2, lambda_k2, subln_a, q_norm_b,
          k_norm_b, sink_b, w_out, ffn_norm, w_router, w_gate_e, w_up_e, w_down_e, ple_norm, w_ple_gate,
          w_ple_proj)
    return (_encoder(x_prompt, p_prompt, *ws), _encoder(x_sample, p_sample, *ws))
```

```python
import functools
import math

import jax
import jax.numpy as jnp
from jax import lax
from jax.experimental import pallas as pl
from jax.experimental.pallas import tpu as pltpu

F32 = jnp.float32
BF16 = jnp.bfloat16

D_MODEL = 1024
DEPTH = 4
HEAD_DIM = 64
DA_HEADS = 4
SW_Q_HEADS = 8
SW_KV_HEADS = 2
SW_GROUP = SW_Q_HEADS // SW_KV_HEADS
WINDOW = 128
N_EXPERTS = 16
EC_CAPACITY_FACTOR = 2
D_EXPERT = 1024
PLE_DIM = 256
ROPE_THETA = 10000.0
NORM_EPS = 1e-6
MASK_VALUE = -1e30

LANES = 128
DA_W = DA_HEADS * 2 * HEAD_DIM
SWX_W = SW_Q_HEADS * LANES
KVB_W = 2 * LANES
COL_QA, COL_KA, COL_QBX, COL_KVB = 0, 512, 1024, 2048
NX = 2304
N_ROPE_TILES = (COL_KVB + LANES) // LANES
VMEM_LIMIT = 48 * 1024 * 1024


def _cparams(sem):
    return pltpu.CompilerParams(dimension_semantics=sem, vmem_limit_bytes=VMEM_LIMIT)


def _inproj_kernel(x_ref, gat_ref, w_ref, wvt_ref, gain_ref, cos_ref, sa_ref, sb_ref, bd_ref,
                   qa_ref, ka_ref, qbx_ref, kvb_ref, vat_ref, xn_ref):
    x = x_ref[...]
    ms = jnp.mean(x * x, axis=-1, keepdims=True)
    xn_ref[...] = (x * lax.rsqrt(ms + NORM_EPS) * gat_ref[...]).astype(BF16)
    cos = cos_ref[...]
    sa = sa_ref[...]
    sb = sb_ref[...]
    outs = ((qa_ref, COL_QA), (ka_ref, COL_KA), (qbx_ref, COL_QBX), (kvb_ref, COL_KVB))
    vat_ref[0] = lax.dot_general(wvt_ref[...], xn_ref[...], (((1,), (1,)), ((), ())),
                                 preferred_element_type=F32).astype(BF16)

    def out_for(col):
        for ref, base in reversed(outs):
            if col >= base:
                return ref, col - base
        raise AssertionError

    for c in range(NX // 256):
        c0 = c * 256
        p = jnp.dot(xn_ref[...], w_ref[:, c0:c0 + 256], preferred_element_type=F32)
        normed = [(c0 + t * LANES) // LANES < N_ROPE_TILES for t in range(2)]
        if any(normed):
            ss = jnp.dot((p * p).astype(BF16), bd_ref[...], preferred_element_type=F32)
        for t in range(2):
            col = c0 + t * LANES
            y = p[:, t * LANES:(t + 1) * LANES]
            if normed[t]:
                sst = ss[:, t * LANES:(t + 1) * LANES]
                y = y * lax.rsqrt(sst * (1.0 / HEAD_DIM) + NORM_EPS) * gain_ref[:, col:col + LANES]
                y = y * cos + pltpu.roll(y, 96, 1) * sa + pltpu.roll(y, 32, 1) * sb
            ref, off = out_for(col)
            ref[:, off:off + LANES] = y.astype(BF16)


def _inproj(x, g_attn, w, wvt, gain, cos, sa, sb, bd, seq, tm):
    n = x.shape[0]
    nblk_s = seq // tm
    tok = lambda i: (i, 0)
    full = lambda i: (0, 0)
    rope = lambda i: (i % nblk_s, 0)
    return pl.pallas_call(
        _inproj_kernel,
        grid=(n // tm,),
        in_specs=[
            pl.BlockSpec((tm, D_MODEL), tok),
            pl.BlockSpec((1, D_MODEL), full),
            pl.BlockSpec((D_MODEL, NX), full),
            pl.BlockSpec((DA_W, D_MODEL), full),
            pl.BlockSpec((1, NX), full),
            pl.BlockSpec((tm, LANES), rope),
            pl.BlockSpec((tm, LANES), rope),
            pl.BlockSpec((tm, LANES), rope),
            pl.BlockSpec((256, 256), full),
        ],
        out_specs=[
            pl.BlockSpec((tm, DA_W), tok),
            pl.BlockSpec((tm, DA_W), tok),
            pl.BlockSpec((tm, SWX_W), tok),
            pl.BlockSpec((tm, KVB_W), tok),
            pl.BlockSpec((1, DA_W, tm), lambda i: (i // nblk_s, 0, i % nblk_s)),
        ],
        out_shape=[
            jax.ShapeDtypeStruct((n, DA_W), BF16),
            jax.ShapeDtypeStruct((n, DA_W), BF16),
            jax.ShapeDtypeStruct((n, SWX_W), BF16),
            jax.ShapeDtypeStruct((n, KVB_W), BF16),
            jax.ShapeDtypeStruct((n // seq, DA_W, seq), BF16),
        ],
        scratch_shapes=[pltpu.VMEM((tm, D_MODEL), BF16)],
        compiler_params=_cparams(("parallel",)),
        name="inproj",
    )(x, g_attn, w, wvt, gain, cos, sa, sb, bd)


def _diffattn_kernel(q_ref, k_ref, vt_ref, lam4_ref, g_ref, o_ref, m_ref, l_ref, acc_ref, qst_ref, s_ref,
                     *, tq, tk, nk, lam_init):
    qt = q_ref[0].astype(F32).T
    feat = lax.broadcasted_iota(jnp.int32, qt.shape, 0)
    zero = jnp.zeros_like(qt)
    qst_ref[...] = jnp.concatenate([jnp.where(feat < HEAD_DIM, qt, zero), jnp.where(feat >= HEAD_DIM, qt, zero)],
                                   axis=1).astype(BF16)
    m_ref[...] = jnp.full(m_ref.shape, -jnp.inf, F32)
    l_ref[...] = jnp.zeros(l_ref.shape, F32)
    acc_ref[...] = jnp.zeros(acc_ref.shape, F32)

    def scores(j):
        k0 = pl.multiple_of(j * tk, tk)
        return jnp.dot(k_ref[0, pl.ds(k0, tk), :], qst_ref[...], preferred_element_type=F32)

    def accumulate(s, j):
        k0 = pl.multiple_of(j * tk, tk)
        vtj = vt_ref[0, :, pl.ds(k0, tk)]
        m_old = m_ref[...]
        m_new = jnp.maximum(m_old, jnp.max(s, axis=0, keepdims=True))
        alpha = jnp.exp2(m_old - m_new)
        p = jnp.exp2(s - m_new)
        l_ref[...] = alpha * l_ref[...] + jnp.sum(p, axis=0, keepdims=True)
        acc_ref[...] = alpha * acc_ref[...] + jnp.dot(vtj, p.astype(BF16), preferred_element_type=F32)
        m_ref[...] = m_new

    s_ref[0] = scores(0)

    def body(jj, _):
        j0 = 2 * jj
        s_ref[1] = scores(j0 + 1)
        accumulate(s_ref[0], j0)
        s_ref[0] = scores(jnp.minimum(j0 + 2, nk - 1))
        accumulate(s_ref[1], j0 + 1)
        return 0

    lax.fori_loop(0, nk // 2, body, 0)
    ot = acc_ref[...] / l_ref[...]
    lam4 = lam4_ref[...]
    lam = (jnp.exp(jnp.sum(lam4[0:1] * lam4[1:2], axis=-1, keepdims=True))
           - jnp.exp(jnp.sum(lam4[2:3] * lam4[3:4], axis=-1, keepdims=True)) + lam_init)
    o = (ot[:, :tq] - lam * ot[:, tq:]).T
    ms = jnp.mean(o * o, axis=-1, keepdims=True)
    o = o * lax.rsqrt(ms + NORM_EPS) * g_ref[...] * (1.0 - lam_init)
    o_ref[0] = o.astype(BF16)


def _diffattn(qa, ka, vat, lam4, g_sub, lam_init, tq, tk):
    b, s, _ = qa.shape
    kern = functools.partial(_diffattn_kernel, tq=tq, tk=tk, nk=s // tk, lam_init=lam_init)
    return pl.pallas_call(
        kern,
        grid=(b, DA_HEADS, s // tq),
        in_specs=[
            pl.BlockSpec((1, tq, LANES), lambda bi, h, i: (bi, i, h)),
            pl.BlockSpec((1, s, LANES), lambda bi, h, i: (bi, 0, h)),
            pl.BlockSpec((1, LANES, s), lambda bi, h, i: (bi, h, 0)),
            pl.BlockSpec((4, HEAD_DIM), lambda bi, h, i: (0, 0)),
            pl.BlockSpec((1, LANES), lambda bi, h, i: (0, 0)),
        ],
        out_specs=pl.BlockSpec((1, tq, LANES), lambda bi, h, i: (bi, i, h)),
        out_shape=jax.ShapeDtypeStruct((b, s, DA_W), BF16),
        scratch_shapes=[
            pltpu.VMEM((1, 2 * tq), F32),
            pltpu.VMEM((1, 2 * tq), F32),
            pltpu.VMEM((LANES, 2 * tq), F32),
            pltpu.VMEM((LANES, 2 * tq), BF16),
            pltpu.VMEM((2, tk, 2 * tq), F32),
        ],
        compiler_params=_cparams(("parallel", "parallel", "parallel")),
        name="diffattn",
    )(qa, ka, vat, lam4, g_sub)


def _swa_kernel(sink_ref, q_ref, kv_ref, o_ref, *, tq, tw, seq):
    i = pl.program_id(1)
    ws = jnp.clip(i * tq - WINDOW, 0, seq - tw)
    ws = pl.multiple_of(ws, LANES)
    kv = kv_ref[0, pl.ds(ws, tw), :]
    k2 = kv[:, 0:LANES]
    v2 = kv[:, LANES:2 * LANES]
    qpos = i * tq + lax.broadcasted_iota(jnp.int32, (tq, tw), 0)
    kpos = ws + lax.broadcasted_iota(jnp.int32, (tq, tw), 1)
    mask = jnp.abs(kpos - qpos) <= WINDOW
    for h in range(SW_Q_HEADS):
        qh = q_ref[0, :, h * LANES:(h + 1) * LANES]
        s = lax.dot_general(qh, k2, (((1,), (1,)), ((), ())), preferred_element_type=F32)
        s = jnp.where(mask, s, MASK_VALUE)
        sink = sink_ref[h]
        m = jnp.maximum(jnp.max(s, axis=-1, keepdims=True), sink)
        e = jnp.exp(s - m)
        den = jnp.sum(e, axis=-1, keepdims=True) + jnp.exp(sink - m)
        pr = (e / den).astype(BF16)
        o_ref[0, :, h * LANES:(h + 1) * LANES] = jnp.dot(pr, v2, preferred_element_type=F32).astype(BF16)


def _swa(sink, qbx, kvb, tq):
    b, s, _ = qbx.shape
    tw = tq + 2 * WINDOW
    kern = functools.partial(_swa_kernel, tq=tq, tw=tw, seq=s)
    return pl.pallas_call(
        kern,
        grid_spec=pltpu.PrefetchScalarGridSpec(
            num_scalar_prefetch=1,
            grid=(b, s // tq),
            in_specs=[
                pl.BlockSpec((1, tq, SWX_W), lambda bi, i, sk: (bi, i, 0)),
                pl.BlockSpec((1, s, KVB_W), lambda bi, i, sk: (bi, 0, 0)),
            ],
            out_specs=pl.BlockSpec((1, tq, SWX_W), lambda bi, i, sk: (bi, i, 0)),
        ),
        out_shape=jax.ShapeDtypeStruct((b, s, SWX_W), BF16),
        compiler_params=_cparams(("parallel", "parallel")),
        name="swa",
    )(sink, qbx, kvb)


def _outproj_kernel(x_ref, oa_ref, obx_ref, woa_ref, wobx_ref, g_ref, wrt_ref, x1_ref, h2_ref, aff_ref):
    x1 = (x_ref[...] + jnp.dot(oa_ref[...], woa_ref[...], preferred_element_type=F32)
          + jnp.dot(obx_ref[...], wobx_ref[...], preferred_element_type=F32))
    x1_ref[...] = x1
    ms = jnp.mean(x1 * x1, axis=-1, keepdims=True)
    h2 = x1 * lax.rsqrt(ms + NORM_EPS) * g_ref[...]
    h2_ref[...] = h2
    logits = lax.dot_general(wrt_ref[...], h2, (((1,), (1,)), ((), ())),
                             preferred_element_type=F32, precision=lax.Precision.HIGHEST)
    mx = jnp.max(logits, axis=0, keepdims=True)
    e = jnp.exp(logits - mx)
    aff_ref[...] = e / jnp.sum(e, axis=0, keepdims=True)


def _outproj(x, oa, obx, w_oa, w_obx, g_ffn, w_rt, tm):
    n = x.shape[0]
    tok = lambda i: (i, 0)
    full = lambda i: (0, 0)
    return pl.pallas_call(
        _outproj_kernel,
        grid=(n // tm,),
        in_specs=[
            pl.BlockSpec((tm, D_MODEL), tok),
            pl.BlockSpec((tm, DA_W), tok),
            pl.BlockSpec((tm, SWX_W), tok),
            pl.BlockSpec((DA_W, D_MODEL), full),
            pl.BlockSpec((SWX_W, D_MODEL), full),
            pl.BlockSpec((1, D_MODEL), full),
            pl.BlockSpec((N_EXPERTS, D_MODEL), full),
        ],
        out_specs=[
            pl.BlockSpec((tm, D_MODEL), tok),
            pl.BlockSpec((tm, D_MODEL), tok),
            pl.BlockSpec((N_EXPERTS, tm), lambda i: (0, i)),
        ],
        out_shape=[
            jax.ShapeDtypeStruct((n, D_MODEL), F32),
            jax.ShapeDtypeStruct((n, D_MODEL), F32),
            jax.ShapeDtypeStruct((N_EXPERTS, n), F32),
        ],
        compiler_params=_cparams(("parallel",)),
        name="outproj",
    )(x, oa, obx, w_oa, w_obx, g_ffn, w_rt)


def _route_kernel(aff_ref, idx_ref, vals_ref, *, cap, nchunk, pblk):
    a = aff_ref[0]
    bits = pltpu.bitcast(a, jnp.int32)
    def count(mask):
        per_lane = jnp.sum(jnp.where(mask, 1.0, 0.0), axis=0, keepdims=True)
        return jnp.sum(per_lane, axis=1, keepdims=True)

    thr = jnp.zeros((1, 1), jnp.int32)
    for bit in range(30, -1, -1):
        cand = thr | (1 << bit)
        thr = jnp.where(count(bits >= cand) >= cap, cand, thr)
    gt = bits > thr
    eq = bits == thr
    need = cap - count(gt)

    r128 = lax.broadcasted_iota(jnp.int32, (LANES, LANES), 0)
    c128 = lax.broadcasted_iota(jnp.int32, (LANES, LANES), 1)
    u_incl = jnp.where(r128 <= c128, 1.0, 0.0).astype(BF16)
    ones = jnp.ones((LANES, LANES), BF16)
    rc = lax.broadcasted_iota(jnp.int32, (nchunk, nchunk), 0)
    cc = lax.broadcasted_iota(jnp.int32, (nchunk, nchunk), 1)
    l_strict = jnp.where(cc < rc, 1.0, 0.0).astype(BF16)
    u_strict = jnp.where(rc < cc, 1.0, 0.0).astype(BF16)

    def chunk_scan(mask_bf):
        incl = jnp.dot(mask_bf, u_incl, preferred_element_type=F32)
        tot_b = jnp.dot(mask_bf, ones, preferred_element_type=F32)
        cexcl_b = jnp.dot(l_strict, tot_b.astype(BF16), preferred_element_type=F32)
        return incl, cexcl_b

    eq_bf = jnp.where(eq, 1.0, 0.0).astype(BF16)
    incl_eq, cexcl_eq = chunk_scan(eq_bf)
    rank_eq = cexcl_eq + incl_eq - eq_bf.astype(F32)
    sel = gt | (eq & (rank_eq < need))
    sel_bf = jnp.where(sel, 1.0, 0.0).astype(BF16)
    lc, cexcl_b = chunk_scan(sel_bf)

    tot_row = lax.dot_general(jnp.ones((8, LANES), BF16), sel_bf, (((1,), (1,)), ((), ())),
                              preferred_element_type=F32)
    cexcl_row = jnp.dot(tot_row.astype(BF16), u_strict, preferred_element_type=F32)[0:1]
    cincl_row = cexcl_row + tot_row[0:1]

    a_hi = a.astype(BF16)
    r1 = a - a_hi.astype(F32)
    a_mid = r1.astype(BF16)
    a_lo = (r1 - a_mid.astype(F32)).astype(BF16)
    kidx = lax.broadcasted_iota(jnp.int32, (nchunk, LANES), 0).astype(F32)
    cex_hi = jnp.floor(cexcl_b * (1.0 / LANES))
    cex_lo = cexcl_b - cex_hi * LANES
    table = jnp.concatenate([lc.astype(BF16), a_hi, a_mid, a_lo, kidx.astype(BF16),
                             cex_hi.astype(BF16), cex_lo.astype(BF16)], axis=1)

    lane_f = lax.broadcasted_iota(jnp.int32, (pblk, LANES), 1).astype(F32)
    eye = r128 == c128

    def slot_block(bi, _):
        base = bi * pblk
        pc = (base + lax.broadcasted_iota(jnp.int32, (pblk, nchunk), 0)).astype(F32)
        onehot_k = jnp.where((cexcl_row <= pc) & (pc < cincl_row), 1.0, 0.0).astype(BF16)
        r = jnp.dot(onehot_k, table, preferred_element_type=F32)
        r_lc = r[:, 0:LANES]
        r_aff = (r[:, LANES:2 * LANES] + r[:, 2 * LANES:3 * LANES]) + r[:, 3 * LANES:4 * LANES]
        r_k = r[:, 4 * LANES:5 * LANES]
        r_cex = r[:, 5 * LANES:6 * LANES] * LANES + r[:, 6 * LANES:7 * LANES]
        p_loc = (base + lax.broadcasted_iota(jnp.int32, (pblk, LANES), 0)).astype(F32) - r_cex
        below = jnp.where(r_lc <= p_loc, 1.0, 0.0).astype(BF16)
        t_loc = jnp.dot(below, ones, preferred_element_type=F32)
        val = jnp.sum(jnp.where(lane_f == t_loc, r_aff, 0.0), axis=-1, keepdims=True)
        vals_ref[0, pl.ds(pl.multiple_of(base, pblk), pblk), :] = jnp.broadcast_to(val, (pblk, LANES))
        tok = r_k * LANES + t_loc
        for sb in range(pblk // LANES):
            blk = tok[sb * LANES:(sb + 1) * LANES]
            row = jnp.sum(jnp.where(eye, blk, 0.0), axis=0, keepdims=True)
            idx_ref[0, :, pl.ds(pl.multiple_of(base + sb * LANES, LANES), LANES)] = row.astype(jnp.int32)
        return 0

    lax.fori_loop(0, cap // pblk, slot_block, 0)


def _route(aff3, cap):
    e, nchunk, _ = aff3.shape
    pblk = min(512, cap)
    kern = functools.partial(_route_kernel, cap=cap, nchunk=nchunk, pblk=pblk)
    return pl.pallas_call(
        kern,
        grid=(e,),
        in_specs=[pl.BlockSpec((1, nchunk, LANES), lambda ei: (ei, 0, 0))],
        out_specs=[
            pl.BlockSpec((1, 1, cap), lambda ei: (ei, 0, 0)),
            pl.BlockSpec((1, cap, LANES), lambda ei: (ei, 0, 0)),
        ],
        out_shape=[
            jax.ShapeDtypeStruct((e, 1, cap), jnp.int32),
            jax.ShapeDtypeStruct((e, cap, LANES), F32),
        ],
        compiler_params=_cparams(("parallel",)),
        name="route",
    )(aff3)


def _expert_kernel(idx_ref, vals_ref, h2_hbm, xin_hbm, wg_ref, wu_ref, wd_ref, x_hbm,
                   xbuf, obuf, sem_x, sem_o, sem_s, *, tc):
    del xin_hbm

    def gather_x(r):
        return pltpu.make_async_copy(h2_hbm.at[pl.ds(idx_ref[0, 0, r], 1), :], xbuf.at[pl.ds(r, 1), :], sem_x)

    def gather_o(r):
        return pltpu.make_async_copy(x_hbm.at[pl.ds(idx_ref[0, 0, r], 1), :], obuf.at[pl.ds(r, 1), :], sem_o)

    def scatter_o(r):
        return pltpu.make_async_copy(obuf.at[pl.ds(r, 1), :], x_hbm.at[pl.ds(idx_ref[0, 0, r], 1), :], sem_s)

    def issue_gather(r, _):
        gather_x(r).start()
        gather_o(r).start()
        return 0

    lax.fori_loop(0, tc, issue_gather, 0)

    def wait_x(r, _):
        gather_x(r).wait()
        return 0

    lax.fori_loop(0, tc, wait_x, 0)
    x = xbuf[...].astype(BF16)
    g = jnp.dot(x, wg_ref[0], preferred_element_type=F32)
    u = jnp.dot(x, wu_ref[0], preferred_element_type=F32)
    hmid = (g * jax.nn.sigmoid(g) * u).astype(BF16)
    y = jnp.dot(hmid, wd_ref[0], preferred_element_type=F32)
    y = y * jnp.tile(vals_ref[0], (1, D_MODEL // LANES))

    def wait_o(r, _):
        gather_o(r).wait()
        return 0

    lax.fori_loop(0, tc, wait_o, 0)
    obuf[...] = obuf[...] + y

    def issue_scatter(r, _):
        scatter_o(r).start()
        return 0

    lax.fori_loop(0, tc, issue_scatter, 0)

    def wait_s(r, _):
        scatter_o(r).wait()
        return 0

    lax.fori_loop(0, tc, wait_s, 0)


def _experts(idx, vals_b, h2, x1, wg, wu, wd, tc):
    e, _, cap = idx.shape
    n = h2.shape[0]
    kern = functools.partial(_expert_kernel, tc=tc)
    wspec = pl.BlockSpec((1, D_MODEL, D_EXPERT), lambda ei, j: (ei, 0, 0))
    return pl.pallas_call(
        kern,
        grid=(e, cap // tc),
        in_specs=[
            pl.BlockSpec((1, 1, tc), lambda ei, j: (ei, 0, j), memory_space=pltpu.SMEM),
            pl.BlockSpec((1, tc, LANES), lambda ei, j: (ei, j, 0)),
            pl.BlockSpec(memory_space=pl.ANY),
            pl.BlockSpec(memory_space=pl.ANY),
            wspec, wspec,
            pl.BlockSpec((1, D_EXPERT, D_MODEL), lambda ei, j: (ei, 0, 0)),
        ],
        out_specs=pl.BlockSpec(memory_space=pl.ANY),
        out_shape=jax.ShapeDtypeStruct((n, D_MODEL), F32),
        scratch_shapes=[
            pltpu.VMEM((tc, D_MODEL), F32),
            pltpu.VMEM((tc, D_MODEL), F32),
            pltpu.SemaphoreType.DMA(()),
            pltpu.SemaphoreType.DMA(()),
            pltpu.SemaphoreType.DMA(()),
        ],
        input_output_aliases={3: 0},
        compiler_params=_cparams(("arbitrary", "arbitrary")),
        name="experts",
    )(idx, vals_b, h2, x1, wg, wu, wd)


def _ple_kernel(x_ref, p_ref, g_ref, wg_ref, wp_ref, o_ref):
    x = x_ref[...]
    ms = jnp.mean(x * x, axis=-1, keepdims=True)
    hn = (x * lax.rsqrt(ms + NORM_EPS) * g_ref[...]).astype(BF16)
    gate = jax.nn.sigmoid(jnp.dot(hn, wg_ref[...], preferred_element_type=F32))
    emb = jnp.dot(p_ref[0].astype(BF16), wp_ref[...], preferred_element_type=F32)
    o_ref[...] = x + gate * emb


def _ple(x, p_all, layer, g_ple, w_g, w_p, tm):
    n = x.shape[0]
    tok = lambda i: (i, 0)
    full = lambda i: (0, 0)
    return pl.pallas_call(
        _ple_kernel,
        grid=(n // tm,),
        in_specs=[
            pl.BlockSpec((tm, D_MODEL), tok),
            pl.BlockSpec((1, tm, PLE_DIM), lambda i: (layer, i, 0)),
            pl.BlockSpec((1, D_MODEL), full),
            pl.BlockSpec((D_MODEL, D_MODEL), full),
            pl.BlockSpec((PLE_DIM, D_MODEL), full),
        ],
        out_specs=pl.BlockSpec((tm, D_MODEL), tok),
        out_shape=jax.ShapeDtypeStruct((n, D_MODEL), F32),
        compiler_params=_cparams(("parallel",)),
        name="ple",
    )(x, p_all, g_ple, w_g, w_p)


def _rope_tables(seq):
    pos = jnp.arange(seq, dtype=F32)
    inv = ROPE_THETA ** (-jnp.arange(0, HEAD_DIM, 2, dtype=F32) / HEAD_DIM)
    ang = pos[:, None] * inv[None, :]
    ang = jnp.concatenate([ang, ang, ang, ang], axis=-1)
    cos, sin = jnp.cos(ang), jnp.sin(ang)
    first_half = (jnp.arange(LANES) % HEAD_DIM) < HEAD_DIM // 2
    sin_a = jnp.where(first_half[None, :], -sin, 0.0)
    sin_b = jnp.where(first_half[None, :], 0.0, sin)
    return cos, sin_a, sin_b


def _expand_sw_cols(w):
    lead = w.shape[:-1]
    w = w.reshape(lead + (SW_Q_HEADS, HEAD_DIM))
    z = jnp.zeros_like(w)
    kvh = (jnp.arange(SW_Q_HEADS) // SW_GROUP)[:, None]
    lo = jnp.where(kvh == 0, w, z)
    hi = jnp.where(kvh == 1, w, z)
    return jnp.concatenate([lo, hi], axis=-1).reshape(lead + (SWX_W,))


def _layer_params(i, attn_norm, w_in, q_norm_a, k_norm_a, subln_a, q_norm_b, k_norm_b, w_out):
    wi = w_in[i]
    w_qa, w_ka, w_va = wi[:, 0:512], wi[:, 512:1024], wi[:, 1024:1536]
    w_qb, w_kb, w_vb = wi[:, 1536:2048], wi[:, 2048:2176], wi[:, 2176:2304]
    w = jnp.concatenate([w_qa, w_ka, _expand_sw_cols(w_qb), w_kb, w_vb], axis=1).astype(BF16)
    wvt = w_va.T.astype(BF16)
    scale = HEAD_DIM ** -0.5
    log2e = math.log2(math.e)
    t2 = lambda g, n: jnp.tile(g, n)
    gain = jnp.concatenate([
        t2(q_norm_a[i], 8) * (scale * log2e), t2(k_norm_a[i], 8), t2(q_norm_b[i], 16) * scale,
        t2(k_norm_b[i], 2), jnp.ones((NX - COL_KVB - LANES,), F32)])[None, :]
    wo = w_out[i]
    w_oa = wo[0:DA_W].astype(BF16)
    w_obx = _expand_sw_cols(wo[DA_W:].T).T.astype(BF16)
    lam_init = 0.8 - 0.6 * math.exp(-0.3 * i)
    return dict(g_attn=attn_norm[i][None, :], w=w, wvt=wvt, gain=gain, w_oa=w_oa, w_obx=w_obx,
                g_sub=subln_a[i][None, :], lam_init=lam_init)


def _pick(n, pref):
    t = pref
    while n % t:
        t //= 2
    return t


def _encoder(x, p, attn_norm, w_in, q_norm_a, k_norm_a, lambda_q1, lambda_k1, lambda_q2, lambda_k2,
             subln_a, q_norm_b, k_norm_b, sink_b, w_out, ffn_norm, w_router, w_gate_e, w_up_e, w_down_e,
             ple_norm, w_ple_gate, w_ple_proj):
    b, s, _ = x.shape
    n = b * s
    cap = EC_CAPACITY_FACTOR * n // N_EXPERTS
    tm = _pick(s, 512)
    tq = _pick(s, 256)
    tk = _pick(s // 2, 512)
    tq_sw = _pick(s, 512)
    while tq_sw + 2 * WINDOW > s:
        tq_sw //= 2
    tc = _pick(cap, 512)
    cos, sin_a, sin_b = _rope_tables(s)
    r = lax.broadcasted_iota(jnp.int32, (256, 256), 0) // HEAD_DIM
    c = lax.broadcasted_iota(jnp.int32, (256, 256), 1) // HEAD_DIM
    bd = (r == c).astype(BF16)
    xf = x.reshape(n, D_MODEL)
    p_all = p.reshape(DEPTH, n, PLE_DIM)
    for i in range(DEPTH):
        lp = _layer_params(i, attn_norm, w_in, q_norm_a, k_norm_a, subln_a, q_norm_b, k_norm_b, w_out)
        qa, ka, qbx, kvb, vat = _inproj(xf, lp["g_attn"], lp["w"], lp["wvt"], lp["gain"], cos, sin_a, sin_b, bd,
                                        s, tm)
        lam4 = jnp.stack([lambda_q1[i], lambda_k1[i], lambda_q2[i], lambda_k2[i]]).astype(F32)
        oa = _diffattn(qa.reshape(b, s, DA_W), ka.reshape(b, s, DA_W), vat, lam4,
                       lp["g_sub"], lp["lam_init"], tq, tk)
        obx = _swa(sink_b[i].astype(F32), qbx.reshape(b, s, SWX_W), kvb.reshape(b, s, KVB_W), tq_sw)
        x1, h2, aff_t = _outproj(xf, oa.reshape(n, DA_W), obx.reshape(n, SWX_W), lp["w_oa"], lp["w_obx"],
                                 ffn_norm[i][None, :], w_router[i].T, tm)
        idx, vals_b = _route(aff_t.reshape(N_EXPERTS, n // LANES, LANES), cap)
        x2 = _experts(idx, vals_b, h2, x1, w_gate_e[i].astype(BF16), w_up_e[i].astype(BF16),
                      w_down_e[i].astype(BF16), tc)
        xf = _ple(x2, p_all, i, ple_norm[i][None, :], w_ple_gate[i].astype(BF16),
                  w_ple_proj[i].astype(BF16), tm)
    return xf.reshape(b, s, D_MODEL)


def kernel(x_prompt, x_sample, p_prompt, p_sample, attn_norm, w_in, q_norm_a, k_norm_a, lambda_q1, lambda_k1,
           lambda_q2, lambda_k2, subln_a, q_norm_b, k_norm_b, sink_b, w_out, ffn_norm, w_router, w_gate_e,
           w_up_e, w_down_e, ple_norm, w_ple_gate, w_ple_proj):
    ws = (attn_norm, w_in, q_norm_a, k_norm_a, lambda_q1, lambda_k1, lambda_q2, lambda_k2, subln_a, q_norm_b,
          k_norm_b, sink_b, w_out, ffn_norm, w_router, w_gate_e, w_up_e, w_down_e, ple_norm, w_ple_gate,
          w_ple_proj)
    return (_encoder(x_prompt, p_prompt, *ws), _encoder(x_sample, p_sample, *ws))
```

```python
import functools
import math

import jax
import jax.numpy as jnp
from jax import lax
from jax.experimental import pallas as pl
from jax.experimental.pallas import tpu as pltpu

F32 = jnp.float32
BF16 = jnp.bfloat16

D_MODEL = 1024
DEPTH = 4
HEAD_DIM = 64
DA_HEADS = 4
SW_Q_HEADS = 8
SW_KV_HEADS = 2
SW_GROUP = SW_Q_HEADS // SW_KV_HEADS
WINDOW = 128
N_EXPERTS = 16
EC_CAPACITY_FACTOR = 2
D_EXPERT = 1024
PLE_DIM = 256
ROPE_THETA = 10000.0
NORM_EPS = 1e-6
MASK_VALUE = -1e30

LANES = 128
SUBLANES = 8
DA_W = DA_HEADS * 2 * HEAD_DIM
SWX_W = SW_Q_HEADS * LANES
KVB_W = 2 * LANES
COL_QA, COL_KA, COL_QBX, COL_KVB = 0, 512, 1024, 2048
NX = 2304
N_ROPE_TILES = (COL_KVB + LANES) // LANES
VMEM_LIMIT = 48 * 1024 * 1024
EXPERT_VMEM_LIMIT = 56 * 1024 * 1024


def _cparams(sem):
    return pltpu.CompilerParams(dimension_semantics=sem, vmem_limit_bytes=VMEM_LIMIT)


def _inproj_kernel(x_ref, gat_ref, w_ref, wvt_ref, gain_ref, cos_ref, sa_ref, sb_ref, bd_ref,
                   qa_ref, ka_ref, qbx_ref, kvb_ref, vat_ref, xn_ref):
    x = x_ref[...]
    ms = jnp.mean(x * x, axis=-1, keepdims=True)
    xn_ref[...] = (x * lax.rsqrt(ms + NORM_EPS) * gat_ref[...]).astype(BF16)
    cos = cos_ref[...]
    sa = sa_ref[...]
    sb = sb_ref[...]
    outs = ((qa_ref, COL_QA), (ka_ref, COL_KA), (qbx_ref, COL_QBX), (kvb_ref, COL_KVB))
    vat_ref[0] = lax.dot_general(wvt_ref[...], xn_ref[...], (((1,), (1,)), ((), ())),
                                 preferred_element_type=F32).astype(BF16)

    def out_for(col):
        for ref, base in reversed(outs):
            if col >= base:
                return ref, col - base
        raise AssertionError

    for c in range(NX // 256):
        c0 = c * 256
        p = jnp.dot(xn_ref[...], w_ref[:, c0:c0 + 256], preferred_element_type=F32)
        normed = [(c0 + t * LANES) // LANES < N_ROPE_TILES for t in range(2)]
        if any(normed):
            ss = jnp.dot((p * p).astype(BF16), bd_ref[...], preferred_element_type=F32)
        for t in range(2):
            col = c0 + t * LANES
            y = p[:, t * LANES:(t + 1) * LANES]
            if normed[t]:
                sst = ss[:, t * LANES:(t + 1) * LANES]
                y = y * lax.rsqrt(sst * (1.0 / HEAD_DIM) + NORM_EPS) * gain_ref[:, col:col + LANES]
                y = y * cos + pltpu.roll(y, 96, 1) * sa + pltpu.roll(y, 32, 1) * sb
            ref, off = out_for(col)
            ref[:, off:off + LANES] = y.astype(BF16)


def _inproj(x, g_attn, w, wvt, gain, cos, sa, sb, bd, seq, tm):
    n = x.shape[0]
    nblk_s = seq // tm
    tok = lambda i: (i, 0)
    full = lambda i: (0, 0)
    rope = lambda i: (i % nblk_s, 0)
    return pl.pallas_call(
        _inproj_kernel,
        grid=(n // tm,),
        in_specs=[
            pl.BlockSpec((tm, D_MODEL), tok),
            pl.BlockSpec((1, D_MODEL), full),
            pl.BlockSpec((D_MODEL, NX), full),
            pl.BlockSpec((DA_W, D_MODEL), full),
            pl.BlockSpec((1, NX), full),
            pl.BlockSpec((tm, LANES), rope),
            pl.BlockSpec((tm, LANES), rope),
            pl.BlockSpec((tm, LANES), rope),
            pl.BlockSpec((256, 256), full),
        ],
        out_specs=[
            pl.BlockSpec((tm, DA_W), tok),
            pl.BlockSpec((tm, DA_W), tok),
            pl.BlockSpec((tm, SWX_W), tok),
            pl.BlockSpec((tm, KVB_W), tok),
            pl.BlockSpec((1, DA_W, tm), lambda i: (i // nblk_s, 0, i % nblk_s)),
        ],
        out_shape=[
            jax.ShapeDtypeStruct((n, DA_W), BF16),
            jax.ShapeDtypeStruct((n, DA_W), BF16),
            jax.ShapeDtypeStruct((n, SWX_W), BF16),
            jax.ShapeDtypeStruct((n, KVB_W), BF16),
            jax.ShapeDtypeStruct((n // seq, DA_W, seq), BF16),
        ],
        scratch_shapes=[pltpu.VMEM((tm, D_MODEL), BF16)],
        compiler_params=_cparams(("parallel",)),
        name="inproj",
    )(x, g_attn, w, wvt, gain, cos, sa, sb, bd)


def _diffattn_kernel(q_ref, k_ref, vt_ref, lam4_ref, g_ref, o_ref, m_ref, l_ref, acc_ref, qst_ref, s_ref,
                     *, tq, tk, nk, lam_init):
    qt = q_ref[0].astype(F32).T
    feat = lax.broadcasted_iota(jnp.int32, qt.shape, 0)
    zero = jnp.zeros_like(qt)
    qst_ref[...] = jnp.concatenate([jnp.where(feat < HEAD_DIM, qt, zero), jnp.where(feat >= HEAD_DIM, qt, zero)],
                                   axis=1).astype(BF16)
    m_ref[...] = jnp.full(m_ref.shape, -jnp.inf, F32)
    l_ref[...] = jnp.zeros(l_ref.shape, F32)
    acc_ref[...] = jnp.zeros(acc_ref.shape, F32)

    def scores(j):
        k0 = pl.multiple_of(j * tk, tk)
        return jnp.dot(k_ref[0, pl.ds(k0, tk), :], qst_ref[...], preferred_element_type=F32)

    def accumulate(s, j):
        k0 = pl.multiple_of(j * tk, tk)
        vtj = vt_ref[0, :, pl.ds(k0, tk)]
        m_old = m_ref[...]
        m_new = jnp.maximum(m_old, jnp.max(s, axis=0, keepdims=True))
        alpha = jnp.exp2(m_old - m_new)
        p = jnp.exp2(s - m_new)
        l_ref[...] = alpha * l_ref[...] + jnp.sum(p, axis=0, keepdims=True)
        acc_ref[...] = alpha * acc_ref[...] + jnp.dot(vtj, p.astype(BF16), preferred_element_type=F32)
        m_ref[...] = m_new

    s_ref[0] = scores(0)

    def body(jj, _):
        j0 = 2 * jj
        s_ref[1] = scores(j0 + 1)
        accumulate(s_ref[0], j0)
        s_ref[0] = scores(jnp.minimum(j0 + 2, nk - 1))
        accumulate(s_ref[1], j0 + 1)
        return 0

    lax.fori_loop(0, nk // 2, body, 0)
    ot = acc_ref[...] / l_ref[...]
    lam4 = lam4_ref[...]
    lam = (jnp.exp(jnp.sum(lam4[0:1] * lam4[1:2], axis=-1, keepdims=True))
           - jnp.exp(jnp.sum(lam4[2:3] * lam4[3:4], axis=-1, keepdims=True)) + lam_init)
    o = (ot[:, :tq] - lam * ot[:, tq:]).T
    ms = jnp.mean(o * o, axis=-1, keepdims=True)
    o = o * lax.rsqrt(ms + NORM_EPS) * g_ref[...] * (1.0 - lam_init)
    o_ref[0] = o.astype(BF16)


def _diffattn(qa, ka, vat, lam4, g_sub, lam_init, tq, tk):
    b, s, _ = qa.shape
    kern = functools.partial(_diffattn_kernel, tq=tq, tk=tk, nk=s // tk, lam_init=lam_init)
    return pl.pallas_call(
        kern,
        grid=(b, DA_HEADS, s // tq),
        in_specs=[
            pl.BlockSpec((1, tq, LANES), lambda bi, h, i: (bi, i, h)),
            pl.BlockSpec((1, s, LANES), lambda bi, h, i: (bi, 0, h)),
            pl.BlockSpec((1, LANES, s), lambda bi, h, i: (bi, h, 0)),
            pl.BlockSpec((4, HEAD_DIM), lambda bi, h, i: (0, 0)),
            pl.BlockSpec((1, LANES), lambda bi, h, i: (0, 0)),
        ],
        out_specs=pl.BlockSpec((1, tq, LANES), lambda bi, h, i: (bi, i, h)),
        out_shape=jax.ShapeDtypeStruct((b, s, DA_W), BF16),
        scratch_shapes=[
            pltpu.VMEM((1, 2 * tq), F32),
            pltpu.VMEM((1, 2 * tq), F32),
            pltpu.VMEM((LANES, 2 * tq), F32),
            pltpu.VMEM((LANES, 2 * tq), BF16),
            pltpu.VMEM((2, tk, 2 * tq), F32),
        ],
        compiler_params=_cparams(("parallel", "parallel", "parallel")),
        name="diffattn",
    )(qa, ka, vat, lam4, g_sub)


def _swa_kernel(sink_ref, q_ref, kv_ref, o_ref, *, tq, seq):
    i = pl.program_id(1)
    kw = 3 * WINDOW
    for sb in range(tq // WINDOW):
        rows = slice(sb * WINDOW, (sb + 1) * WINDOW)
        q0 = i * tq + sb * WINDOW
        g0 = pl.multiple_of(jnp.clip(q0 - WINDOW, 0, seq - kw), WINDOW)
        kv = kv_ref[0, pl.ds(g0, kw), :]
        k2 = kv[:, 0:LANES]
        v2 = kv[:, LANES:2 * LANES]
        qpos = q0 + lax.broadcasted_iota(jnp.int32, (WINDOW, kw), 0)
        kpos = g0 + lax.broadcasted_iota(jnp.int32, (WINDOW, kw), 1)
        mask = jnp.abs(kpos - qpos) <= WINDOW
        heads = [slice(h * LANES, (h + 1) * LANES) for h in range(SW_Q_HEADS)]
        scores = [lax.dot_general(q_ref[0, rows, cols], k2, (((1,), (1,)), ((), ())), preferred_element_type=F32)
                  for cols in heads]
        for h, cols in enumerate(heads):
            s = jnp.where(mask, scores[h], MASK_VALUE)
            sink = sink_ref[h]
            m = jnp.maximum(jnp.max(s, axis=-1, keepdims=True), sink)
            e = jnp.exp(s - m)
            den = jnp.sum(e, axis=-1, keepdims=True) + jnp.exp(sink - m)
            pr = (e / den).astype(BF16)
            o_ref[0, rows, cols] = jnp.dot(pr, v2, preferred_element_type=F32).astype(BF16)


def _swa(sink, qbx, kvb, tq):
    b, s, _ = qbx.shape
    kern = functools.partial(_swa_kernel, tq=tq, seq=s)
    return pl.pallas_call(
        kern,
        grid_spec=pltpu.PrefetchScalarGridSpec(
            num_scalar_prefetch=1,
            grid=(b, s // tq),
            in_specs=[
                pl.BlockSpec((1, tq, SWX_W), lambda bi, i, sk: (bi, i, 0)),
                pl.BlockSpec((1, s, KVB_W), lambda bi, i, sk: (bi, 0, 0)),
            ],
            out_specs=pl.BlockSpec((1, tq, SWX_W), lambda bi, i, sk: (bi, i, 0)),
        ),
        out_shape=jax.ShapeDtypeStruct((b, s, SWX_W), BF16),
        compiler_params=_cparams(("parallel", "parallel")),
        name="swa",
    )(sink, qbx, kvb)


def _outproj_kernel(x_ref, oa_ref, obx_ref, woa_ref, wobx_ref, g_ref, wrt_ref, x1_ref, h2r_ref, aff_ref):
    x1 = (x_ref[...] + jnp.dot(oa_ref[...], woa_ref[...], preferred_element_type=F32)
          + jnp.dot(obx_ref[...], wobx_ref[...], preferred_element_type=F32))
    x1_ref[...] = x1
    ms = jnp.mean(x1 * x1, axis=-1, keepdims=True)
    h2 = x1 * lax.rsqrt(ms + NORM_EPS) * g_ref[...]
    tm = h2.shape[0]
    for j in range(SUBLANES):
        h2r_ref[pl.ds(j, tm, stride=SUBLANES), :] = h2[:, j * LANES:(j + 1) * LANES]
    logits = lax.dot_general(wrt_ref[...], h2, (((1,), (1,)), ((), ())),
                             preferred_element_type=F32, precision=lax.Precision.HIGHEST)
    mx = jnp.max(logits, axis=0, keepdims=True)
    e = jnp.exp(logits - mx)
    aff_ref[...] = e / jnp.sum(e, axis=0, keepdims=True)


def _outproj(x, oa, obx, w_oa, w_obx, g_ffn, w_rt, tm):
    n = x.shape[0]
    tok = lambda i: (i, 0)
    full = lambda i: (0, 0)
    return pl.pallas_call(
        _outproj_kernel,
        grid=(n // tm,),
        in_specs=[
            pl.BlockSpec((tm, D_MODEL), tok),
            pl.BlockSpec((tm, DA_W), tok),
            pl.BlockSpec((tm, SWX_W), tok),
            pl.BlockSpec((DA_W, D_MODEL), full),
            pl.BlockSpec((SWX_W, D_MODEL), full),
            pl.BlockSpec((1, D_MODEL), full),
            pl.BlockSpec((N_EXPERTS, D_MODEL), full),
        ],
        out_specs=[
            pl.BlockSpec((tm, D_MODEL), tok),
            pl.BlockSpec((tm * SUBLANES, LANES), tok),
            pl.BlockSpec((N_EXPERTS, tm), lambda i: (0, i)),
        ],
        out_shape=[
            jax.ShapeDtypeStruct((n, D_MODEL), F32),
            jax.ShapeDtypeStruct((n * SUBLANES, LANES), F32),
            jax.ShapeDtypeStruct((N_EXPERTS, n), F32),
        ],
        compiler_params=_cparams(("parallel",)),
        name="outproj",
    )(x, oa, obx, w_oa, w_obx, g_ffn, w_rt)


def _route_kernel(aff_ref, idx_ref, vals_ref, cex_ref, *, cap, nchunk, pblk):
    a = aff_ref[0]
    bits = pltpu.bitcast(a, jnp.int32)
    def count(mask):
        per_lane = jnp.sum(jnp.where(mask, 1.0, 0.0), axis=0, keepdims=True)
        return jnp.sum(per_lane, axis=1, keepdims=True)

    thr = jnp.zeros((1, 1), jnp.int32)
    for bit in range(30, -1, -1):
        cand = thr | (1 << bit)
        thr = jnp.where(count(bits >= cand) >= cap, cand, thr)
    gt = bits > thr
    eq = bits == thr
    need = cap - count(gt)

    r128 = lax.broadcasted_iota(jnp.int32, (LANES, LANES), 0)
    c128 = lax.broadcasted_iota(jnp.int32, (LANES, LANES), 1)
    u_incl = jnp.where(r128 <= c128, 1.0, 0.0).astype(BF16)
    ones = jnp.ones((LANES, LANES), BF16)
    rc = lax.broadcasted_iota(jnp.int32, (nchunk, nchunk), 0)
    cc = lax.broadcasted_iota(jnp.int32, (nchunk, nchunk), 1)
    l_strict = jnp.where(cc < rc, 1.0, 0.0).astype(BF16)
    u_strict = jnp.where(rc < cc, 1.0, 0.0).astype(BF16)

    def chunk_scan(mask_bf):
        incl = jnp.dot(mask_bf, u_incl, preferred_element_type=F32)
        tot_b = jnp.dot(mask_bf, ones, preferred_element_type=F32)
        cexcl_b = jnp.dot(l_strict, tot_b.astype(BF16), preferred_element_type=F32)
        return incl, cexcl_b

    eq_bf = jnp.where(eq, 1.0, 0.0).astype(BF16)
    incl_eq, cexcl_eq = chunk_scan(eq_bf)
    rank_eq = cexcl_eq + incl_eq - eq_bf.astype(F32)
    sel = gt | (eq & (rank_eq < need))
    sel_bf = jnp.where(sel, 1.0, 0.0).astype(BF16)
    lc, cexcl_b = chunk_scan(sel_bf)

    tot_row = lax.dot_general(jnp.ones((8, LANES), BF16), sel_bf, (((1,), (1,)), ((), ())),
                              preferred_element_type=F32)
    cexcl_row = jnp.dot(tot_row.astype(BF16), u_strict, preferred_element_type=F32)[0:1]
    cincl_row = cexcl_row + tot_row[0:1]

    a_hi = a.astype(BF16)
    r1 = a - a_hi.astype(F32)
    a_mid = r1.astype(BF16)
    a_lo = (r1 - a_mid.astype(F32)).astype(BF16)
    kidx = lax.broadcasted_iota(jnp.int32, (nchunk, LANES), 0).astype(F32)
    cex_hi = jnp.floor(cexcl_b * (1.0 / LANES))
    cex_lo = cexcl_b - cex_hi * LANES
    table = jnp.concatenate([lc.astype(BF16), a_hi, a_mid, a_lo, kidx.astype(BF16),
                             cex_hi.astype(BF16), cex_lo.astype(BF16)], axis=1)

    lane_f = lax.broadcasted_iota(jnp.int32, (pblk, LANES), 1).astype(F32)
    eye = r128 == c128

    def slot_block(bi, _):
        base = bi * pblk
        pc = (base + lax.broadcasted_iota(jnp.int32, (pblk, nchunk), 0)).astype(F32)
        onehot_k = jnp.where((cexcl_row <= pc) & (pc < cincl_row), 1.0, 0.0).astype(BF16)
        r = jnp.dot(onehot_k, table, preferred_element_type=F32)
        r_lc = r[:, 0:LANES]
        r_aff = (r[:, LANES:2 * LANES] + r[:, 2 * LANES:3 * LANES]) + r[:, 3 * LANES:4 * LANES]
        r_k = r[:, 4 * LANES:5 * LANES]
        r_cex = r[:, 5 * LANES:6 * LANES] * LANES + r[:, 6 * LANES:7 * LANES]
        p_loc = (base + lax.broadcasted_iota(jnp.int32, (pblk, LANES), 0)).astype(F32) - r_cex
        below = jnp.where(r_lc <= p_loc, 1.0, 0.0).astype(BF16)
        t_loc = jnp.dot(below, ones, preferred_element_type=F32)
        val = jnp.sum(jnp.where(lane_f == t_loc, r_aff, 0.0), axis=-1, keepdims=True)
        tok = r_k * LANES + t_loc
        for sb in range(pblk // LANES):
            rows = slice(sb * LANES, (sb + 1) * LANES)
            out = pl.ds(pl.multiple_of(base + sb * LANES, LANES), LANES)
            idx_ref[0, :, out] = jnp.sum(jnp.where(eye, tok[rows], 0.0), axis=0, keepdims=True).astype(jnp.int32)
            vals_ref[0, :, out] = jnp.sum(jnp.where(eye, val[rows], 0.0), axis=0, keepdims=True)
        return 0

    lax.fori_loop(0, cap // pblk, slot_block, 0)
    cex_ref[0] = cexcl_row.astype(jnp.int32)


def _route(aff3, cap):
    e, nchunk, _ = aff3.shape
    pblk = min(512, cap)
    kern = functools.partial(_route_kernel, cap=cap, nchunk=nchunk, pblk=pblk)
    return pl.pallas_call(
        kern,
        grid=(e,),
        in_specs=[pl.BlockSpec((1, nchunk, LANES), lambda ei: (ei, 0, 0))],
        out_specs=[
            pl.BlockSpec((1, 1, cap), lambda ei: (ei, 0, 0)),
            pl.BlockSpec((1, 1, cap), lambda ei: (ei, 0, 0)),
            pl.BlockSpec((1, 1, nchunk), lambda ei: (ei, 0, 0)),
        ],
        out_shape=[
            jax.ShapeDtypeStruct((e, 1, cap), jnp.int32),
            jax.ShapeDtypeStruct((e, 1, cap), F32),
            jax.ShapeDtypeStruct((e, 1, nchunk), jnp.int32),
        ],
        compiler_params=_cparams(("parallel",)),
        name="route",
    )(aff3)


def _expert_kernel(ps_ref, idx_ref, val_ref, h2r_ref, wg_ref, wu_ref, wd_ref, moe_ref, xt_ref, yt_ref,
                   *, tchunk, rblk, stride):
    c = pl.program_id(0)
    e = pl.program_id(1)

    @pl.when(e == 0)
    def _():
        moe_ref[...] = jnp.zeros(moe_ref.shape, F32)

    p0 = ps_ref[e, c]
    p1 = ps_ref[e, c + 1]
    tok0 = c * tchunk
    unroll = SUBLANES

    def tile_row(p):
        return pl.multiple_of((idx_ref[0, 0, p] - tok0) * SUBLANES, SUBLANES)

    def block(b, _):
        base = p0 + b * rblk
        last = p1 - 1

        def gather(g, _):
            for i in range(unroll):
                r = g * unroll + i
                src = tile_row(jnp.minimum(base + r, last))
                xt_ref[pl.ds(r, SUBLANES, stride=stride), :] = h2r_ref[pl.ds(src, SUBLANES), :]
            return 0

        lax.fori_loop(0, rblk // unroll, gather, 0)
        x = jnp.concatenate([xt_ref[pl.ds(j * stride, rblk), :].astype(BF16) for j in range(SUBLANES)], axis=1)
        g = jnp.dot(x, wg_ref[0], preferred_element_type=F32)
        u = jnp.dot(x, wu_ref[0], preferred_element_type=F32)
        hmid = (g * jax.nn.sigmoid(g) * u).astype(BF16)
        y = jnp.dot(hmid, wd_ref[0], preferred_element_type=F32)
        for j in range(SUBLANES):
            yt_ref[pl.ds(j * stride, rblk), :] = y[:, j * LANES:(j + 1) * LANES]

        nvalid = jnp.minimum(rblk, p1 - base)

        def updated(r):
            dst = tile_row(base + r)
            contrib = yt_ref[pl.ds(r, SUBLANES, stride=stride), :] * val_ref[0, 0, base + r]
            return dst, moe_ref[pl.ds(dst, SUBLANES), :] + contrib

        def scatter_group(g, _):
            new = [updated(g * unroll + i) for i in range(unroll)]
            for dst, v in new:
                moe_ref[pl.ds(dst, SUBLANES), :] = v
            return 0

        ngroups = nvalid // unroll
        lax.fori_loop(0, ngroups, scatter_group, 0)

        def scatter_one(r, _):
            dst, v = updated(r)
            moe_ref[pl.ds(dst, SUBLANES), :] = v
            return 0

        lax.fori_loop(ngroups * unroll, nvalid, scatter_one, 0)
        return 0

    lax.fori_loop(0, (p1 - p0 + rblk - 1) // rblk, block, 0)


def _experts(pstart, idx, vals, h2r, wg, wu, wd, tchunk, rblk):
    e, _, cap = idx.shape
    n8 = h2r.shape[0]
    stride = rblk + SUBLANES
    kern = functools.partial(_expert_kernel, tchunk=tchunk, rblk=rblk, stride=stride)
    wspec = pl.BlockSpec((1, D_MODEL, D_EXPERT), lambda ci, ei, ps: (ei, 0, 0))
    slot = pl.BlockSpec((1, 1, cap), lambda ci, ei, ps: (ei, 0, 0), memory_space=pltpu.SMEM)
    chunk = lambda: pl.BlockSpec((tchunk * SUBLANES, LANES), lambda ci, ei, ps: (ci, 0),
                                 pipeline_mode=pl.Buffered(1))
    return pl.pallas_call(
        kern,
        grid_spec=pltpu.PrefetchScalarGridSpec(
            num_scalar_prefetch=1,
            grid=(n8 // (tchunk * SUBLANES), e),
            in_specs=[slot, slot, chunk(), wspec, wspec,
                      pl.BlockSpec((1, D_EXPERT, D_MODEL), lambda ci, ei, ps: (ei, 0, 0))],
            out_specs=chunk(),
            scratch_shapes=[
                pltpu.VMEM((SUBLANES * stride, LANES), F32),
                pltpu.VMEM((SUBLANES * stride, LANES), F32),
            ],
        ),
        out_shape=jax.ShapeDtypeStruct((n8, LANES), F32),
        compiler_params=pltpu.CompilerParams(dimension_semantics=("arbitrary", "arbitrary"),
                                             vmem_limit_bytes=EXPERT_VMEM_LIMIT),
        name="experts",
    )(pstart, idx, vals, h2r, wg, wu, wd)


def _ple_kernel(x_ref, moe_ref, p_ref, g_ref, wg_ref, wp_ref, o_ref):
    tm = x_ref.shape[0]
    moe = jnp.concatenate([moe_ref[pl.ds(j, tm, stride=SUBLANES), :] for j in range(SUBLANES)], axis=1)
    x = x_ref[...] + moe
    ms = jnp.mean(x * x, axis=-1, keepdims=True)
    hn = (x * lax.rsqrt(ms + NORM_EPS) * g_ref[...]).astype(BF16)
    gate = jax.nn.sigmoid(jnp.dot(hn, wg_ref[...], preferred_element_type=F32))
    emb = jnp.dot(p_ref[0].astype(BF16), wp_ref[...], preferred_element_type=F32)
    o_ref[...] = x + gate * emb


def _ple(x, moe_r, p_all, layer, g_ple, w_g, w_p, tm):
    n = x.shape[0]
    tok = lambda i: (i, 0)
    full = lambda i: (0, 0)
    return pl.pallas_call(
        _ple_kernel,
        grid=(n // tm,),
        in_specs=[
            pl.BlockSpec((tm, D_MODEL), tok),
            pl.BlockSpec((tm * SUBLANES, LANES), tok),
            pl.BlockSpec((1, tm, PLE_DIM), lambda i: (layer, i, 0)),
            pl.BlockSpec((1, D_MODEL), full),
            pl.BlockSpec((D_MODEL, D_MODEL), full),
            pl.BlockSpec((PLE_DIM, D_MODEL), full),
        ],
        out_specs=pl.BlockSpec((tm, D_MODEL), tok),
        out_shape=jax.ShapeDtypeStruct((n, D_MODEL), F32),
        compiler_params=_cparams(("parallel",)),
        name="ple",
    )(x, moe_r, p_all, g_ple, w_g, w_p)


def _rope_tables(seq):
    pos = jnp.arange(seq, dtype=F32)
    inv = ROPE_THETA ** (-jnp.arange(0, HEAD_DIM, 2, dtype=F32) / HEAD_DIM)
    ang = pos[:, None] * inv[None, :]
    ang = jnp.concatenate([ang, ang, ang, ang], axis=-1)
    cos, sin = jnp.cos(ang), jnp.sin(ang)
    first_half = (jnp.arange(LANES) % HEAD_DIM) < HEAD_DIM // 2
    sin_a = jnp.where(first_half[None, :], -sin, 0.0)
    sin_b = jnp.where(first_half[None, :], 0.0, sin)
    return cos, sin_a, sin_b


def _expand_sw_cols(w):
    lead = w.shape[:-1]
    w = w.reshape(lead + (SW_Q_HEADS, HEAD_DIM))
    z = jnp.zeros_like(w)
    kvh = (jnp.arange(SW_Q_HEADS) // SW_GROUP)[:, None]
    lo = jnp.where(kvh == 0, w, z)
    hi = jnp.where(kvh == 1, w, z)
    return jnp.concatenate([lo, hi], axis=-1).reshape(lead + (SWX_W,))


def _layer_params(i, attn_norm, w_in, q_norm_a, k_norm_a, subln_a, q_norm_b, k_norm_b, w_out):
    wi = w_in[i]
    w_qa, w_ka, w_va = wi[:, 0:512], wi[:, 512:1024], wi[:, 1024:1536]
    w_qb, w_kb, w_vb = wi[:, 1536:2048], wi[:, 2048:2176], wi[:, 2176:2304]
    w = jnp.concatenate([w_qa, w_ka, _expand_sw_cols(w_qb), w_kb, w_vb], axis=1).astype(BF16)
    wvt = w_va.T.astype(BF16)
    scale = HEAD_DIM ** -0.5
    log2e = math.log2(math.e)
    t2 = lambda g, n: jnp.tile(g, n)
    gain = jnp.concatenate([
        t2(q_norm_a[i], 8) * (scale * log2e), t2(k_norm_a[i], 8), t2(q_norm_b[i], 16) * scale,
        t2(k_norm_b[i], 2), jnp.ones((NX - COL_KVB - LANES,), F32)])[None, :]
    wo = w_out[i]
    w_oa = wo[0:DA_W].astype(BF16)
    w_obx = _expand_sw_cols(wo[DA_W:].T).T.astype(BF16)
    lam_init = 0.8 - 0.6 * math.exp(-0.3 * i)
    return dict(g_attn=attn_norm[i][None, :], w=w, wvt=wvt, gain=gain, w_oa=w_oa, w_obx=w_obx,
                g_sub=subln_a[i][None, :], lam_init=lam_init)


def _pick(n, pref):
    t = pref
    while n % t:
        t //= 2
    return t


def _encoder(x, p, attn_norm, w_in, q_norm_a, k_norm_a, lambda_q1, lambda_k1, lambda_q2, lambda_k2,
             subln_a, q_norm_b, k_norm_b, sink_b, w_out, ffn_norm, w_router, w_gate_e, w_up_e, w_down_e,
             ple_norm, w_ple_gate, w_ple_proj):
    b, s, _ = x.shape
    n = b * s
    cap = EC_CAPACITY_FACTOR * n // N_EXPERTS
    tm = _pick(s, 512)
    tq = _pick(s, 256)
    tk = _pick(s // 2, 512)
    tq_sw = _pick(s, 512)
    tchunk = _pick(n, 4096)
    rblk = 256
    cos, sin_a, sin_b = _rope_tables(s)
    r = lax.broadcasted_iota(jnp.int32, (256, 256), 0) // HEAD_DIM
    c = lax.broadcasted_iota(jnp.int32, (256, 256), 1) // HEAD_DIM
    bd = (r == c).astype(BF16)
    xf = x.reshape(n, D_MODEL)
    p_all = p.reshape(DEPTH, n, PLE_DIM)
    for i in range(DEPTH):
        lp = _layer_params(i, attn_norm, w_in, q_norm_a, k_norm_a, subln_a, q_norm_b, k_norm_b, w_out)
        qa, ka, qbx, kvb, vat = _inproj(xf, lp["g_attn"], lp["w"], lp["wvt"], lp["gain"], cos, sin_a, sin_b, bd,
                                        s, tm)
        lam4 = jnp.stack([lambda_q1[i], lambda_k1[i], lambda_q2[i], lambda_k2[i]]).astype(F32)
        oa = _diffattn(qa.reshape(b, s, DA_W), ka.reshape(b, s, DA_W), vat, lam4,
                       lp["g_sub"], lp["lam_init"], tq, tk)
        obx = _swa(sink_b[i].astype(F32), qbx.reshape(b, s, SWX_W), kvb.reshape(b, s, KVB_W), tq_sw)
        x1, h2r, aff_t = _outproj(xf, oa.reshape(n, DA_W), obx.reshape(n, SWX_W), lp["w_oa"], lp["w_obx"],
                                  ffn_norm[i][None, :], w_router[i].T, tm)
        idx, vals, cex = _route(aff_t.reshape(N_EXPERTS, n // LANES, LANES), cap)
        pstart = jnp.concatenate([cex[:, 0, ::tchunk // LANES], jnp.full((N_EXPERTS, 1), cap, jnp.int32)], axis=1)
        moe_r = _experts(pstart, idx, vals, h2r, w_gate_e[i].astype(BF16), w_up_e[i].astype(BF16),
                         w_down_e[i].astype(BF16), tchunk, rblk)
        xf = _ple(x1, moe_r, p_all, i, ple_norm[i][None, :], w_ple_gate[i].astype(BF16),
                  w_ple_proj[i].astype(BF16), tm)
    return xf.reshape(b, s, D_MODEL)


def kernel(x_prompt, x_sample, p_prompt, p_sample, attn_norm, w_in, q_norm_a, k_norm_a, lambda_q1, lambda_k1,
           lambda_q2, lambda_k2, subln_a, q_norm_b, k_norm_b, sink_b, w_out, ffn_norm, w_router, w_gate_e,
           w_up_e, w_down_e, ple_norm, w_ple_gate, w_ple_proj):
    ws = (attn_norm, w_in, q_norm_a, k_norm_a, lambda_q1, lambda_k1, lambda_q2, lambda_k2, subln_a, q_norm_b,
          k_norm_b, sink_b, w_out, ffn_norm, w_router, w_gate_e, w_up_e, w_down_e, ple_norm, w_ple_gate,
          w_ple_proj)
    return (_encoder(x_prompt, p_prompt, *ws), _encoder(x_sample, p_sample, *ws))
```

```python
import functools
import math

import jax
import jax.numpy as jnp
from jax import lax
from jax.experimental import pallas as pl
from jax.experimental.pallas import tpu as pltpu

F32 = jnp.float32
BF16 = jnp.bfloat16

D_MODEL = 1024
DEPTH = 4
HEAD_DIM = 64
DA_HEADS = 4
SW_Q_HEADS = 8
SW_KV_HEADS = 2
SW_GROUP = SW_Q_HEADS // SW_KV_HEADS
WINDOW = 128
N_EXPERTS = 16
EC_CAPACITY_FACTOR = 2
D_EXPERT = 1024
PLE_DIM = 256
ROPE_THETA = 10000.0
NORM_EPS = 1e-6
MASK_VALUE = -1e30

LANES = 128
SUBLANES = 8
DA_W = DA_HEADS * 2 * HEAD_DIM
SWX_W = SW_Q_HEADS * LANES
KVB_W = 2 * LANES
COL_QA, COL_KA, COL_QBX, COL_KVB = 0, 512, 1024, 2048
NX = 2304
N_ROPE_TILES = (COL_KVB + LANES) // LANES
VMEM_LIMIT = 48 * 1024 * 1024
EXPERT_VMEM_LIMIT = 56 * 1024 * 1024
EXPERT_ROWS = 288
SLOT_PAD = 512


def _cparams(sem):
    return pltpu.CompilerParams(dimension_semantics=sem, vmem_limit_bytes=VMEM_LIMIT)


def _inproj_kernel(x_ref, gat_ref, w_ref, wvt_ref, gain_ref, cos_ref, sa_ref, sb_ref, bd_ref,
                   qa_ref, ka_ref, qbx_ref, kvb_ref, vat_ref, xn_ref):
    x = x_ref[...]
    ms = jnp.mean(x * x, axis=-1, keepdims=True)
    xn_ref[...] = (x * lax.rsqrt(ms + NORM_EPS) * gat_ref[...]).astype(BF16)
    cos = cos_ref[...]
    sa = sa_ref[...]
    sb = sb_ref[...]
    outs = ((qa_ref, COL_QA), (ka_ref, COL_KA), (qbx_ref, COL_QBX), (kvb_ref, COL_KVB))
    vat_ref[0] = lax.dot_general(wvt_ref[...], xn_ref[...], (((1,), (1,)), ((), ())),
                                 preferred_element_type=F32).astype(BF16)

    def out_for(col):
        for ref, base in reversed(outs):
            if col >= base:
                return ref, col - base
        raise AssertionError

    def project(c):
        return jnp.dot(xn_ref[...], w_ref[:, c * 256:(c + 1) * 256], preferred_element_type=F32)

    nchunks = NX // 256
    p_next = project(0)
    for c in range(nchunks):
        c0 = c * 256
        p = p_next
        if c + 1 < nchunks:
            p_next = project(c + 1)
        normed = [(c0 + t * LANES) // LANES < N_ROPE_TILES for t in range(2)]
        if any(normed):
            ss = jnp.dot((p * p).astype(BF16), bd_ref[...], preferred_element_type=F32)
        for t in range(2):
            col = c0 + t * LANES
            y = p[:, t * LANES:(t + 1) * LANES]
            if normed[t]:
                sst = ss[:, t * LANES:(t + 1) * LANES]
                y = y * lax.rsqrt(sst * (1.0 / HEAD_DIM) + NORM_EPS) * gain_ref[:, col:col + LANES]
                y = y * cos + pltpu.roll(y, 96, 1) * sa + pltpu.roll(y, 32, 1) * sb
            ref, off = out_for(col)
            ref[:, off:off + LANES] = y.astype(BF16)


def _inproj(x, g_attn, w, wvt, gain, cos, sa, sb, bd, seq, tm):
    n = x.shape[0]
    nblk_s = seq // tm
    tok = lambda i: (i, 0)
    full = lambda i: (0, 0)
    rope = lambda i: (i % nblk_s, 0)
    return pl.pallas_call(
        _inproj_kernel,
        grid=(n // tm,),
        in_specs=[
            pl.BlockSpec((tm, D_MODEL), tok),
            pl.BlockSpec((1, D_MODEL), full),
            pl.BlockSpec((D_MODEL, NX), full),
            pl.BlockSpec((DA_W, D_MODEL), full),
            pl.BlockSpec((1, NX), full),
            pl.BlockSpec((tm, LANES), rope),
            pl.BlockSpec((tm, LANES), rope),
            pl.BlockSpec((tm, LANES), rope),
            pl.BlockSpec((256, 256), full),
        ],
        out_specs=[
            pl.BlockSpec((tm, DA_W), tok),
            pl.BlockSpec((tm, DA_W), tok),
            pl.BlockSpec((tm, SWX_W), tok),
            pl.BlockSpec((tm, KVB_W), tok),
            pl.BlockSpec((1, DA_W, tm), lambda i: (i // nblk_s, 0, i % nblk_s)),
        ],
        out_shape=[
            jax.ShapeDtypeStruct((n, DA_W), BF16),
            jax.ShapeDtypeStruct((n, DA_W), BF16),
            jax.ShapeDtypeStruct((n, SWX_W), BF16),
            jax.ShapeDtypeStruct((n, KVB_W), BF16),
            jax.ShapeDtypeStruct((n // seq, DA_W, seq), BF16),
        ],
        scratch_shapes=[pltpu.VMEM((tm, D_MODEL), BF16)],
        compiler_params=_cparams(("parallel",)),
        name="inproj",
    )(x, g_attn, w, wvt, gain, cos, sa, sb, bd)


def _diffattn_kernel(q_ref, k_ref, vt_ref, lam4_ref, g_ref, o_ref, m_ref, l_ref, acc_ref, qst_ref, s_ref,
                     *, tq, tk, nk, unroll, lam_init):
    qt = q_ref[0].astype(F32).T
    feat = lax.broadcasted_iota(jnp.int32, qt.shape, 0)
    zero = jnp.zeros_like(qt)
    qst_ref[...] = jnp.concatenate([jnp.where(feat < HEAD_DIM, qt, zero), jnp.where(feat >= HEAD_DIM, qt, zero)],
                                   axis=1).astype(BF16)
    m_ref[...] = jnp.full(m_ref.shape, -jnp.inf, F32)
    l_ref[...] = jnp.zeros(l_ref.shape, F32)
    acc_ref[...] = jnp.zeros(acc_ref.shape, F32)

    def scores(j):
        k0 = pl.multiple_of(j * tk, tk)
        return jnp.dot(k_ref[0, pl.ds(k0, tk), :], qst_ref[...], preferred_element_type=F32)

    def accumulate(s, j):
        k0 = pl.multiple_of(j * tk, tk)
        vtj = vt_ref[0, :, pl.ds(k0, tk)]
        m_old = m_ref[...]
        m_new = jnp.maximum(m_old, jnp.max(s, axis=0, keepdims=True))
        alpha = jnp.exp2(m_old - m_new)
        p = jnp.exp2(s - m_new)
        l_ref[...] = alpha * l_ref[...] + jnp.sum(p, axis=0, keepdims=True)
        acc_ref[...] = alpha * acc_ref[...] + jnp.dot(vtj, p.astype(BF16), preferred_element_type=F32)
        m_ref[...] = m_new

    s_ref[0] = scores(0)

    def body(jj, _):
        j0 = unroll * jj
        for u in range(unroll):
            s_ref[(u + 1) % 2] = scores(jnp.minimum(j0 + u + 1, nk - 1))
            accumulate(s_ref[u % 2], j0 + u)
        return 0

    lax.fori_loop(0, nk // unroll, body, 0)
    ot = acc_ref[...] / l_ref[...]
    lam4 = lam4_ref[...]
    lam = (jnp.exp(jnp.sum(lam4[0:1] * lam4[1:2], axis=-1, keepdims=True))
           - jnp.exp(jnp.sum(lam4[2:3] * lam4[3:4], axis=-1, keepdims=True)) + lam_init)
    o = (ot[:, :tq] - lam * ot[:, tq:]).T
    ms = jnp.mean(o * o, axis=-1, keepdims=True)
    o = o * lax.rsqrt(ms + NORM_EPS) * g_ref[...] * (1.0 - lam_init)
    o_ref[0] = o.astype(BF16)


def _diffattn(qa, ka, vat, lam4, g_sub, lam_init, tq, tk):
    b, s, _ = qa.shape
    nk = s // tk
    unroll = 4 if nk % 4 == 0 else 2
    kern = functools.partial(_diffattn_kernel, tq=tq, tk=tk, nk=nk, unroll=unroll, lam_init=lam_init)
    return pl.pallas_call(
        kern,
        grid=(b, DA_HEADS, s // tq),
        in_specs=[
            pl.BlockSpec((1, tq, LANES), lambda bi, h, i: (bi, i, h)),
            pl.BlockSpec((1, s, LANES), lambda bi, h, i: (bi, 0, h)),
            pl.BlockSpec((1, LANES, s), lambda bi, h, i: (bi, h, 0)),
            pl.BlockSpec((4, HEAD_DIM), lambda bi, h, i: (0, 0)),
            pl.BlockSpec((1, LANES), lambda bi, h, i: (0, 0)),
        ],
        out_specs=pl.BlockSpec((1, tq, LANES), lambda bi, h, i: (bi, i, h)),
        out_shape=jax.ShapeDtypeStruct((b, s, DA_W), BF16),
        scratch_shapes=[
            pltpu.VMEM((1, 2 * tq), F32),
            pltpu.VMEM((1, 2 * tq), F32),
            pltpu.VMEM((LANES, 2 * tq), F32),
            pltpu.VMEM((LANES, 2 * tq), BF16),
            pltpu.VMEM((2, tk, 2 * tq), F32),
        ],
        compiler_params=_cparams(("parallel", "parallel", "parallel")),
        name="diffattn",
    )(qa, ka, vat, lam4, g_sub)


def _swa_kernel(sink_ref, q_ref, kv_ref, o_ref, *, tq, seq):
    i = pl.program_id(1)
    kw = 3 * WINDOW
    for sb in range(tq // WINDOW):
        rows = slice(sb * WINDOW, (sb + 1) * WINDOW)
        q0 = i * tq + sb * WINDOW
        g0 = pl.multiple_of(jnp.clip(q0 - WINDOW, 0, seq - kw), WINDOW)
        kv = kv_ref[0, pl.ds(g0, kw), :]
        k2 = kv[:, 0:LANES]
        v2 = kv[:, LANES:2 * LANES]
        qpos = q0 + lax.broadcasted_iota(jnp.int32, (WINDOW, kw), 0)
        kpos = g0 + lax.broadcasted_iota(jnp.int32, (WINDOW, kw), 1)
        mask = jnp.abs(kpos - qpos) <= WINDOW
        heads = [slice(h * LANES, (h + 1) * LANES) for h in range(SW_Q_HEADS)]
        scores = [lax.dot_general(q_ref[0, rows, cols], k2, (((1,), (1,)), ((), ())), preferred_element_type=F32)
                  for cols in heads]
        for h, cols in enumerate(heads):
            s = jnp.where(mask, scores[h], MASK_VALUE)
            sink = sink_ref[h]
            m = jnp.maximum(jnp.max(s, axis=-1, keepdims=True), sink)
            e = jnp.exp(s - m)
            den = jnp.sum(e, axis=-1, keepdims=True) + jnp.exp(sink - m)
            pr = (e / den).astype(BF16)
            o_ref[0, rows, cols] = jnp.dot(pr, v2, preferred_element_type=F32).astype(BF16)


def _swa(sink, qbx, kvb, tq):
    b, s, _ = qbx.shape
    kern = functools.partial(_swa_kernel, tq=tq, seq=s)
    return pl.pallas_call(
        kern,
        grid_spec=pltpu.PrefetchScalarGridSpec(
            num_scalar_prefetch=1,
            grid=(b, s // tq),
            in_specs=[
                pl.BlockSpec((1, tq, SWX_W), lambda bi, i, sk: (bi, i, 0)),
                pl.BlockSpec((1, s, KVB_W), lambda bi, i, sk: (bi, 0, 0)),
            ],
            out_specs=pl.BlockSpec((1, tq, SWX_W), lambda bi, i, sk: (bi, i, 0)),
        ),
        out_shape=jax.ShapeDtypeStruct((b, s, SWX_W), BF16),
        compiler_params=_cparams(("parallel", "parallel")),
        name="swa",
    )(sink, qbx, kvb)


def _outproj_kernel(x_ref, oa_ref, obx_ref, woa_ref, wobx_ref, g_ref, wrt_ref, x1_ref, h2r_ref, aff_ref):
    x1 = (x_ref[...] + jnp.dot(oa_ref[...], woa_ref[...], preferred_element_type=F32)
          + jnp.dot(obx_ref[...], wobx_ref[...], preferred_element_type=F32))
    x1_ref[...] = x1
    ms = jnp.mean(x1 * x1, axis=-1, keepdims=True)
    h2 = x1 * lax.rsqrt(ms + NORM_EPS) * g_ref[...]
    tm = h2.shape[0]
    for j in range(SUBLANES):
        h2r_ref[pl.ds(j, tm, stride=SUBLANES), :] = h2[:, j * LANES:(j + 1) * LANES]
    logits = lax.dot_general(wrt_ref[...], h2, (((1,), (1,)), ((), ())),
                             preferred_element_type=F32, precision=lax.Precision.HIGHEST)
    mx = jnp.max(logits, axis=0, keepdims=True)
    e = jnp.exp(logits - mx)
    aff_ref[...] = e / jnp.sum(e, axis=0, keepdims=True)


def _outproj(x, oa, obx, w_oa, w_obx, g_ffn, w_rt, tm):
    n = x.shape[0]
    tok = lambda i: (i, 0)
    full = lambda i: (0, 0)
    return pl.pallas_call(
        _outproj_kernel,
        grid=(n // tm,),
        in_specs=[
            pl.BlockSpec((tm, D_MODEL), tok),
            pl.BlockSpec((tm, DA_W), tok),
            pl.BlockSpec((tm, SWX_W), tok),
            pl.BlockSpec((DA_W, D_MODEL), full),
            pl.BlockSpec((SWX_W, D_MODEL), full),
            pl.BlockSpec((1, D_MODEL), full),
            pl.BlockSpec((N_EXPERTS, D_MODEL), full),
        ],
        out_specs=[
            pl.BlockSpec((tm, D_MODEL), tok),
            pl.BlockSpec((tm * SUBLANES, LANES), tok),
            pl.BlockSpec((N_EXPERTS, tm), lambda i: (0, i)),
        ],
        out_shape=[
            jax.ShapeDtypeStruct((n, D_MODEL), F32),
            jax.ShapeDtypeStruct((n * SUBLANES, LANES), F32),
            jax.ShapeDtypeStruct((N_EXPERTS, n), F32),
        ],
        compiler_params=_cparams(("parallel",)),
        name="outproj",
    )(x, oa, obx, w_oa, w_obx, g_ffn, w_rt)


def _route_kernel(aff_ref, loc_ref, vals_ref, cex_ref, *, cap, nchunk, pblk, tchunk):
    a = aff_ref[0]
    bits = pltpu.bitcast(a, jnp.int32)
    def count(mask):
        per_lane = jnp.sum(jnp.where(mask, 1.0, 0.0), axis=0, keepdims=True)
        return jnp.sum(per_lane, axis=1, keepdims=True)

    thr = jnp.zeros((1, 1), jnp.int32)
    for bit in range(30, -1, -1):
        cand = thr | (1 << bit)
        thr = jnp.where(count(bits >= cand) >= cap, cand, thr)
    gt = bits > thr
    eq = bits == thr
    need = cap - count(gt)

    r128 = lax.broadcasted_iota(jnp.int32, (LANES, LANES), 0)
    c128 = lax.broadcasted_iota(jnp.int32, (LANES, LANES), 1)
    u_incl = jnp.where(r128 <= c128, 1.0, 0.0).astype(BF16)
    ones = jnp.ones((LANES, LANES), BF16)
    rc = lax.broadcasted_iota(jnp.int32, (nchunk, nchunk), 0)
    cc = lax.broadcasted_iota(jnp.int32, (nchunk, nchunk), 1)
    l_strict = jnp.where(cc < rc, 1.0, 0.0).astype(BF16)
    u_strict = jnp.where(rc < cc, 1.0, 0.0).astype(BF16)

    def chunk_scan(mask_bf):
        incl = jnp.dot(mask_bf, u_incl, preferred_element_type=F32)
        tot_b = jnp.dot(mask_bf, ones, preferred_element_type=F32)
        cexcl_b = jnp.dot(l_strict, tot_b.astype(BF16), preferred_element_type=F32)
        return incl, cexcl_b

    eq_bf = jnp.where(eq, 1.0, 0.0).astype(BF16)
    incl_eq, cexcl_eq = chunk_scan(eq_bf)
    rank_eq = cexcl_eq + incl_eq - eq_bf.astype(F32)
    sel = gt | (eq & (rank_eq < need))
    sel_bf = jnp.where(sel, 1.0, 0.0).astype(BF16)
    lc, cexcl_b = chunk_scan(sel_bf)

    tot_row = lax.dot_general(jnp.ones((8, LANES), BF16), sel_bf, (((1,), (1,)), ((), ())),
                              preferred_element_type=F32)
    cexcl_row = jnp.dot(tot_row.astype(BF16), u_strict, preferred_element_type=F32)[0:1]
    cincl_row = cexcl_row + tot_row[0:1]

    a_hi = a.astype(BF16)
    r1 = a - a_hi.astype(F32)
    a_mid = r1.astype(BF16)
    a_lo = (r1 - a_mid.astype(F32)).astype(BF16)
    kidx = lax.broadcasted_iota(jnp.int32, (nchunk, LANES), 0).astype(F32)
    cex_hi = jnp.floor(cexcl_b * (1.0 / LANES))
    cex_lo = cexcl_b - cex_hi * LANES
    table = jnp.concatenate([lc.astype(BF16), a_hi, a_mid, a_lo, kidx.astype(BF16),
                             cex_hi.astype(BF16), cex_lo.astype(BF16)], axis=1)

    lane_f = lax.broadcasted_iota(jnp.int32, (pblk, LANES), 1).astype(F32)
    eye = r128 == c128

    def slot_block(bi, _):
        base = bi * pblk
        pc = (base + lax.broadcasted_iota(jnp.int32, (pblk, nchunk), 0)).astype(F32)
        onehot_k = jnp.where((cexcl_row <= pc) & (pc < cincl_row), 1.0, 0.0).astype(BF16)
        r = jnp.dot(onehot_k, table, preferred_element_type=F32)
        r_lc = r[:, 0:LANES]
        r_aff = (r[:, LANES:2 * LANES] + r[:, 2 * LANES:3 * LANES]) + r[:, 3 * LANES:4 * LANES]
        r_k = r[:, 4 * LANES:5 * LANES]
        r_cex = r[:, 5 * LANES:6 * LANES] * LANES + r[:, 6 * LANES:7 * LANES]
        p_loc = (base + lax.broadcasted_iota(jnp.int32, (pblk, LANES), 0)).astype(F32) - r_cex
        below = jnp.where(r_lc <= p_loc, 1.0, 0.0).astype(BF16)
        t_loc = jnp.dot(below, ones, preferred_element_type=F32)
        val = jnp.sum(jnp.where(lane_f == t_loc, r_aff, 0.0), axis=-1, keepdims=True)
        tok = r_k * LANES + t_loc
        for sb in range(pblk // LANES):
            rows = slice(sb * LANES, (sb + 1) * LANES)
            out = pl.ds(pl.multiple_of(base + sb * LANES, LANES), LANES)
            tok_row = jnp.sum(jnp.where(eye, tok[rows], 0.0), axis=0, keepdims=True).astype(jnp.int32)
            loc_ref[0, :, out] = (tok_row & (tchunk - 1)) * SUBLANES
            vals_ref[0, :, out] = jnp.sum(jnp.where(eye, val[rows], 0.0), axis=0, keepdims=True)
        return 0

    lax.fori_loop(0, cap // pblk, slot_block, 0)
    loc_ref[0, :, pl.ds(cap, SLOT_PAD)] = jnp.zeros((1, SLOT_PAD), jnp.int32)
    cex_ref[0] = cexcl_row.astype(jnp.int32)


def _route(aff3, cap, tchunk):
    e, nchunk, _ = aff3.shape
    pblk = min(512, cap)
    kern = functools.partial(_route_kernel, cap=cap, nchunk=nchunk, pblk=pblk, tchunk=tchunk)
    return pl.pallas_call(
        kern,
        grid=(e,),
        in_specs=[pl.BlockSpec((1, nchunk, LANES), lambda ei: (ei, 0, 0))],
        out_specs=[
            pl.BlockSpec((1, 1, cap + SLOT_PAD), lambda ei: (ei, 0, 0)),
            pl.BlockSpec((1, 1, cap), lambda ei: (ei, 0, 0)),
            pl.BlockSpec((1, 1, nchunk), lambda ei: (ei, 0, 0)),
        ],
        out_shape=[
            jax.ShapeDtypeStruct((e, 1, cap + SLOT_PAD), jnp.int32),
            jax.ShapeDtypeStruct((e, 1, cap), F32),
            jax.ShapeDtypeStruct((e, 1, nchunk), jnp.int32),
        ],
        compiler_params=_cparams(("parallel",)),
        name="route",
    )(aff3)


def _expert_kernel(ps_ref, loc_ref, val_ref, h2r_ref, wg_ref, wu_ref, wd_ref, moe_ref, xt_ref, yt_ref,
                   *, rblk, stride):
    c = pl.program_id(0)
    e = pl.program_id(1)

    @pl.when(e == 0)
    def _():
        moe_ref[...] = jnp.zeros(moe_ref.shape, F32)

    p0 = ps_ref[e, c]
    p1 = ps_ref[e, c + 1]
    unroll = SUBLANES

    def tile_row(p):
        return pl.multiple_of(loc_ref[0, 0, p], SUBLANES)

    def block(b, _):
        base = p0 + b * rblk

        def gather(g, _):
            for i in range(2 * unroll):
                r = g * 2 * unroll + i
                xt_ref[pl.ds(r, SUBLANES, stride=stride), :] = h2r_ref[pl.ds(tile_row(base + r), SUBLANES), :]
            return 0

        lax.fori_loop(0, rblk // (2 * unroll), gather, 0)
        x = jnp.concatenate([xt_ref[pl.ds(j * stride, rblk), :].astype(BF16) for j in range(SUBLANES)], axis=1)
        g = jnp.dot(x, wg_ref[0], preferred_element_type=F32)
        u = jnp.dot(x, wu_ref[0], preferred_element_type=F32)
        hmid = (g * jax.nn.sigmoid(g) * u).astype(BF16)
        y = jnp.dot(hmid, wd_ref[0], preferred_element_type=F32)
        for j in range(SUBLANES):
            yt_ref[pl.ds(j * stride, rblk), :] = y[:, j * LANES:(j + 1) * LANES]

        nvalid = jnp.minimum(rblk, p1 - base)

        def scatter_rows(r0, count):
            new = []
            for i in range(count):
                dst = tile_row(base + r0 + i)
                contrib = yt_ref[pl.ds(r0 + i, SUBLANES, stride=stride), :] * val_ref[0, 0, base + r0 + i]
                new.append((dst, moe_ref[pl.ds(dst, SUBLANES), :] + contrib))
            for dst, v in new:
                moe_ref[pl.ds(dst, SUBLANES), :] = v

        def scatter_pair(g, _):
            scatter_rows(g * 2 * unroll, unroll)
            scatter_rows(g * 2 * unroll + unroll, unroll)
            return 0

        npairs = nvalid // (2 * unroll)
        lax.fori_loop(0, npairs, scatter_pair, 0)

        def scatter_one(r, _):
            scatter_rows(r, 1)
            return 0

        lax.fori_loop(npairs * 2 * unroll, nvalid, scatter_one, 0)
        return 0

    lax.fori_loop(0, (p1 - p0 + rblk - 1) // rblk, block, 0)


def _experts(pstart, loc, vals, h2r, wg, wu, wd, tchunk, rblk):
    e, _, cap_pad = loc.shape
    n8 = h2r.shape[0]
    stride = rblk + SUBLANES
    kern = functools.partial(_expert_kernel, rblk=rblk, stride=stride)
    wspec = pl.BlockSpec((1, D_MODEL, D_EXPERT), lambda ci, ei, ps: (ei, 0, 0))
    slot = lambda a: pl.BlockSpec((1, 1, a.shape[2]), lambda ci, ei, ps: (ei, 0, 0), memory_space=pltpu.SMEM)
    chunk = lambda: pl.BlockSpec((tchunk * SUBLANES, LANES), lambda ci, ei, ps: (ci, 0),
                                 pipeline_mode=pl.Buffered(1))
    return pl.pallas_call(
        kern,
        grid_spec=pltpu.PrefetchScalarGridSpec(
            num_scalar_prefetch=1,
            grid=(n8 // (tchunk * SUBLANES), e),
            in_specs=[slot(loc), slot(vals), chunk(), wspec, wspec,
                      pl.BlockSpec((1, D_EXPERT, D_MODEL), lambda ci, ei, ps: (ei, 0, 0))],
            out_specs=chunk(),
            scratch_shapes=[
                pltpu.VMEM((SUBLANES * stride, LANES), F32),
                pltpu.VMEM((SUBLANES * stride, LANES), F32),
            ],
        ),
        out_shape=jax.ShapeDtypeStruct((n8, LANES), F32),
        compiler_params=pltpu.CompilerParams(dimension_semantics=("arbitrary", "arbitrary"),
                                             vmem_limit_bytes=EXPERT_VMEM_LIMIT),
        name="experts",
    )(pstart, loc, vals, h2r, wg, wu, wd)


def _ple_kernel(x_ref, moe_ref, p_ref, g_ref, wg_ref, wp_ref, o_ref):
    tm = x_ref.shape[0]
    moe = jnp.concatenate([moe_ref[pl.ds(j, tm, stride=SUBLANES), :] for j in range(SUBLANES)], axis=1)
    x = x_ref[...] + moe
    ms = jnp.mean(x * x, axis=-1, keepdims=True)
    hn = (x * lax.rsqrt(ms + NORM_EPS) * g_ref[...]).astype(BF16)
    gate = jax.nn.sigmoid(jnp.dot(hn, wg_ref[...], preferred_element_type=F32))
    emb = jnp.dot(p_ref[0].astype(BF16), wp_ref[...], preferred_element_type=F32)
    o_ref[...] = x + gate * emb


def _ple(x, moe_r, p_all, layer, g_ple, w_g, w_p, tm):
    n = x.shape[0]
    tok = lambda i: (i, 0)
    full = lambda i: (0, 0)
    return pl.pallas_call(
        _ple_kernel,
        grid=(n // tm,),
        in_specs=[
            pl.BlockSpec((tm, D_MODEL), tok),
            pl.BlockSpec((tm * SUBLANES, LANES), tok),
            pl.BlockSpec((1, tm, PLE_DIM), lambda i: (layer, i, 0)),
            pl.BlockSpec((1, D_MODEL), full),
            pl.BlockSpec((D_MODEL, D_MODEL), full),
            pl.BlockSpec((PLE_DIM, D_MODEL), full),
        ],
        out_specs=pl.BlockSpec((tm, D_MODEL), tok),
        out_shape=jax.ShapeDtypeStruct((n, D_MODEL), F32),
        compiler_params=_cparams(("parallel",)),
        name="ple",
    )(x, moe_r, p_all, g_ple, w_g, w_p)


def _rope_tables(seq):
    pos = jnp.arange(seq, dtype=F32)
    inv = ROPE_THETA ** (-jnp.arange(0, HEAD_DIM, 2, dtype=F32) / HEAD_DIM)
    ang = pos[:, None] * inv[None, :]
    ang = jnp.concatenate([ang, ang, ang, ang], axis=-1)
    cos, sin = jnp.cos(ang), jnp.sin(ang)
    first_half = (jnp.arange(LANES) % HEAD_DIM) < HEAD_DIM // 2
    sin_a = jnp.where(first_half[None, :], -sin, 0.0)
    sin_b = jnp.where(first_half[None, :], 0.0, sin)
    return cos, sin_a, sin_b


def _expand_sw_cols(w):
    lead = w.shape[:-1]
    w = w.reshape(lead + (SW_Q_HEADS, HEAD_DIM))
    z = jnp.zeros_like(w)
    kvh = (jnp.arange(SW_Q_HEADS) // SW_GROUP)[:, None]
    lo = jnp.where(kvh == 0, w, z)
    hi = jnp.where(kvh == 1, w, z)
    return jnp.concatenate([lo, hi], axis=-1).reshape(lead + (SWX_W,))


def _layer_params(i, attn_norm, w_in, q_norm_a, k_norm_a, subln_a, q_norm_b, k_norm_b, w_out):
    wi = w_in[i]
    w_qa, w_ka, w_va = wi[:, 0:512], wi[:, 512:1024], wi[:, 1024:1536]
    w_qb, w_kb, w_vb = wi[:, 1536:2048], wi[:, 2048:2176], wi[:, 2176:2304]
    w = jnp.concatenate([w_qa, w_ka, _expand_sw_cols(w_qb), w_kb, w_vb], axis=1).astype(BF16)
    wvt = w_va.T.astype(BF16)
    scale = HEAD_DIM ** -0.5
    log2e = math.log2(math.e)
    t2 = lambda g, n: jnp.tile(g, n)
    gain = jnp.concatenate([
        t2(q_norm_a[i], 8) * (scale * log2e), t2(k_norm_a[i], 8), t2(q_norm_b[i], 16) * scale,
        t2(k_norm_b[i], 2), jnp.ones((NX - COL_KVB - LANES,), F32)])[None, :]
    wo = w_out[i]
    w_oa = wo[0:DA_W].astype(BF16)
    w_obx = _expand_sw_cols(wo[DA_W:].T).T.astype(BF16)
    lam_init = 0.8 - 0.6 * math.exp(-0.3 * i)
    return dict(g_attn=attn_norm[i][None, :], w=w, wvt=wvt, gain=gain, w_oa=w_oa, w_obx=w_obx,
                g_sub=subln_a[i][None, :], lam_init=lam_init)


def _pick(n, pref):
    t = pref
    while n % t:
        t //= 2
    return t


def _encoder(x, p, attn_norm, w_in, q_norm_a, k_norm_a, lambda_q1, lambda_k1, lambda_q2, lambda_k2,
             subln_a, q_norm_b, k_norm_b, sink_b, w_out, ffn_norm, w_router, w_gate_e, w_up_e, w_down_e,
             ple_norm, w_ple_gate, w_ple_proj):
    b, s, _ = x.shape
    n = b * s
    cap = EC_CAPACITY_FACTOR * n // N_EXPERTS
    tm = _pick(s, 512)
    tq = _pick(s, 512)
    tk = _pick(s // 2, 512)
    tq_sw = _pick(s, 512)
    tchunk = _pick(n, 4096)
    rblk = EXPERT_ROWS
    cos, sin_a, sin_b = _rope_tables(s)
    r = lax.broadcasted_iota(jnp.int32, (256, 256), 0) // HEAD_DIM
    c = lax.broadcasted_iota(jnp.int32, (256, 256), 1) // HEAD_DIM
    bd = (r == c).astype(BF16)
    xf = x.reshape(n, D_MODEL)
    p_all = p.reshape(DEPTH, n, PLE_DIM)
    for i in range(DEPTH):
        lp = _layer_params(i, attn_norm, w_in, q_norm_a, k_norm_a, subln_a, q_norm_b, k_norm_b, w_out)
        qa, ka, qbx, kvb, vat = _inproj(xf, lp["g_attn"], lp["w"], lp["wvt"], lp["gain"], cos, sin_a, sin_b, bd,
                                        s, tm)
        lam4 = jnp.stack([lambda_q1[i], lambda_k1[i], lambda_q2[i], lambda_k2[i]]).astype(F32)
        oa = _diffattn(qa.reshape(b, s, DA_W), ka.reshape(b, s, DA_W), vat, lam4,
                       lp["g_sub"], lp["lam_init"], tq, tk)
        obx = _swa(sink_b[i].astype(F32), qbx.reshape(b, s, SWX_W), kvb.reshape(b, s, KVB_W), tq_sw)
        x1, h2r, aff_t = _outproj(xf, oa.reshape(n, DA_W), obx.reshape(n, SWX_W), lp["w_oa"], lp["w_obx"],
                                  ffn_norm[i][None, :], w_router[i].T, tm)
        loc, vals, cex = _route(aff_t.reshape(N_EXPERTS, n // LANES, LANES), cap, tchunk)
        pstart = jnp.concatenate([cex[:, 0, ::tchunk // LANES], jnp.full((N_EXPERTS, 1), cap, jnp.int32)], axis=1)
        moe_r = _experts(pstart, loc, vals, h2r, w_gate_e[i].astype(BF16), w_up_e[i].astype(BF16),
                         w_down_e[i].astype(BF16), tchunk, rblk)
        xf = _ple(x1, moe_r, p_all, i, ple_norm[i][None, :], w_ple_gate[i].astype(BF16),
                  w_ple_proj[i].astype(BF16), tm)
    return xf.reshape(b, s, D_MODEL)


def kernel(x_prompt, x_sample, p_prompt, p_sample, attn_norm, w_in, q_norm_a, k_norm_a, lambda_q1, lambda_k1,
           lambda_q2, lambda_k2, subln_a, q_norm_b, k_norm_b, sink_b, w_out, ffn_norm, w_router, w_gate_e,
           w_up_e, w_down_e, ple_norm, w_ple_gate, w_ple_proj):
    ws = (attn_norm, w_in, q_norm_a, k_norm_a, lambda_q1, lambda_k1, lambda_q2, lambda_k2, subln_a, q_norm_b,
          k_norm_b, sink_b, w_out, ffn_norm, w_router, w_gate_e, w_up_e, w_down_e, ple_norm, w_ple_gate,
          w_ple_proj)
    return (_encoder(x_prompt, p_prompt, *ws), _encoder(x_sample, p_sample, *ws))
```

```python
import functools
import math

import jax
import jax.numpy as jnp
from jax import lax
from jax.experimental import pallas as pl
from jax.experimental.pallas import tpu as pltpu

F32 = jnp.float32
BF16 = jnp.bfloat16

D_MODEL = 1024
DEPTH = 4
HEAD_DIM = 64
DA_HEADS = 4
SW_Q_HEADS = 8
SW_KV_HEADS = 2
SW_GROUP = SW_Q_HEADS // SW_KV_HEADS
WINDOW = 128
N_EXPERTS = 16
EC_CAPACITY_FACTOR = 2
D_EXPERT = 1024
PLE_DIM = 256
ROPE_THETA = 10000.0
NORM_EPS = 1e-6
MASK_VALUE = -1e30

LANES = 128
SUBLANES = 8
DA_W = DA_HEADS * 2 * HEAD_DIM
SWX_W = SW_Q_HEADS * LANES
KVB_W = 2 * LANES
COL_QA, COL_KA, COL_QBX, COL_KVB = 0, 512, 1024, 2048
NX = 2304
N_ROPE_TILES = (COL_KVB + LANES) // LANES
VMEM_LIMIT = 48 * 1024 * 1024
EXPERT_VMEM_LIMIT = 56 * 1024 * 1024
EXPERT_ROWS = 288
SLOT_PAD = 512


def _cparams(sem):
    return pltpu.CompilerParams(dimension_semantics=sem, vmem_limit_bytes=VMEM_LIMIT)


def _inproj_kernel(x_ref, gat_ref, w_ref, wvt_ref, gain_ref, cos_ref, sa_ref, sb_ref, bd_ref,
                   qa_ref, ka_ref, qbx_ref, kvb_ref, vat_ref, xn_ref):
    x = x_ref[...]
    ms = jnp.mean(x * x, axis=-1, keepdims=True)
    xn_ref[...] = (x * lax.rsqrt(ms + NORM_EPS) * gat_ref[...]).astype(BF16)
    cos = cos_ref[...]
    sa = sa_ref[...]
    sb = sb_ref[...]
    outs = ((qa_ref, COL_QA), (ka_ref, COL_KA), (qbx_ref, COL_QBX), (kvb_ref, COL_KVB))
    vat_ref[0] = lax.dot_general(wvt_ref[...], xn_ref[...], (((1,), (1,)), ((), ())),
                                 preferred_element_type=F32).astype(BF16)

    def out_for(col):
        for ref, base in reversed(outs):
            if col >= base:
                return ref, col - base
        raise AssertionError

    def project(c):
        return jnp.dot(xn_ref[...], w_ref[:, c * 256:(c + 1) * 256], preferred_element_type=F32)

    nchunks = NX // 256
    p_next = project(0)
    for c in range(nchunks):
        c0 = c * 256
        p = p_next
        if c + 1 < nchunks:
            p_next = project(c + 1)
        normed = [(c0 + t * LANES) // LANES < N_ROPE_TILES for t in range(2)]
        if any(normed):
            ss = jnp.dot((p * p).astype(BF16), bd_ref[...], preferred_element_type=F32)
        for t in range(2):
            col = c0 + t * LANES
            y = p[:, t * LANES:(t + 1) * LANES]
            if normed[t]:
                sst = ss[:, t * LANES:(t + 1) * LANES]
                y = y * lax.rsqrt(sst * (1.0 / HEAD_DIM) + NORM_EPS) * gain_ref[:, col:col + LANES]
                y = y * cos + pltpu.roll(y, 96, 1) * sa + pltpu.roll(y, 32, 1) * sb
            ref, off = out_for(col)
            ref[:, off:off + LANES] = y.astype(BF16)


def _inproj(x, g_attn, w, wvt, gain, cos, sa, sb, bd, seq, tm):
    n = x.shape[0]
    nblk_s = seq // tm
    tok = lambda i: (i, 0)
    full = lambda i: (0, 0)
    rope = lambda i: (i % nblk_s, 0)
    return pl.pallas_call(
        _inproj_kernel,
        grid=(n // tm,),
        in_specs=[
            pl.BlockSpec((tm, D_MODEL), tok),
            pl.BlockSpec((1, D_MODEL), full),
            pl.BlockSpec((D_MODEL, NX), full),
            pl.BlockSpec((DA_W, D_MODEL), full),
            pl.BlockSpec((1, NX), full),
            pl.BlockSpec((tm, LANES), rope),
            pl.BlockSpec((tm, LANES), rope),
            pl.BlockSpec((tm, LANES), rope),
            pl.BlockSpec((256, 256), full),
        ],
        out_specs=[
            pl.BlockSpec((tm, DA_W), tok),
            pl.BlockSpec((tm, DA_W), tok),
            pl.BlockSpec((tm, SWX_W), tok),
            pl.BlockSpec((tm, KVB_W), tok),
            pl.BlockSpec((1, DA_W, tm), lambda i: (i // nblk_s, 0, i % nblk_s)),
        ],
        out_shape=[
            jax.ShapeDtypeStruct((n, DA_W), BF16),
            jax.ShapeDtypeStruct((n, DA_W), BF16),
            jax.ShapeDtypeStruct((n, SWX_W), BF16),
            jax.ShapeDtypeStruct((n, KVB_W), BF16),
            jax.ShapeDtypeStruct((n // seq, DA_W, seq), BF16),
        ],
        scratch_shapes=[pltpu.VMEM((tm, D_MODEL), BF16)],
        compiler_params=_cparams(("parallel",)),
        name="inproj",
    )(x, g_attn, w, wvt, gain, cos, sa, sb, bd)


def _diffattn_kernel(q_ref, k_ref, vt_ref, lam4_ref, g_ref, o_ref, m_ref, l_ref, acc_ref, qst_ref, s_ref,
                     *, tq, tk, nk, unroll, lam_init):
    qt = q_ref[0].astype(F32).T
    feat = lax.broadcasted_iota(jnp.int32, qt.shape, 0)
    zero = jnp.zeros_like(qt)
    qst_ref[...] = jnp.concatenate([jnp.where(feat < HEAD_DIM, qt, zero), jnp.where(feat >= HEAD_DIM, qt, zero)],
                                   axis=1).astype(BF16)
    m_ref[...] = jnp.full(m_ref.shape, -jnp.inf, F32)
    l_ref[...] = jnp.zeros(l_ref.shape, F32)
    acc_ref[...] = jnp.zeros(acc_ref.shape, F32)

    def scores(j):
        k0 = pl.multiple_of(j * tk, tk)
        return jnp.dot(k_ref[0, pl.ds(k0, tk), :], qst_ref[...], preferred_element_type=F32)

    def accumulate(s, j):
        k0 = pl.multiple_of(j * tk, tk)
        vtj = vt_ref[0, :, pl.ds(k0, tk)]
        m_old = m_ref[...]
        m_new = jnp.maximum(m_old, jnp.max(s, axis=0, keepdims=True))
        alpha = jnp.exp2(m_old - m_new)
        p = jnp.exp2(s - m_new)
        l_ref[...] = alpha * l_ref[...] + jnp.sum(p, axis=0, keepdims=True)
        acc_ref[...] = alpha * acc_ref[...] + jnp.dot(vtj, p.astype(BF16), preferred_element_type=F32)
        m_ref[...] = m_new

    s_ref[0] = scores(0)

    def body(jj, _):
        j0 = unroll * jj
        for u in range(unroll):
            s_ref[(u + 1) % 2] = scores(jnp.minimum(j0 + u + 1, nk - 1))
            accumulate(s_ref[u % 2], j0 + u)
        return 0

    lax.fori_loop(0, nk // unroll, body, 0)
    ot = acc_ref[...] / l_ref[...]
    lam4 = lam4_ref[...]
    lam = (jnp.exp(jnp.sum(lam4[0:1] * lam4[1:2], axis=-1, keepdims=True))
           - jnp.exp(jnp.sum(lam4[2:3] * lam4[3:4], axis=-1, keepdims=True)) + lam_init)
    o = (ot[:, :tq] - lam * ot[:, tq:]).T
    ms = jnp.mean(o * o, axis=-1, keepdims=True)
    o = o * lax.rsqrt(ms + NORM_EPS) * g_ref[...] * (1.0 - lam_init)
    o_ref[0] = o.astype(BF16)


def _diffattn(qa, ka, vat, lam4, g_sub, lam_init, tq, tk):
    b, s, _ = qa.shape
    nk = s // tk
    unroll = 4 if nk % 4 == 0 else 2
    kern = functools.partial(_diffattn_kernel, tq=tq, tk=tk, nk=nk, unroll=unroll, lam_init=lam_init)
    return pl.pallas_call(
        kern,
        grid=(b, DA_HEADS, s // tq),
        in_specs=[
            pl.BlockSpec((1, tq, LANES), lambda bi, h, i: (bi, i, h)),
            pl.BlockSpec((1, s, LANES), lambda bi, h, i: (bi, 0, h)),
            pl.BlockSpec((1, LANES, s), lambda bi, h, i: (bi, h, 0)),
            pl.BlockSpec((4, HEAD_DIM), lambda bi, h, i: (0, 0)),
            pl.BlockSpec((1, LANES), lambda bi, h, i: (0, 0)),
        ],
        out_specs=pl.BlockSpec((1, tq, LANES), lambda bi, h, i: (bi, i, h)),
        out_shape=jax.ShapeDtypeStruct((b, s, DA_W), BF16),
        scratch_shapes=[
            pltpu.VMEM((1, 2 * tq), F32),
            pltpu.VMEM((1, 2 * tq), F32),
            pltpu.VMEM((LANES, 2 * tq), F32),
            pltpu.VMEM((LANES, 2 * tq), BF16),
            pltpu.VMEM((2, tk, 2 * tq), F32),
        ],
        compiler_params=_cparams(("parallel", "parallel", "parallel")),
        name="diffattn",
    )(qa, ka, vat, lam4, g_sub)


def _swa_kernel(sink_ref, q_ref, kv_ref, o_ref, *, tq, seq):
    i = pl.program_id(1)
    kw = 3 * WINDOW
    for sb in range(tq // WINDOW):
        rows = slice(sb * WINDOW, (sb + 1) * WINDOW)
        q0 = i * tq + sb * WINDOW
        g0 = pl.multiple_of(jnp.clip(q0 - WINDOW, 0, seq - kw), WINDOW)
        kv = kv_ref[0, pl.ds(g0, kw), :]
        k2 = kv[:, 0:LANES]
        v2 = kv[:, LANES:2 * LANES]
        qpos = q0 + lax.broadcasted_iota(jnp.int32, (WINDOW, kw), 0)
        kpos = g0 + lax.broadcasted_iota(jnp.int32, (WINDOW, kw), 1)
        mask = jnp.abs(kpos - qpos) <= WINDOW
        heads = [slice(h * LANES, (h + 1) * LANES) for h in range(SW_Q_HEADS)]
        scores = [lax.dot_general(q_ref[0, rows, cols], k2, (((1,), (1,)), ((), ())), preferred_element_type=F32)
                  for cols in heads]
        for h, cols in enumerate(heads):
            s = jnp.where(mask, scores[h], MASK_VALUE)
            sink = sink_ref[h]
            m = jnp.maximum(jnp.max(s, axis=-1, keepdims=True), sink)
            e = jnp.exp2(s - m)
            den = jnp.sum(e, axis=-1, keepdims=True) + jnp.exp2(sink - m)
            o = jnp.dot(e.astype(BF16), v2, preferred_element_type=F32) * (1.0 / den)
            o_ref[0, rows, cols] = o.astype(BF16)


def _swa(sink, qbx, kvb, tq):
    b, s, _ = qbx.shape
    kern = functools.partial(_swa_kernel, tq=tq, seq=s)
    return pl.pallas_call(
        kern,
        grid_spec=pltpu.PrefetchScalarGridSpec(
            num_scalar_prefetch=1,
            grid=(b, s // tq),
            in_specs=[
                pl.BlockSpec((1, tq, SWX_W), lambda bi, i, sk: (bi, i, 0)),
                pl.BlockSpec((1, s, KVB_W), lambda bi, i, sk: (bi, 0, 0)),
            ],
            out_specs=pl.BlockSpec((1, tq, SWX_W), lambda bi, i, sk: (bi, i, 0)),
        ),
        out_shape=jax.ShapeDtypeStruct((b, s, SWX_W), BF16),
        compiler_params=_cparams(("parallel", "parallel")),
        name="swa",
    )(sink, qbx, kvb)


def _outproj_kernel(x_ref, oa_ref, obx_ref, woa_ref, wobx_ref, g_ref, wrt_ref, x1_ref, h2r_ref, aff_ref):
    x1 = (x_ref[...] + jnp.dot(oa_ref[...], woa_ref[...], preferred_element_type=F32)
          + jnp.dot(obx_ref[...], wobx_ref[...], preferred_element_type=F32))
    x1_ref[...] = x1
    ms = jnp.mean(x1 * x1, axis=-1, keepdims=True)
    h2 = x1 * lax.rsqrt(ms + NORM_EPS) * g_ref[...]
    tm = h2.shape[0]
    for j in range(SUBLANES):
        h2r_ref[pl.ds(j, tm, stride=SUBLANES), :] = h2[:, j * LANES:(j + 1) * LANES]
    h_hi = h2.astype(BF16)
    h_lo = (h2 - h_hi.astype(F32)).astype(BF16)
    w = wrt_ref[...]
    w_hi = w.astype(BF16)
    w_lo = (w - w_hi.astype(F32)).astype(BF16)
    logits = lax.dot_general(jnp.concatenate([w_hi, w_hi, w_lo], axis=1),
                             jnp.concatenate([h_hi, h_lo, h_hi], axis=1),
                             (((1,), (1,)), ((), ())), preferred_element_type=F32)
    mx = jnp.max(logits, axis=0, keepdims=True)
    e = jnp.exp(logits - mx)
    aff_ref[...] = e / jnp.sum(e, axis=0, keepdims=True)


def _outproj(x, oa, obx, w_oa, w_obx, g_ffn, w_rt, tm):
    n = x.shape[0]
    tok = lambda i: (i, 0)
    full = lambda i: (0, 0)
    return pl.pallas_call(
        _outproj_kernel,
        grid=(n // tm,),
        in_specs=[
            pl.BlockSpec((tm, D_MODEL), tok),
            pl.BlockSpec((tm, DA_W), tok),
            pl.BlockSpec((tm, SWX_W), tok),
            pl.BlockSpec((DA_W, D_MODEL), full),
            pl.BlockSpec((SWX_W, D_MODEL), full),
            pl.BlockSpec((1, D_MODEL), full),
            pl.BlockSpec((N_EXPERTS, D_MODEL), full),
        ],
        out_specs=[
            pl.BlockSpec((tm, D_MODEL), tok),
            pl.BlockSpec((tm * SUBLANES, LANES), tok),
            pl.BlockSpec((N_EXPERTS, tm), lambda i: (0, i)),
        ],
        out_shape=[
            jax.ShapeDtypeStruct((n, D_MODEL), F32),
            jax.ShapeDtypeStruct((n * SUBLANES, LANES), F32),
            jax.ShapeDtypeStruct((N_EXPERTS, n), F32),
        ],
        compiler_params=_cparams(("parallel",)),
        name="outproj",
    )(x, oa, obx, w_oa, w_obx, g_ffn, w_rt)


def _route_kernel(aff_ref, loc_ref, vals_ref, cex_ref, thr_ref, *, cap, nchunk, pblk, tchunk):
    n_exp = aff_ref.shape[0]

    def expert_bits(e):
        return pltpu.bitcast(aff_ref[e], jnp.int32)

    def count(mask):
        per_lane = jnp.sum(jnp.where(mask, 1.0, 0.0), axis=0, keepdims=True)
        return jnp.sum(per_lane, axis=1, keepdims=True)

    def search(i, thrs):
        bit = jnp.left_shift(jnp.int32(1), 30 - i)
        out = []
        for e in range(n_exp):
            cand = thrs[e] | bit
            out.append(jnp.where(count(expert_bits(e) >= cand) >= cap, cand, thrs[e]))
        return tuple(out)

    thrs = lax.fori_loop(0, 31, search, tuple(jnp.zeros((1, 1), jnp.int32) for _ in range(n_exp)))
    for e in range(n_exp):
        thr_ref[e] = jnp.broadcast_to(thrs[e], (1, LANES))
    lax.fori_loop(0, n_exp, functools.partial(_route_expert, aff_ref, loc_ref, vals_ref, cex_ref, thr_ref,
                                              count, cap, nchunk, pblk, tchunk), 0)


def _route_expert(aff_ref, loc_ref, vals_ref, cex_ref, thr_ref, count, cap, nchunk, pblk, tchunk, e, carry):
    a = aff_ref[e]
    bits = pltpu.bitcast(a, jnp.int32)
    thr = thr_ref[e]
    gt = bits > thr
    eq = bits == thr
    need = cap - count(gt)

    r128 = lax.broadcasted_iota(jnp.int32, (LANES, LANES), 0)
    c128 = lax.broadcasted_iota(jnp.int32, (LANES, LANES), 1)
    u_incl = jnp.where(r128 <= c128, 1.0, 0.0).astype(BF16)
    ones = jnp.ones((LANES, LANES), BF16)
    rc = lax.broadcasted_iota(jnp.int32, (nchunk, nchunk), 0)
    cc = lax.broadcasted_iota(jnp.int32, (nchunk, nchunk), 1)
    l_strict = jnp.where(cc < rc, 1.0, 0.0).astype(BF16)
    u_strict = jnp.where(rc < cc, 1.0, 0.0).astype(BF16)

    def chunk_scan(mask_bf):
        incl = jnp.dot(mask_bf, u_incl, preferred_element_type=F32)
        tot_b = jnp.dot(mask_bf, ones, preferred_element_type=F32)
        cexcl_b = jnp.dot(l_strict, tot_b.astype(BF16), preferred_element_type=F32)
        return incl, cexcl_b

    eq_bf = jnp.where(eq, 1.0, 0.0).astype(BF16)
    incl_eq, cexcl_eq = chunk_scan(eq_bf)
    rank_eq = cexcl_eq + incl_eq - eq_bf.astype(F32)
    sel = gt | (eq & (rank_eq < need))
    sel_bf = jnp.where(sel, 1.0, 0.0).astype(BF16)
    lc, cexcl_b = chunk_scan(sel_bf)

    tot_row = lax.dot_general(jnp.ones((8, LANES), BF16), sel_bf, (((1,), (1,)), ((), ())),
                              preferred_element_type=F32)
    cexcl_row = jnp.dot(tot_row.astype(BF16), u_strict, preferred_element_type=F32)[0:1]
    cincl_row = cexcl_row + tot_row[0:1]

    a_hi = a.astype(BF16)
    r1 = a - a_hi.astype(F32)
    a_mid = r1.astype(BF16)
    a_lo = (r1 - a_mid.astype(F32)).astype(BF16)
    kidx = lax.broadcasted_iota(jnp.int32, (nchunk, LANES), 0).astype(F32)
    cex_hi = jnp.floor(cexcl_b * (1.0 / LANES))
    cex_lo = cexcl_b - cex_hi * LANES
    table = jnp.concatenate([lc.astype(BF16), a_hi, a_mid, a_lo, kidx.astype(BF16),
                             cex_hi.astype(BF16), cex_lo.astype(BF16)], axis=1)

    lane_f = lax.broadcasted_iota(jnp.int32, (pblk, LANES), 1).astype(F32)
    eye = r128 == c128

    def slot_block(bi, _):
        base = bi * pblk
        pc = (base + lax.broadcasted_iota(jnp.int32, (pblk, nchunk), 0)).astype(F32)
        onehot_k = jnp.where((cexcl_row <= pc) & (pc < cincl_row), 1.0, 0.0).astype(BF16)
        r = jnp.dot(onehot_k, table, preferred_element_type=F32)
        r_lc = r[:, 0:LANES]
        r_aff = (r[:, LANES:2 * LANES] + r[:, 2 * LANES:3 * LANES]) + r[:, 3 * LANES:4 * LANES]
        r_k = r[:, 4 * LANES:5 * LANES]
        r_cex = r[:, 5 * LANES:6 * LANES] * LANES + r[:, 6 * LANES:7 * LANES]
        p_loc = (base + lax.broadcasted_iota(jnp.int32, (pblk, LANES), 0)).astype(F32) - r_cex
        below = jnp.where(r_lc <= p_loc, 1.0, 0.0).astype(BF16)
        t_loc = jnp.dot(below, ones, preferred_element_type=F32)
        val = jnp.sum(jnp.where(lane_f == t_loc, r_aff, 0.0), axis=-1, keepdims=True)
        tok = r_k * LANES + t_loc
        for sb in range(pblk // LANES):
            rows = slice(sb * LANES, (sb + 1) * LANES)
            out = pl.ds(pl.multiple_of(base + sb * LANES, LANES), LANES)
            tok_row = jnp.sum(jnp.where(eye, tok[rows], 0.0), axis=0, keepdims=True).astype(jnp.int32)
            loc_ref[e, :, out] = (tok_row & (tchunk - 1)) * SUBLANES
            vals_ref[e, :, out] = jnp.sum(jnp.where(eye, val[rows], 0.0), axis=0, keepdims=True)
        return 0

    lax.fori_loop(0, cap // pblk, slot_block, 0)
    loc_ref[e, :, pl.ds(cap, SLOT_PAD)] = jnp.zeros((1, SLOT_PAD), jnp.int32)
    cex_ref[e] = cexcl_row.astype(jnp.int32)
    return carry


def _route(aff3, cap, tchunk):
    e, nchunk, _ = aff3.shape
    pblk = min(512, cap)
    kern = functools.partial(_route_kernel, cap=cap, nchunk=nchunk, pblk=pblk, tchunk=tchunk)
    return pl.pallas_call(
        kern,
        out_shape=[
            jax.ShapeDtypeStruct((e, 1, cap + SLOT_PAD), jnp.int32),
            jax.ShapeDtypeStruct((e, 1, cap), F32),
            jax.ShapeDtypeStruct((e, 1, nchunk), jnp.int32),
        ],
        scratch_shapes=[pltpu.VMEM((e, 1, LANES), jnp.int32)],
        compiler_params=pltpu.CompilerParams(vmem_limit_bytes=VMEM_LIMIT),
        name="route",
    )(aff3)


def _expert_kernel(ps_ref, loc_ref, val_ref, h2r_ref, wg_ref, wu_ref, wd_ref, moe_ref, xt_ref, yt_ref,
                   *, rblk, stride):
    c = pl.program_id(0)
    e = pl.program_id(1)

    @pl.when(e == 0)
    def _():
        moe_ref[...] = jnp.zeros(moe_ref.shape, F32)

    p0 = ps_ref[e, c]
    p1 = ps_ref[e, c + 1]
    unroll = SUBLANES

    def tile_row(p):
        return pl.multiple_of(loc_ref[0, 0, p], SUBLANES)

    def block(b, _):
        base = p0 + b * rblk

        def gather(g, _):
            for i in range(2 * unroll):
                r = g * 2 * unroll + i
                xt_ref[pl.ds(r, SUBLANES, stride=stride), :] = h2r_ref[pl.ds(tile_row(base + r), SUBLANES), :]
            return 0

        lax.fori_loop(0, rblk // (2 * unroll), gather, 0)
        x = jnp.concatenate([xt_ref[pl.ds(j * stride, rblk), :].astype(BF16) for j in range(SUBLANES)], axis=1)
        g = jnp.dot(x, wg_ref[0], preferred_element_type=F32)
        u = jnp.dot(x, wu_ref[0], preferred_element_type=F32)
        hmid = (g * jax.nn.sigmoid(g) * u).astype(BF16)
        y = jnp.dot(hmid, wd_ref[0], preferred_element_type=F32)
        for j in range(SUBLANES):
            yt_ref[pl.ds(j * stride, rblk), :] = y[:, j * LANES:(j + 1) * LANES]

        nvalid = jnp.minimum(rblk, p1 - base)

        def scatter_rows(r0, count):
            new = []
            for i in range(count):
                dst = tile_row(base + r0 + i)
                contrib = yt_ref[pl.ds(r0 + i, SUBLANES, stride=stride), :] * val_ref[0, 0, base + r0 + i]
                new.append((dst, moe_ref[pl.ds(dst, SUBLANES), :] + contrib))
            for dst, v in new:
                moe_ref[pl.ds(dst, SUBLANES), :] = v

        def scatter_pair(g, _):
            scatter_rows(g * 2 * unroll, unroll)
            scatter_rows(g * 2 * unroll + unroll, unroll)
            return 0

        npairs = nvalid // (2 * unroll)
        lax.fori_loop(0, npairs, scatter_pair, 0)

        def scatter_one(r, _):
            scatter_rows(r, 1)
            return 0

        lax.fori_loop(npairs * 2 * unroll, nvalid, scatter_one, 0)
        return 0

    lax.fori_loop(0, (p1 - p0 + rblk - 1) // rblk, block, 0)


def _experts(pstart, loc, vals, h2r, wg, wu, wd, tchunk, rblk):
    e, _, cap_pad = loc.shape
    n8 = h2r.shape[0]
    stride = rblk + SUBLANES
    kern = functools.partial(_expert_kernel, rblk=rblk, stride=stride)
    wspec = pl.BlockSpec((1, D_MODEL, D_EXPERT), lambda ci, ei, ps: (ei, 0, 0))
    slot = lambda a: pl.BlockSpec((1, 1, a.shape[2]), lambda ci, ei, ps: (ei, 0, 0), memory_space=pltpu.SMEM)
    chunk = lambda: pl.BlockSpec((tchunk * SUBLANES, LANES), lambda ci, ei, ps: (ci, 0),
                                 pipeline_mode=pl.Buffered(1))
    return pl.pallas_call(
        kern,
        grid_spec=pltpu.PrefetchScalarGridSpec(
            num_scalar_prefetch=1,
            grid=(n8 // (tchunk * SUBLANES), e),
            in_specs=[slot(loc), slot(vals), chunk(), wspec, wspec,
                      pl.BlockSpec((1, D_EXPERT, D_MODEL), lambda ci, ei, ps: (ei, 0, 0))],
            out_specs=chunk(),
            scratch_shapes=[
                pltpu.VMEM((SUBLANES * stride, LANES), F32),
                pltpu.VMEM((SUBLANES * stride, LANES), F32),
            ],
        ),
        out_shape=jax.ShapeDtypeStruct((n8, LANES), F32),
        compiler_params=pltpu.CompilerParams(dimension_semantics=("arbitrary", "arbitrary"),
                                             vmem_limit_bytes=EXPERT_VMEM_LIMIT),
        name="experts",
    )(pstart, loc, vals, h2r, wg, wu, wd)


def _ple_kernel(x_ref, moe_ref, p_ref, g_ref, wg_ref, wp_ref, o_ref):
    tm = x_ref.shape[0]
    moe = jnp.concatenate([moe_ref[pl.ds(j, tm, stride=SUBLANES), :] for j in range(SUBLANES)], axis=1)
    x = x_ref[...] + moe
    ms = jnp.mean(x * x, axis=-1, keepdims=True)
    hn = (x * lax.rsqrt(ms + NORM_EPS) * g_ref[...]).astype(BF16)
    gate = jax.nn.sigmoid(jnp.dot(hn, wg_ref[...], preferred_element_type=F32))
    emb = jnp.dot(p_ref[0].astype(BF16), wp_ref[...], preferred_element_type=F32)
    o_ref[...] = x + gate * emb


def _ple(x, moe_r, p_all, layer, g_ple, w_g, w_p, tm):
    n = x.shape[0]
    tok = lambda i: (i, 0)
    full = lambda i: (0, 0)
    return pl.pallas_call(
        _ple_kernel,
        grid=(n // tm,),
        in_specs=[
            pl.BlockSpec((tm, D_MODEL), tok),
            pl.BlockSpec((tm * SUBLANES, LANES), tok),
            pl.BlockSpec((1, tm, PLE_DIM), lambda i: (layer, i, 0)),
            pl.BlockSpec((1, D_MODEL), full),
            pl.BlockSpec((D_MODEL, D_MODEL), full),
            pl.BlockSpec((PLE_DIM, D_MODEL), full),
        ],
        out_specs=pl.BlockSpec((tm, D_MODEL), tok),
        out_shape=jax.ShapeDtypeStruct((n, D_MODEL), F32),
        compiler_params=_cparams(("parallel",)),
        name="ple",
    )(x, moe_r, p_all, g_ple, w_g, w_p)


def _rope_tables(seq):
    pos = jnp.arange(seq, dtype=F32)
    inv = ROPE_THETA ** (-jnp.arange(0, HEAD_DIM, 2, dtype=F32) / HEAD_DIM)
    ang = pos[:, None] * inv[None, :]
    ang = jnp.concatenate([ang, ang, ang, ang], axis=-1)
    cos, sin = jnp.cos(ang), jnp.sin(ang)
    first_half = (jnp.arange(LANES) % HEAD_DIM) < HEAD_DIM // 2
    sin_a = jnp.where(first_half[None, :], -sin, 0.0)
    sin_b = jnp.where(first_half[None, :], 0.0, sin)
    return cos, sin_a, sin_b


def _expand_sw_cols(w):
    lead = w.shape[:-1]
    w = w.reshape(lead + (SW_Q_HEADS, HEAD_DIM))
    z = jnp.zeros_like(w)
    kvh = (jnp.arange(SW_Q_HEADS) // SW_GROUP)[:, None]
    lo = jnp.where(kvh == 0, w, z)
    hi = jnp.where(kvh == 1, w, z)
    return jnp.concatenate([lo, hi], axis=-1).reshape(lead + (SWX_W,))


def _layer_params(i, attn_norm, w_in, q_norm_a, k_norm_a, subln_a, q_norm_b, k_norm_b, w_out):
    wi = w_in[i]
    w_qa, w_ka, w_va = wi[:, 0:512], wi[:, 512:1024], wi[:, 1024:1536]
    w_qb, w_kb, w_vb = wi[:, 1536:2048], wi[:, 2048:2176], wi[:, 2176:2304]
    w = jnp.concatenate([w_qa, w_ka, _expand_sw_cols(w_qb), w_kb, w_vb], axis=1).astype(BF16)
    wvt = w_va.T.astype(BF16)
    scale = HEAD_DIM ** -0.5
    log2e = math.log2(math.e)
    t2 = lambda g, n: jnp.tile(g, n)
    gain = jnp.concatenate([
        t2(q_norm_a[i], 8) * (scale * log2e), t2(k_norm_a[i], 8), t2(q_norm_b[i], 16) * (scale * log2e),
        t2(k_norm_b[i], 2), jnp.ones((NX - COL_KVB - LANES,), F32)])[None, :]
    wo = w_out[i]
    w_oa = wo[0:DA_W].astype(BF16)
    w_obx = _expand_sw_cols(wo[DA_W:].T).T.astype(BF16)
    lam_init = 0.8 - 0.6 * math.exp(-0.3 * i)
    return dict(g_attn=attn_norm[i][None, :], w=w, wvt=wvt, gain=gain, w_oa=w_oa, w_obx=w_obx,
                g_sub=subln_a[i][None, :], lam_init=lam_init)


def _pick(n, pref):
    t = pref
    while n % t:
        t //= 2
    return t


def _encoder(x, p, attn_norm, w_in, q_norm_a, k_norm_a, lambda_q1, lambda_k1, lambda_q2, lambda_k2,
             subln_a, q_norm_b, k_norm_b, sink_b, w_out, ffn_norm, w_router, w_gate_e, w_up_e, w_down_e,
             ple_norm, w_ple_gate, w_ple_proj):
    b, s, _ = x.shape
    n = b * s
    cap = EC_CAPACITY_FACTOR * n // N_EXPERTS
    tm = _pick(s, 512)
    tq = _pick(s, 512)
    tk = _pick(s // 2, 512)
    tq_sw = _pick(s, 512)
    tchunk = _pick(n, 4096)
    rblk = EXPERT_ROWS
    cos, sin_a, sin_b = _rope_tables(s)
    r = lax.broadcasted_iota(jnp.int32, (256, 256), 0) // HEAD_DIM
    c = lax.broadcasted_iota(jnp.int32, (256, 256), 1) // HEAD_DIM
    bd = (r == c).astype(BF16)
    xf = x.reshape(n, D_MODEL)
    p_all = p.reshape(DEPTH, n, PLE_DIM)
    for i in range(DEPTH):
        lp = _layer_params(i, attn_norm, w_in, q_norm_a, k_norm_a, subln_a, q_norm_b, k_norm_b, w_out)
        qa, ka, qbx, kvb, vat = _inproj(xf, lp["g_attn"], lp["w"], lp["wvt"], lp["gain"], cos, sin_a, sin_b, bd,
                                        s, tm)
        lam4 = jnp.stack([lambda_q1[i], lambda_k1[i], lambda_q2[i], lambda_k2[i]]).astype(F32)
        oa = _diffattn(qa.reshape(b, s, DA_W), ka.reshape(b, s, DA_W), vat, lam4,
                       lp["g_sub"], lp["lam_init"], tq, tk)
        obx = _swa(sink_b[i].astype(F32) * math.log2(math.e), qbx.reshape(b, s, SWX_W),
                   kvb.reshape(b, s, KVB_W), tq_sw)
        x1, h2r, aff_t = _outproj(xf, oa.reshape(n, DA_W), obx.reshape(n, SWX_W), lp["w_oa"], lp["w_obx"],
                                  ffn_norm[i][None, :], w_router[i].T, tm)
        loc, vals, cex = _route(aff_t.reshape(N_EXPERTS, n // LANES, LANES), cap, tchunk)
        pstart = jnp.concatenate([cex[:, 0, ::tchunk // LANES], jnp.full((N_EXPERTS, 1), cap, jnp.int32)], axis=1)
        moe_r = _experts(pstart, loc, vals, h2r, w_gate_e[i].astype(BF16), w_up_e[i].astype(BF16),
                         w_down_e[i].astype(BF16), tchunk, rblk)
        xf = _ple(x1, moe_r, p_all, i, ple_norm[i][None, :], w_ple_gate[i].astype(BF16),
                  w_ple_proj[i].astype(BF16), tm)
    return xf.reshape(b, s, D_MODEL)


def kernel(x_prompt, x_sample, p_prompt, p_sample, attn_norm, w_in, q_norm_a, k_norm_a, lambda_q1, lambda_k1,
           lambda_q2, lambda_k2, subln_a, q_norm_b, k_norm_b, sink_b, w_out, ffn_norm, w_router, w_gate_e,
           w_up_e, w_down_e, ple_norm, w_ple_gate, w_ple_proj):
    ws = (attn_norm, w_in, q_norm_a, k_norm_a, lambda_q1, lambda_k1, lambda_q2, lambda_k2, subln_a, q_norm_b,
          k_norm_b, sink_b, w_out, ffn_norm, w_router, w_gate_e, w_up_e, w_down_e, ple_norm, w_ple_gate,
          w_ple_proj)
    return (_encoder(x_prompt, p_prompt, *ws), _encoder(x_sample, p_sample, *ws))
```

```python
import functools
import math

import jax
import jax.numpy as jnp
from jax import lax
from jax.experimental import pallas as pl
from jax.experimental.pallas import tpu as pltpu

F32 = jnp.float32
BF16 = jnp.bfloat16

D_MODEL = 1024
DEPTH = 4
HEAD_DIM = 64
DA_HEADS = 4
SW_Q_HEADS = 8
SW_KV_HEADS = 2
SW_GROUP = SW_Q_HEADS // SW_KV_HEADS
WINDOW = 128
N_EXPERTS = 16
EC_CAPACITY_FACTOR = 2
D_EXPERT = 1024
PLE_DIM = 256
ROPE_THETA = 10000.0
NORM_EPS = 1e-6
MASK_VALUE = -1e30

LANES = 128
SUBLANES = 8
DA_W = DA_HEADS * 2 * HEAD_DIM
SW_W = SW_Q_HEADS * HEAD_DIM
KVB_W = 2 * LANES
COL_QA, COL_KA, COL_QB, COL_KVB = 0, 512, 1024, 1536
NX = 1792
N_ROPE_TILES = (COL_KVB + LANES) // LANES
VMEM_LIMIT = 48 * 1024 * 1024
EXPERT_VMEM_LIMIT = 56 * 1024 * 1024
EXPERT_ROWS = 288
SLOT_PAD = 512


def _cparams(sem):
    return pltpu.CompilerParams(dimension_semantics=sem, vmem_limit_bytes=VMEM_LIMIT)


def _inproj_kernel(x_ref, gat_ref, w_ref, wvt_ref, gain_ref, cos_ref, sa_ref, sb_ref, bd_ref,
                   qa_ref, ka_ref, qb_ref, kvb_ref, vat_ref, xn_ref):
    x = x_ref[...]
    ms = jnp.mean(x * x, axis=-1, keepdims=True)
    xn_ref[...] = (x * lax.rsqrt(ms + NORM_EPS) * gat_ref[...]).astype(BF16)
    cos = cos_ref[...]
    sa = sa_ref[...]
    sb = sb_ref[...]
    outs = ((qa_ref, COL_QA), (ka_ref, COL_KA), (qb_ref, COL_QB), (kvb_ref, COL_KVB))
    vat_ref[0] = lax.dot_general(wvt_ref[...], xn_ref[...], (((1,), (1,)), ((), ())),
                                 preferred_element_type=F32).astype(BF16)

    def out_for(col):
        for ref, base in reversed(outs):
            if col >= base:
                return ref, col - base
        raise AssertionError

    def project(c):
        return jnp.dot(xn_ref[...], w_ref[:, c * 256:(c + 1) * 256], preferred_element_type=F32)

    nchunks = NX // 256
    p_next = project(0)
    for c in range(nchunks):
        c0 = c * 256
        p = p_next
        if c + 1 < nchunks:
            p_next = project(c + 1)
        normed = [(c0 + t * LANES) // LANES < N_ROPE_TILES for t in range(2)]
        if any(normed):
            ss = jnp.dot((p * p).astype(BF16), bd_ref[...], preferred_element_type=F32)
        for t in range(2):
            col = c0 + t * LANES
            y = p[:, t * LANES:(t + 1) * LANES]
            if normed[t]:
                sst = ss[:, t * LANES:(t + 1) * LANES]
                y = y * lax.rsqrt(sst * (1.0 / HEAD_DIM) + NORM_EPS) * gain_ref[:, col:col + LANES]
                y = y * cos + pltpu.roll(y, 96, 1) * sa + pltpu.roll(y, 32, 1) * sb
            ref, off = out_for(col)
            ref[:, off:off + LANES] = y.astype(BF16)


def _inproj(x, g_attn, w, wvt, gain, cos, sa, sb, bd, seq, tm):
    n = x.shape[0]
    nblk_s = seq // tm
    tok = lambda i: (i, 0)
    full = lambda i: (0, 0)
    rope = lambda i: (i % nblk_s, 0)
    return pl.pallas_call(
        _inproj_kernel,
        grid=(n // tm,),
        in_specs=[
            pl.BlockSpec((tm, D_MODEL), tok),
            pl.BlockSpec((1, D_MODEL), full),
            pl.BlockSpec((D_MODEL, NX), full),
            pl.BlockSpec((DA_W, D_MODEL), full),
            pl.BlockSpec((1, NX), full),
            pl.BlockSpec((tm, LANES), rope),
            pl.BlockSpec((tm, LANES), rope),
            pl.BlockSpec((tm, LANES), rope),
            pl.BlockSpec((256, 256), full),
        ],
        out_specs=[
            pl.BlockSpec((tm, DA_W), tok),
            pl.BlockSpec((tm, DA_W), tok),
            pl.BlockSpec((tm, SW_W), tok),
            pl.BlockSpec((tm, KVB_W), tok),
            pl.BlockSpec((1, DA_W, tm), lambda i: (i // nblk_s, 0, i % nblk_s)),
        ],
        out_shape=[
            jax.ShapeDtypeStruct((n, DA_W), BF16),
            jax.ShapeDtypeStruct((n, DA_W), BF16),
            jax.ShapeDtypeStruct((n, SW_W), BF16),
            jax.ShapeDtypeStruct((n, KVB_W), BF16),
            jax.ShapeDtypeStruct((n // seq, DA_W, seq), BF16),
        ],
        scratch_shapes=[pltpu.VMEM((tm, D_MODEL), BF16)],
        compiler_params=_cparams(("parallel",)),
        name="inproj",
    )(x, g_attn, w, wvt, gain, cos, sa, sb, bd)


def _diffattn_kernel(q_ref, k_ref, vt_ref, lam4_ref, g_ref, o_ref, m_ref, l_ref, acc_ref, qst_ref, s_ref,
                     *, tq, tk, nk, unroll, lam_init):
    qt = q_ref[0].astype(F32).T
    feat = lax.broadcasted_iota(jnp.int32, qt.shape, 0)
    zero = jnp.zeros_like(qt)
    qst_ref[...] = jnp.concatenate([jnp.where(feat < HEAD_DIM, qt, zero), jnp.where(feat >= HEAD_DIM, qt, zero)],
                                   axis=1).astype(BF16)
    m_ref[...] = jnp.full(m_ref.shape, -jnp.inf, F32)
    l_ref[...] = jnp.zeros(l_ref.shape, F32)
    acc_ref[...] = jnp.zeros(acc_ref.shape, F32)

    def scores(j):
        k0 = pl.multiple_of(j * tk, tk)
        return jnp.dot(k_ref[0, pl.ds(k0, tk), :], qst_ref[...], preferred_element_type=F32)

    def accumulate(s, j):
        k0 = pl.multiple_of(j * tk, tk)
        vtj = vt_ref[0, :, pl.ds(k0, tk)]
        m_old = m_ref[...]
        m_new = jnp.maximum(m_old, jnp.max(s, axis=0, keepdims=True))
        alpha = jnp.exp2(m_old - m_new)
        p = jnp.exp2(s - m_new)
        l_ref[...] = alpha * l_ref[...] + jnp.sum(p, axis=0, keepdims=True)
        acc_ref[...] = alpha * acc_ref[...] + jnp.dot(vtj, p.astype(BF16), preferred_element_type=F32)
        m_ref[...] = m_new

    s_ref[0] = scores(0)

    def body(jj, _):
        j0 = unroll * jj
        for u in range(unroll):
            s_ref[(u + 1) % 2] = scores(jnp.minimum(j0 + u + 1, nk - 1))
            accumulate(s_ref[u % 2], j0 + u)
        return 0

    lax.fori_loop(0, nk // unroll, body, 0)
    ot = acc_ref[...] / l_ref[...]
    lam4 = lam4_ref[...]
    lam = (jnp.exp(jnp.sum(lam4[0:1] * lam4[1:2], axis=-1, keepdims=True))
           - jnp.exp(jnp.sum(lam4[2:3] * lam4[3:4], axis=-1, keepdims=True)) + lam_init)
    o = (ot[:, :tq] - lam * ot[:, tq:]).T
    ms = jnp.mean(o * o, axis=-1, keepdims=True)
    o = o * lax.rsqrt(ms + NORM_EPS) * g_ref[...] * (1.0 - lam_init)
    o_ref[0] = o.astype(BF16)


def _diffattn(qa, ka, vat, lam4, g_sub, lam_init, tq, tk):
    b, s, _ = qa.shape
    nk = s // tk
    unroll = 8 if nk % 8 == 0 else (4 if nk % 4 == 0 else 2)
    kern = functools.partial(_diffattn_kernel, tq=tq, tk=tk, nk=nk, unroll=unroll, lam_init=lam_init)
    return pl.pallas_call(
        kern,
        grid=(b, DA_HEADS, s // tq),
        in_specs=[
            pl.BlockSpec((1, tq, LANES), lambda bi, h, i: (bi, i, h)),
            pl.BlockSpec((1, s, LANES), lambda bi, h, i: (bi, 0, h)),
            pl.BlockSpec((1, LANES, s), lambda bi, h, i: (bi, h, 0)),
            pl.BlockSpec((4, HEAD_DIM), lambda bi, h, i: (0, 0)),
            pl.BlockSpec((1, LANES), lambda bi, h, i: (0, 0)),
        ],
        out_specs=pl.BlockSpec((1, tq, LANES), lambda bi, h, i: (bi, i, h)),
        out_shape=jax.ShapeDtypeStruct((b, s, DA_W), BF16),
        scratch_shapes=[
            pltpu.VMEM((1, 2 * tq), F32),
            pltpu.VMEM((1, 2 * tq), F32),
            pltpu.VMEM((LANES, 2 * tq), F32),
            pltpu.VMEM((LANES, 2 * tq), BF16),
            pltpu.VMEM((2, tk, 2 * tq), F32),
        ],
        compiler_params=_cparams(("parallel", "parallel", "parallel")),
        name="diffattn",
    )(qa, ka, vat, lam4, g_sub)


def _swa_kernel(sink_ref, q_ref, kv_ref, o_ref, *, tq, seq):
    i = pl.program_id(1)
    kw = 3 * WINDOW
    for sb in range(tq // WINDOW):
        rows = slice(sb * WINDOW, (sb + 1) * WINDOW)
        q0 = i * tq + sb * WINDOW
        g0 = pl.multiple_of(jnp.clip(q0 - WINDOW, 0, seq - kw), WINDOW)
        kv = kv_ref[0, pl.ds(g0, kw), :]
        k2 = kv[:, 0:LANES]
        v2 = kv[:, LANES:2 * LANES]
        qpos = q0 + lax.broadcasted_iota(jnp.int32, (WINDOW, kw), 0)
        kpos = g0 + lax.broadcasted_iota(jnp.int32, (WINDOW, kw), 1)
        mask = jnp.abs(kpos - qpos) <= WINDOW
        k2s = pltpu.roll(k2.astype(F32), HEAD_DIM, 1).astype(BF16)
        v2s = pltpu.roll(v2.astype(F32), HEAD_DIM, 1).astype(BF16)
        low = lax.broadcasted_iota(jnp.int32, (WINDOW, LANES), 1) < HEAD_DIM
        units = []
        for h in range(SW_Q_HEADS):
            half, kvh = h % 2, h // SW_GROUP
            qp = q_ref[0, rows, (h // 2) * LANES:(h // 2 + 1) * LANES]
            qm = jnp.where(low if half == 0 else jnp.logical_not(low), qp, jnp.zeros_like(qp))
            units.append((qm, k2 if half == kvh else k2s, v2 if half == kvh else v2s))
        scores = [lax.dot_general(qm, kk, (((1,), (1,)), ((), ())), preferred_element_type=F32)
                  for qm, kk, _ in units]
        outs = []
        for h in range(SW_Q_HEADS):
            s = jnp.where(mask, scores[h], MASK_VALUE)
            sink = sink_ref[h]
            m = jnp.maximum(jnp.max(s, axis=-1, keepdims=True), sink)
            e = jnp.exp2(s - m)
            den = jnp.sum(e, axis=-1, keepdims=True) + jnp.exp2(sink - m)
            outs.append(jnp.dot(e.astype(BF16), units[h][2], preferred_element_type=F32) * (1.0 / den))
        for t in range(SW_Q_HEADS // 2):
            o = jnp.where(low, outs[2 * t], outs[2 * t + 1])
            o_ref[0, rows, t * LANES:(t + 1) * LANES] = o.astype(BF16)


def _swa(sink, qbx, kvb, tq):
    b, s, _ = qbx.shape
    kern = functools.partial(_swa_kernel, tq=tq, seq=s)
    return pl.pallas_call(
        kern,
        grid_spec=pltpu.PrefetchScalarGridSpec(
            num_scalar_prefetch=1,
            grid=(b, s // tq),
            in_specs=[
                pl.BlockSpec((1, tq, SW_W), lambda bi, i, sk: (bi, i, 0)),
                pl.BlockSpec((1, s, KVB_W), lambda bi, i, sk: (bi, 0, 0)),
            ],
            out_specs=pl.BlockSpec((1, tq, SW_W), lambda bi, i, sk: (bi, i, 0)),
        ),
        out_shape=jax.ShapeDtypeStruct((b, s, SW_W), BF16),
        compiler_params=_cparams(("parallel", "parallel")),
        name="swa",
    )(sink, qbx, kvb)


def _outproj_kernel(x_ref, oa_ref, obx_ref, woa_ref, wobx_ref, g_ref, wrt_ref, x1_ref, h2r_ref, aff_ref):
    x1 = (x_ref[...] + jnp.dot(oa_ref[...], woa_ref[...], preferred_element_type=F32)
          + jnp.dot(obx_ref[...], wobx_ref[...], preferred_element_type=F32))
    x1_ref[...] = x1
    ms = jnp.mean(x1 * x1, axis=-1, keepdims=True)
    h2 = x1 * lax.rsqrt(ms + NORM_EPS) * g_ref[...]
    tm = h2.shape[0]
    for j in range(SUBLANES):
        h2r_ref[pl.ds(j, tm, stride=SUBLANES), :] = h2[:, j * LANES:(j + 1) * LANES]
    h_hi = h2.astype(BF16)
    h_lo = (h2 - h_hi.astype(F32)).astype(BF16)
    w = wrt_ref[...]
    w_hi = w.astype(BF16)
    w_lo = (w - w_hi.astype(F32)).astype(BF16)
    logits = lax.dot_general(jnp.concatenate([w_hi, w_hi, w_lo], axis=1),
                             jnp.concatenate([h_hi, h_lo, h_hi], axis=1),
                             (((1,), (1,)), ((), ())), preferred_element_type=F32)
    mx = jnp.max(logits, axis=0, keepdims=True)
    e = jnp.exp(logits - mx)
    aff_ref[...] = e / jnp.sum(e, axis=0, keepdims=True)


def _outproj(x, oa, obx, w_oa, w_obx, g_ffn, w_rt, tm):
    n = x.shape[0]
    tok = lambda i: (i, 0)
    full = lambda i: (0, 0)
    return pl.pallas_call(
        _outproj_kernel,
        grid=(n // tm,),
        in_specs=[
            pl.BlockSpec((tm, D_MODEL), tok),
            pl.BlockSpec((tm, DA_W), tok),
            pl.BlockSpec((tm, SW_W), tok),
            pl.BlockSpec((DA_W, D_MODEL), full),
            pl.BlockSpec((SW_W, D_MODEL), full),
            pl.BlockSpec((1, D_MODEL), full),
            pl.BlockSpec((N_EXPERTS, D_MODEL), full),
        ],
        out_specs=[
            pl.BlockSpec((tm, D_MODEL), tok),
            pl.BlockSpec((tm * SUBLANES, LANES), tok),
            pl.BlockSpec((N_EXPERTS, tm), lambda i: (0, i)),
        ],
        out_shape=[
            jax.ShapeDtypeStruct((n, D_MODEL), F32),
            jax.ShapeDtypeStruct((n * SUBLANES, LANES), F32),
            jax.ShapeDtypeStruct((N_EXPERTS, n), F32),
        ],
        compiler_params=_cparams(("parallel",)),
        name="outproj",
    )(x, oa, obx, w_oa, w_obx, g_ffn, w_rt)


def _route_kernel(aff_ref, loc_ref, vals_ref, cex_ref, thr_ref, *, cap, nchunk, pblk, tchunk):
    n_exp = aff_ref.shape[0]

    def expert_bits(e):
        return pltpu.bitcast(aff_ref[e], jnp.int32)

    def count(mask):
        per_lane = jnp.sum(jnp.where(mask, 1.0, 0.0), axis=0, keepdims=True)
        return jnp.sum(per_lane, axis=1, keepdims=True)

    def search(i, thrs):
        bit = jnp.left_shift(jnp.int32(1), 30 - i)
        out = []
        for e in range(n_exp):
            cand = thrs[e] | bit
            out.append(jnp.where(count(expert_bits(e) >= cand) >= cap, cand, thrs[e]))
        return tuple(out)

    thrs = lax.fori_loop(0, 31, search, tuple(jnp.zeros((1, 1), jnp.int32) for _ in range(n_exp)))
    for e in range(n_exp):
        thr_ref[e] = jnp.broadcast_to(thrs[e], (1, LANES))
    lax.fori_loop(0, n_exp, functools.partial(_route_expert, aff_ref, loc_ref, vals_ref, cex_ref, thr_ref,
                                              count, cap, nchunk, pblk, tchunk), 0)


def _route_expert(aff_ref, loc_ref, vals_ref, cex_ref, thr_ref, count, cap, nchunk, pblk, tchunk, e, carry):
    a = aff_ref[e]
    bits = pltpu.bitcast(a, jnp.int32)
    thr = thr_ref[e]
    gt = bits > thr
    eq = bits == thr
    need = cap - count(gt)

    r128 = lax.broadcasted_iota(jnp.int32, (LANES, LANES), 0)
    c128 = lax.broadcasted_iota(jnp.int32, (LANES, LANES), 1)
    u_incl = jnp.where(r128 <= c128, 1.0, 0.0).astype(BF16)
    ones = jnp.ones((LANES, LANES), BF16)
    rc = lax.broadcasted_iota(jnp.int32, (nchunk, nchunk), 0)
    cc = lax.broadcasted_iota(jnp.int32, (nchunk, nchunk), 1)
    l_strict = jnp.where(cc < rc, 1.0, 0.0).astype(BF16)
    u_strict = jnp.where(rc < cc, 1.0, 0.0).astype(BF16)

    def chunk_scan(mask_bf):
        incl = jnp.dot(mask_bf, u_incl, preferred_element_type=F32)
        tot_b = jnp.dot(mask_bf, ones, preferred_element_type=F32)
        cexcl_b = jnp.dot(l_strict, tot_b.astype(BF16), preferred_element_type=F32)
        return incl, cexcl_b

    eq_bf = jnp.where(eq, 1.0, 0.0).astype(BF16)
    incl_eq, cexcl_eq = chunk_scan(eq_bf)
    rank_eq = cexcl_eq + incl_eq - eq_bf.astype(F32)
    sel = gt | (eq & (rank_eq < need))
    sel_bf = jnp.where(sel, 1.0, 0.0).astype(BF16)
    lc, cexcl_b = chunk_scan(sel_bf)

    tot_row = lax.dot_general(jnp.ones((8, LANES), BF16), sel_bf, (((1,), (1,)), ((), ())),
                              preferred_element_type=F32)
    cexcl_row = jnp.dot(tot_row.astype(BF16), u_strict, preferred_element_type=F32)[0:1]
    cincl_row = cexcl_row + tot_row[0:1]

    a_hi = a.astype(BF16)
    r1 = a - a_hi.astype(F32)
    a_mid = r1.astype(BF16)
    a_lo = (r1 - a_mid.astype(F32)).astype(BF16)
    kidx = lax.broadcasted_iota(jnp.int32, (nchunk, LANES), 0).astype(F32)
    cex_hi = jnp.floor(cexcl_b * (1.0 / LANES))
    cex_lo = cexcl_b - cex_hi * LANES
    table = jnp.concatenate([lc.astype(BF16), a_hi, a_mid, a_lo, kidx.astype(BF16),
                             cex_hi.astype(BF16), cex_lo.astype(BF16)], axis=1)

    lane_f = lax.broadcasted_iota(jnp.int32, (pblk, LANES), 1).astype(F32)
    eye = r128 == c128

    def slot_block(bi, _):
        base = bi * pblk
        pc = (base + lax.broadcasted_iota(jnp.int32, (pblk, nchunk), 0)).astype(F32)
        onehot_k = jnp.where((cexcl_row <= pc) & (pc < cincl_row), 1.0, 0.0).astype(BF16)
        r = jnp.dot(onehot_k, table, preferred_element_type=F32)
        r_lc = r[:, 0:LANES]
        r_aff = (r[:, LANES:2 * LANES] + r[:, 2 * LANES:3 * LANES]) + r[:, 3 * LANES:4 * LANES]
        r_k = r[:, 4 * LANES:5 * LANES]
        r_cex = r[:, 5 * LANES:6 * LANES] * LANES + r[:, 6 * LANES:7 * LANES]
        p_loc = (base + lax.broadcasted_iota(jnp.int32, (pblk, LANES), 0)).astype(F32) - r_cex
        below = jnp.where(r_lc <= p_loc, 1.0, 0.0).astype(BF16)
        t_loc = jnp.dot(below, ones, preferred_element_type=F32)
        val = jnp.sum(jnp.where(lane_f == t_loc, r_aff, 0.0), axis=-1, keepdims=True)
        tok = r_k * LANES + t_loc
        for sb in range(pblk // LANES):
            rows = slice(sb * LANES, (sb + 1) * LANES)
            out = pl.ds(pl.multiple_of(base + sb * LANES, LANES), LANES)
            tok_row = jnp.sum(jnp.where(eye, tok[rows], 0.0), axis=0, keepdims=True).astype(jnp.int32)
            loc_ref[e, :, out] = (tok_row & (tchunk - 1)) * SUBLANES
            vals_ref[e, :, out] = jnp.sum(jnp.where(eye, val[rows], 0.0), axis=0, keepdims=True)
        return 0

    lax.fori_loop(0, cap // pblk, slot_block, 0)
    loc_ref[e, :, pl.ds(cap, SLOT_PAD)] = jnp.zeros((1, SLOT_PAD), jnp.int32)
    cex_ref[e] = cexcl_row.astype(jnp.int32)
    return carry


def _route(aff3, cap, tchunk):
    e, nchunk, _ = aff3.shape
    pblk = min(512, cap)
    kern = functools.partial(_route_kernel, cap=cap, nchunk=nchunk, pblk=pblk, tchunk=tchunk)
    return pl.pallas_call(
        kern,
        out_shape=[
            jax.ShapeDtypeStruct((e, 1, cap + SLOT_PAD), jnp.int32),
            jax.ShapeDtypeStruct((e, 1, cap), F32),
            jax.ShapeDtypeStruct((e, 1, nchunk), jnp.int32),
        ],
        scratch_shapes=[pltpu.VMEM((e, 1, LANES), jnp.int32)],
        compiler_params=pltpu.CompilerParams(vmem_limit_bytes=VMEM_LIMIT),
        name="route",
    )(aff3)


def _expert_kernel(ps_ref, loc_ref, val_ref, h2r_ref, wg_ref, wu_ref, wd_ref, moe_ref, xt_ref, yt_ref,
                   *, rblk, stride):
    c = pl.program_id(0)
    e = pl.program_id(1)

    @pl.when(e == 0)
    def _():
        moe_ref[...] = jnp.zeros(moe_ref.shape, F32)

    p0 = ps_ref[e, c]
    p1 = ps_ref[e, c + 1]
    unroll = SUBLANES

    def tile_row(p):
        return pl.multiple_of(loc_ref[0, 0, p], SUBLANES)

    nblk = (p1 - p0 + rblk - 1) // rblk

    def gather_rows(base, r0, count):
        for i in range(count):
            src = tile_row(base + r0 + i)
            xt_ref[pl.ds(r0 + i, SUBLANES, stride=stride), :] = h2r_ref[pl.ds(src, SUBLANES), :]

    @pl.when(nblk > 0)
    def _():
        def first_gather(g, _):
            gather_rows(p0, g * 2 * unroll, 2 * unroll)
            return 0

        lax.fori_loop(0, rblk // (2 * unroll), first_gather, 0)

    def block(b, _):
        base = p0 + b * rblk
        x = jnp.concatenate([xt_ref[pl.ds(j * stride, rblk), :].astype(BF16) for j in range(SUBLANES)], axis=1)
        g = jnp.dot(x, wg_ref[0], preferred_element_type=F32)
        u = jnp.dot(x, wu_ref[0], preferred_element_type=F32)
        hmid = (g * jax.nn.sigmoid(g) * u).astype(BF16)
        y = jnp.dot(hmid, wd_ref[0], preferred_element_type=F32)
        for j in range(SUBLANES):
            yt_ref[pl.ds(j * stride, rblk), :] = y[:, j * LANES:(j + 1) * LANES]

        nvalid = jnp.minimum(rblk, p1 - base)

        def scatter_rows(r0, count):
            new = []
            for i in range(count):
                dst = tile_row(base + r0 + i)
                contrib = yt_ref[pl.ds(r0 + i, SUBLANES, stride=stride), :] * val_ref[0, 0, base + r0 + i]
                new.append((dst, moe_ref[pl.ds(dst, SUBLANES), :] + contrib))
            for dst, v in new:
                moe_ref[pl.ds(dst, SUBLANES), :] = v

        def scatter_pair(g, _):
            scatter_rows(g * 2 * unroll, unroll)
            scatter_rows(g * 2 * unroll + unroll, unroll)
            return 0

        has_next = b + 1 < nblk

        @pl.when(has_next)
        def _():
            def scatter_and_gather(g, _):
                scatter_pair(g, 0)
                gather_rows(base + rblk, g * 2 * unroll, 2 * unroll)
                return 0

            lax.fori_loop(0, rblk // (2 * unroll), scatter_and_gather, 0)

        @pl.when(jnp.logical_not(has_next))
        def _():
            npairs = nvalid // (2 * unroll)
            lax.fori_loop(0, npairs, scatter_pair, 0)

            def scatter_one(r, _):
                scatter_rows(r, 1)
                return 0

            lax.fori_loop(npairs * 2 * unroll, nvalid, scatter_one, 0)

        return 0

    lax.fori_loop(0, nblk, block, 0)


def _experts(pstart, loc, vals, h2r, wg, wu, wd, tchunk, rblk):
    e, _, cap_pad = loc.shape
    n8 = h2r.shape[0]
    stride = rblk + SUBLANES
    kern = functools.partial(_expert_kernel, rblk=rblk, stride=stride)
    wspec = pl.BlockSpec((1, D_MODEL, D_EXPERT), lambda ci, ei, ps: (ei, 0, 0))
    slot = lambda a: pl.BlockSpec((1, 1, a.shape[2]), lambda ci, ei, ps: (ei, 0, 0), memory_space=pltpu.SMEM)
    chunk = lambda: pl.BlockSpec((tchunk * SUBLANES, LANES), lambda ci, ei, ps: (ci, 0),
                                 pipeline_mode=pl.Buffered(1))
    return pl.pallas_call(
        kern,
        grid_spec=pltpu.PrefetchScalarGridSpec(
            num_scalar_prefetch=1,
            grid=(n8 // (tchunk * SUBLANES), e),
            in_specs=[slot(loc), slot(vals), chunk(), wspec, wspec,
                      pl.BlockSpec((1, D_EXPERT, D_MODEL), lambda ci, ei, ps: (ei, 0, 0))],
            out_specs=chunk(),
            scratch_shapes=[
                pltpu.VMEM((SUBLANES * stride, LANES), F32),
                pltpu.VMEM((SUBLANES * stride, LANES), F32),
            ],
        ),
        out_shape=jax.ShapeDtypeStruct((n8, LANES), F32),
        compiler_params=pltpu.CompilerParams(dimension_semantics=("arbitrary", "arbitrary"),
                                             vmem_limit_bytes=EXPERT_VMEM_LIMIT),
        name="experts",
    )(pstart, loc, vals, h2r, wg, wu, wd)


def _ple_kernel(x_ref, moe_ref, p_ref, g_ref, wg_ref, wp_ref, o_ref):
    tm = x_ref.shape[0]
    moe = jnp.concatenate([moe_ref[pl.ds(j, tm, stride=SUBLANES), :] for j in range(SUBLANES)], axis=1)
    x = x_ref[...] + moe
    ms = jnp.mean(x * x, axis=-1, keepdims=True)
    hn = (x * lax.rsqrt(ms + NORM_EPS) * g_ref[...]).astype(BF16)
    gate = jax.nn.sigmoid(jnp.dot(hn, wg_ref[...], preferred_element_type=F32))
    emb = jnp.dot(p_ref[0].astype(BF16), wp_ref[...], preferred_element_type=F32)
    o_ref[...] = x + gate * emb


def _ple(x, moe_r, p_all, layer, g_ple, w_g, w_p, tm):
    n = x.shape[0]
    tok = lambda i: (i, 0)
    full = lambda i: (0, 0)
    return pl.pallas_call(
        _ple_kernel,
        grid=(n // tm,),
        in_specs=[
            pl.BlockSpec((tm, D_MODEL), tok),
            pl.BlockSpec((tm * SUBLANES, LANES), tok),
            pl.BlockSpec((1, tm, PLE_DIM), lambda i: (layer, i, 0)),
            pl.BlockSpec((1, D_MODEL), full),
            pl.BlockSpec((D_MODEL, D_MODEL), full),
            pl.BlockSpec((PLE_DIM, D_MODEL), full),
        ],
        out_specs=pl.BlockSpec((tm, D_MODEL), tok),
        out_shape=jax.ShapeDtypeStruct((n, D_MODEL), F32),
        compiler_params=_cparams(("parallel",)),
        name="ple",
    )(x, moe_r, p_all, g_ple, w_g, w_p)


def _rope_tables(seq):
    pos = jnp.arange(seq, dtype=F32)
    inv = ROPE_THETA ** (-jnp.arange(0, HEAD_DIM, 2, dtype=F32) / HEAD_DIM)
    ang = pos[:, None] * inv[None, :]
    ang = jnp.concatenate([ang, ang, ang, ang], axis=-1)
    cos, sin = jnp.cos(ang), jnp.sin(ang)
    first_half = (jnp.arange(LANES) % HEAD_DIM) < HEAD_DIM // 2
    sin_a = jnp.where(first_half[None, :], -sin, 0.0)
    sin_b = jnp.where(first_half[None, :], 0.0, sin)
    return cos, sin_a, sin_b


def _layer_params(i, attn_norm, w_in, q_norm_a, k_norm_a, subln_a, q_norm_b, k_norm_b, w_out):
    wi = w_in[i]
    w = jnp.concatenate([wi[:, 0:1024], wi[:, 1536:2304]], axis=1).astype(BF16)
    wvt = wi[:, 1024:1536].T.astype(BF16)
    scale = HEAD_DIM ** -0.5
    log2e = math.log2(math.e)
    t2 = lambda g, n: jnp.tile(g, n)
    gain = jnp.concatenate([
        t2(q_norm_a[i], 8) * (scale * log2e), t2(k_norm_a[i], 8), t2(q_norm_b[i], 8) * (scale * log2e),
        t2(k_norm_b[i], 2), jnp.ones((NX - COL_KVB - LANES,), F32)])[None, :]
    wo = w_out[i].astype(BF16)
    lam_init = 0.8 - 0.6 * math.exp(-0.3 * i)
    return dict(g_attn=attn_norm[i][None, :], w=w, wvt=wvt, gain=gain, w_oa=wo[0:DA_W], w_ob=wo[DA_W:],
                g_sub=subln_a[i][None, :], lam_init=lam_init)


def _pick(n, pref):
    t = pref
    while n % t:
        t //= 2
    return t


def _encoder(x, p, attn_norm, w_in, q_norm_a, k_norm_a, lambda_q1, lambda_k1, lambda_q2, lambda_k2,
             subln_a, q_norm_b, k_norm_b, sink_b, w_out, ffn_norm, w_router, w_gate_e, w_up_e, w_down_e,
             ple_norm, w_ple_gate, w_ple_proj):
    b, s, _ = x.shape
    n = b * s
    cap = EC_CAPACITY_FACTOR * n // N_EXPERTS
    tm = _pick(s, 512)
    tq = _pick(s, 512)
    tk = _pick(s // 2, 512)
    tq_sw = _pick(s, 512)
    tchunk = _pick(n, 4096)
    rblk = EXPERT_ROWS
    cos, sin_a, sin_b = _rope_tables(s)
    r = lax.broadcasted_iota(jnp.int32, (256, 256), 0) // HEAD_DIM
    c = lax.broadcasted_iota(jnp.int32, (256, 256), 1) // HEAD_DIM
    bd = (r == c).astype(BF16)
    xf = x.reshape(n, D_MODEL)
    p_all = p.reshape(DEPTH, n, PLE_DIM)
    for i in range(DEPTH):
        lp = _layer_params(i, attn_norm, w_in, q_norm_a, k_norm_a, subln_a, q_norm_b, k_norm_b, w_out)
        qa, ka, qb, kvb, vat = _inproj(xf, lp["g_attn"], lp["w"], lp["wvt"], lp["gain"], cos, sin_a, sin_b, bd,
                                        s, tm)
        lam4 = jnp.stack([lambda_q1[i], lambda_k1[i], lambda_q2[i], lambda_k2[i]]).astype(F32)
        oa = _diffattn(qa.reshape(b, s, DA_W), ka.reshape(b, s, DA_W), vat, lam4,
                       lp["g_sub"], lp["lam_init"], tq, tk)
        ob = _swa(sink_b[i].astype(F32) * math.log2(math.e), qb.reshape(b, s, SW_W),
                   kvb.reshape(b, s, KVB_W), tq_sw)
        x1, h2r, aff_t = _outproj(xf, oa.reshape(n, DA_W), ob.reshape(n, SW_W), lp["w_oa"], lp["w_ob"],
                                  ffn_norm[i][None, :], w_router[i].T, tm)
        loc, vals, cex = _route(aff_t.reshape(N_EXPERTS, n // LANES, LANES), cap, tchunk)
        pstart = jnp.concatenate([cex[:, 0, ::tchunk // LANES], jnp.full((N_EXPERTS, 1), cap, jnp.int32)], axis=1)
        moe_r = _experts(pstart, loc, vals, h2r, w_gate_e[i].astype(BF16), w_up_e[i].astype(BF16),
                         w_down_e[i].astype(BF16), tchunk, rblk)
        xf = _ple(x1, moe_r, p_all, i, ple_norm[i][None, :], w_ple_gate[i].astype(BF16),
                  w_ple_proj[i].astype(BF16), tm)
    return xf.reshape(b, s, D_MODEL)


def kernel(x_prompt, x_sample, p_prompt, p_sample, attn_norm, w_in, q_norm_a, k_norm_a, lambda_q1, lambda_k1,
           lambda_q2, lambda_k2, subln_a, q_norm_b, k_norm_b, sink_b, w_out, ffn_norm, w_router, w_gate_e,
           w_up_e, w_down_e, ple_norm, w_ple_gate, w_ple_proj):
    ws = (attn_norm, w_in, q_norm_a, k_norm_a, lambda_q1, lambda_k1, lambda_q2, lambda_k2, subln_a, q_norm_b,
          k_norm_b, sink_b, w_out, ffn_norm, w_router, w_gate_e, w_up_e, w_down_e, ple_norm, w_ple_gate,
          w_ple_proj)
    return (_encoder(x_prompt, p_prompt, *ws), _encoder(x_sample, p_sample, *ws))
```

```python
import functools
import math

import jax
import jax.numpy as jnp
from jax import lax
from jax.experimental import pallas as pl
from jax.experimental.pallas import tpu as pltpu

F32 = jnp.float32
BF16 = jnp.bfloat16

D_MODEL = 1024
DEPTH = 4
HEAD_DIM = 64
DA_HEADS = 4
SW_Q_HEADS = 8
SW_KV_HEADS = 2
SW_GROUP = SW_Q_HEADS // SW_KV_HEADS
WINDOW = 128
N_EXPERTS = 16
EC_CAPACITY_FACTOR = 2
D_EXPERT = 1024
PLE_DIM = 256
ROPE_THETA = 10000.0
NORM_EPS = 1e-6
MASK_VALUE = -1e30

LANES = 128
SUBLANES = 8
DA_W = DA_HEADS * 2 * HEAD_DIM
SW_W = SW_Q_HEADS * HEAD_DIM
KVB_W = 2 * LANES
COL_QA, COL_KA, COL_QB, COL_KVB = 0, 512, 1024, 1536
NX = 1792
N_ROPE_TILES = (COL_KVB + LANES) // LANES
VMEM_LIMIT = 48 * 1024 * 1024
EXPERT_VMEM_LIMIT = 56 * 1024 * 1024
EXPERT_ROWS = 288
SLOT_PAD = 512


def _cparams(sem):
    return pltpu.CompilerParams(dimension_semantics=sem, vmem_limit_bytes=VMEM_LIMIT)


def _inproj_kernel(x_ref, gat_ref, w_ref, wvt_ref, gain_ref, cos_ref, sa_ref, sb_ref, bd_ref,
                   qa_ref, ka_ref, qb_ref, kvb_ref, vat_ref, xn_ref):
    x = x_ref[...]
    ms = jnp.mean(x * x, axis=-1, keepdims=True)
    xn_ref[...] = (x * lax.rsqrt(ms + NORM_EPS) * gat_ref[...]).astype(BF16)
    cos = cos_ref[...]
    sa = sa_ref[...]
    sb = sb_ref[...]
    outs = ((qa_ref, COL_QA), (ka_ref, COL_KA), (qb_ref, COL_QB), (kvb_ref, COL_KVB))
    vat_ref[0] = lax.dot_general(wvt_ref[...], xn_ref[...], (((1,), (1,)), ((), ())),
                                 preferred_element_type=F32).astype(BF16)

    def out_for(col):
        for ref, base in reversed(outs):
            if col >= base:
                return ref, col - base
        raise AssertionError

    def project(c):
        return jnp.dot(xn_ref[...], w_ref[:, c * 256:(c + 1) * 256], preferred_element_type=F32)

    nchunks = NX // 256
    p_next = project(0)
    for c in range(nchunks):
        c0 = c * 256
        p = p_next
        if c + 1 < nchunks:
            p_next = project(c + 1)
        normed = [(c0 + t * LANES) // LANES < N_ROPE_TILES for t in range(2)]
        if any(normed):
            ss = jnp.dot((p * p).astype(BF16), bd_ref[...], preferred_element_type=F32)
        for t in range(2):
            col = c0 + t * LANES
            y = p[:, t * LANES:(t + 1) * LANES]
            if normed[t]:
                sst = ss[:, t * LANES:(t + 1) * LANES]
                y = y * lax.rsqrt(sst * (1.0 / HEAD_DIM) + NORM_EPS) * gain_ref[:, col:col + LANES]
                y = y * cos + pltpu.roll(y, 96, 1) * sa + pltpu.roll(y, 32, 1) * sb
            ref, off = out_for(col)
            ref[:, off:off + LANES] = y.astype(BF16)


def _inproj(x, g_attn, w, wvt, gain, cos, sa, sb, bd, seq, tm):
    n = x.shape[0]
    nblk_s = seq // tm
    tok = lambda i: (i, 0)
    full = lambda i: (0, 0)
    rope = lambda i: (i % nblk_s, 0)
    return pl.pallas_call(
        _inproj_kernel,
        grid=(n // tm,),
        in_specs=[
            pl.BlockSpec((tm, D_MODEL), tok),
            pl.BlockSpec((1, D_MODEL), full),
            pl.BlockSpec((D_MODEL, NX), full),
            pl.BlockSpec((DA_W, D_MODEL), full),
            pl.BlockSpec((1, NX), full),
            pl.BlockSpec((tm, LANES), rope),
            pl.BlockSpec((tm, LANES), rope),
            pl.BlockSpec((tm, LANES), rope),
            pl.BlockSpec((256, 256), full),
        ],
        out_specs=[
            pl.BlockSpec((tm, DA_W), tok),
            pl.BlockSpec((tm, DA_W), tok),
            pl.BlockSpec((tm, SW_W), tok),
            pl.BlockSpec((tm, KVB_W), tok),
            pl.BlockSpec((1, DA_W, tm), lambda i: (i // nblk_s, 0, i % nblk_s)),
        ],
        out_shape=[
            jax.ShapeDtypeStruct((n, DA_W), BF16),
            jax.ShapeDtypeStruct((n, DA_W), BF16),
            jax.ShapeDtypeStruct((n, SW_W), BF16),
            jax.ShapeDtypeStruct((n, KVB_W), BF16),
            jax.ShapeDtypeStruct((n // seq, DA_W, seq), BF16),
        ],
        scratch_shapes=[pltpu.VMEM((tm, D_MODEL), BF16)],
        compiler_params=_cparams(("parallel",)),
        name="inproj",
    )(x, g_attn, w, wvt, gain, cos, sa, sb, bd)


def _diffattn_kernel(q_ref, k_ref, vt_ref, lam4_ref, g_ref, o_ref, m_ref, l_ref, acc_ref, qst_ref, s_ref,
                     *, tq, tk, nk, nq, unroll, lam_init):
    def load_queries(i):
        q0 = pl.multiple_of(i * tq, tq)
        qt = q_ref[0, pl.ds(q0, tq), :].astype(F32).T
        feat = lax.broadcasted_iota(jnp.int32, qt.shape, 0)
        zero = jnp.zeros_like(qt)
        qst_ref[...] = jnp.concatenate([jnp.where(feat < HEAD_DIM, qt, zero), jnp.where(feat >= HEAD_DIM, qt, zero)],
                                       axis=1).astype(BF16)

    def reset_state():
        m_ref[...] = jnp.full(m_ref.shape, -jnp.inf, F32)
        l_ref[...] = jnp.zeros(l_ref.shape, F32)
        acc_ref[...] = jnp.zeros(acc_ref.shape, F32)

    def scores(j):
        k0 = pl.multiple_of(j * tk, tk)
        return jnp.dot(k_ref[0, pl.ds(k0, tk), :], qst_ref[...], preferred_element_type=F32)

    def accumulate(s, j):
        k0 = pl.multiple_of(j * tk, tk)
        vtj = vt_ref[0, :, pl.ds(k0, tk)]
        m_old = m_ref[...]
        m_new = jnp.maximum(m_old, jnp.max(s, axis=0, keepdims=True))
        alpha = jnp.exp2(m_old - m_new)
        p = jnp.exp2(s - m_new)
        l_ref[...] = alpha * l_ref[...] + jnp.sum(p, axis=0, keepdims=True)
        acc_ref[...] = alpha * acc_ref[...] + jnp.dot(vtj, p.astype(BF16), preferred_element_type=F32)
        m_ref[...] = m_new

    def finalize(i):
        ot = acc_ref[...] / l_ref[...]
        lam4 = lam4_ref[...]
        lam = (jnp.exp(jnp.sum(lam4[0:1] * lam4[1:2], axis=-1, keepdims=True))
               - jnp.exp(jnp.sum(lam4[2:3] * lam4[3:4], axis=-1, keepdims=True)) + lam_init)
        o = (ot[:, :tq] - lam * ot[:, tq:]).T
        ms = jnp.mean(o * o, axis=-1, keepdims=True)
        o = o * lax.rsqrt(ms + NORM_EPS) * g_ref[...] * (1.0 - lam_init)
        o_ref[0, pl.ds(pl.multiple_of(i * tq, tq), tq), :] = o.astype(BF16)

    def trip(jj, _):
        j0 = unroll * jj
        for u in range(unroll):
            s_ref[(u + 1) % 2] = scores(j0 + u + 1)
            accumulate(s_ref[u % 2], j0 + u)
        return 0

    def query_tile(i, _):
        lax.fori_loop(0, nk // unroll - 1, trip, 0)
        j0 = nk - unroll
        for u in range(unroll - 1):
            s_ref[(u + 1) % 2] = scores(j0 + u + 1)
            accumulate(s_ref[u % 2], j0 + u)
        load_queries(jnp.minimum(i + 1, nq - 1))
        s_ref[0] = scores(0)
        accumulate(s_ref[1], nk - 1)
        finalize(i)
        reset_state()
        return 0

    load_queries(0)
    reset_state()
    s_ref[0] = scores(0)
    lax.fori_loop(0, nq, query_tile, 0)


def _diffattn(qa, ka, vat, lam4, g_sub, lam_init, tq, tk):
    b, s, _ = qa.shape
    nk = s // tk
    unroll = 8 if nk % 8 == 0 else (4 if nk % 4 == 0 else 2)
    kern = functools.partial(_diffattn_kernel, tq=tq, tk=tk, nk=nk, nq=s // tq, unroll=unroll, lam_init=lam_init)
    per_head = lambda bi, h: (bi, 0, h)
    return pl.pallas_call(
        kern,
        grid=(b, DA_HEADS),
        in_specs=[
            pl.BlockSpec((1, s, LANES), per_head),
            pl.BlockSpec((1, s, LANES), per_head),
            pl.BlockSpec((1, LANES, s), lambda bi, h: (bi, h, 0)),
            pl.BlockSpec((4, HEAD_DIM), lambda bi, h: (0, 0)),
            pl.BlockSpec((1, LANES), lambda bi, h: (0, 0)),
        ],
        out_specs=pl.BlockSpec((1, s, LANES), per_head),
        out_shape=jax.ShapeDtypeStruct((b, s, DA_W), BF16),
        scratch_shapes=[
            pltpu.VMEM((1, 2 * tq), F32),
            pltpu.VMEM((1, 2 * tq), F32),
            pltpu.VMEM((LANES, 2 * tq), F32),
            pltpu.VMEM((LANES, 2 * tq), BF16),
            pltpu.VMEM((2, tk, 2 * tq), F32),
        ],
        compiler_params=_cparams(("parallel", "parallel")),
        name="diffattn",
    )(qa, ka, vat, lam4, g_sub)


def _swa_kernel(sink_ref, q_ref, kv_ref, o_ref, *, tq, seq):
    i = pl.program_id(1)
    kw = 3 * WINDOW
    low = lax.broadcasted_iota(jnp.int32, (WINDOW, LANES), 1) < HEAD_DIM
    blocks = {}

    def block_operands(sb):
        if sb not in blocks:
            q0 = i * tq + sb * WINDOW
            g0 = pl.multiple_of(jnp.clip(q0 - WINDOW, 0, seq - kw), WINDOW)
            kv = kv_ref[0, pl.ds(g0, kw), :]
            k2 = kv[:, 0:LANES]
            v2 = kv[:, LANES:2 * LANES]
            qpos = q0 + lax.broadcasted_iota(jnp.int32, (WINDOW, kw), 0)
            kpos = g0 + lax.broadcasted_iota(jnp.int32, (WINDOW, kw), 1)
            blocks[sb] = dict(mask=jnp.abs(kpos - qpos) <= WINDOW,
                              k=(k2, pltpu.roll(k2.astype(F32), HEAD_DIM, 1).astype(BF16)),
                              v=(v2, pltpu.roll(v2.astype(F32), HEAD_DIM, 1).astype(BF16)))
        return blocks[sb]

    def issue_scores(sb, h):
        ops = block_operands(sb)
        rows = slice(sb * WINDOW, (sb + 1) * WINDOW)
        qp = q_ref[0, rows, (h // 2) * LANES:(h // 2 + 1) * LANES]
        qm = jnp.where(low if h % 2 == 0 else jnp.logical_not(low), qp, jnp.zeros_like(qp))
        swapped = int(h % 2 != h // SW_GROUP)
        return lax.dot_general(qm, ops["k"][swapped], (((1,), (1,)), ((), ())), preferred_element_type=F32)

    def finish(sb, h, scores):
        ops = block_operands(sb)
        s = jnp.where(ops["mask"], scores, MASK_VALUE)
        sink = sink_ref[h]
        m = jnp.maximum(jnp.max(s, axis=-1, keepdims=True), sink)
        e = jnp.exp2(s - m)
        den = jnp.sum(e, axis=-1, keepdims=True) + jnp.exp2(sink - m)
        swapped = int(h % 2 != h // SW_GROUP)
        return jnp.dot(e.astype(BF16), ops["v"][swapped], preferred_element_type=F32) * (1.0 / den)

    units = [(sb, h) for sb in range(tq // WINDOW) for h in range(SW_Q_HEADS)]
    ahead = 3
    pending, outs = {}, {}
    for n in range(len(units) + ahead):
        if n < len(units):
            pending[units[n]] = issue_scores(*units[n])
        if n >= ahead:
            sb, h = units[n - ahead]
            outs[h] = finish(sb, h, pending.pop((sb, h)))
            if h % 2 == 1:
                rows = slice(sb * WINDOW, (sb + 1) * WINDOW)
                o = jnp.where(low, outs.pop(h - 1), outs.pop(h))
                o_ref[0, rows, (h // 2) * LANES:(h // 2 + 1) * LANES] = o.astype(BF16)


def _swa(sink, qbx, kvb, tq):
    b, s, _ = qbx.shape
    kern = functools.partial(_swa_kernel, tq=tq, seq=s)
    return pl.pallas_call(
        kern,
        grid_spec=pltpu.PrefetchScalarGridSpec(
            num_scalar_prefetch=1,
            grid=(b, s // tq),
            in_specs=[
                pl.BlockSpec((1, tq, SW_W), lambda bi, i, sk: (bi, i, 0)),
                pl.BlockSpec((1, s, KVB_W), lambda bi, i, sk: (bi, 0, 0)),
            ],
            out_specs=pl.BlockSpec((1, tq, SW_W), lambda bi, i, sk: (bi, i, 0)),
        ),
        out_shape=jax.ShapeDtypeStruct((b, s, SW_W), BF16),
        compiler_params=_cparams(("parallel", "parallel")),
        name="swa",
    )(sink, qbx, kvb)


def _outproj_kernel(x_ref, oa_ref, obx_ref, woa_ref, wobx_ref, g_ref, wrt_ref, x1_ref, h2r_ref, aff_ref):
    x1 = (x_ref[...] + jnp.dot(oa_ref[...], woa_ref[...], preferred_element_type=F32)
          + jnp.dot(obx_ref[...], wobx_ref[...], preferred_element_type=F32))
    x1_ref[...] = x1
    ms = jnp.mean(x1 * x1, axis=-1, keepdims=True)
    h2 = x1 * lax.rsqrt(ms + NORM_EPS) * g_ref[...]
    tm = h2.shape[0]
    for j in range(SUBLANES):
        h2r_ref[pl.ds(j, tm, stride=SUBLANES), :] = h2[:, j * LANES:(j + 1) * LANES]
    h_hi = h2.astype(BF16)
    h_lo = (h2 - h_hi.astype(F32)).astype(BF16)
    w = wrt_ref[...]
    w_hi = w.astype(BF16)
    w_lo = (w - w_hi.astype(F32)).astype(BF16)
    logits = lax.dot_general(jnp.concatenate([w_hi, w_hi, w_lo], axis=1),
                             jnp.concatenate([h_hi, h_lo, h_hi], axis=1),
                             (((1,), (1,)), ((), ())), preferred_element_type=F32)
    mx = jnp.max(logits, axis=0, keepdims=True)
    e = jnp.exp(logits - mx)
    aff_ref[...] = e / jnp.sum(e, axis=0, keepdims=True)


def _outproj(x, oa, obx, w_oa, w_obx, g_ffn, w_rt, tm):
    n = x.shape[0]
    tok = lambda i: (i, 0)
    full = lambda i: (0, 0)
    return pl.pallas_call(
        _outproj_kernel,
        grid=(n // tm,),
        in_specs=[
            pl.BlockSpec((tm, D_MODEL), tok),
            pl.BlockSpec((tm, DA_W), tok),
            pl.BlockSpec((tm, SW_W), tok),
            pl.BlockSpec((DA_W, D_MODEL), full),
            pl.BlockSpec((SW_W, D_MODEL), full),
            pl.BlockSpec((1, D_MODEL), full),
            pl.BlockSpec((N_EXPERTS, D_MODEL), full),
        ],
        out_specs=[
            pl.BlockSpec((tm, D_MODEL), tok),
            pl.BlockSpec((tm * SUBLANES, LANES), tok),
            pl.BlockSpec((N_EXPERTS, tm), lambda i: (0, i)),
        ],
        out_shape=[
            jax.ShapeDtypeStruct((n, D_MODEL), F32),
            jax.ShapeDtypeStruct((n * SUBLANES, LANES), F32),
            jax.ShapeDtypeStruct((N_EXPERTS, n), F32),
        ],
        compiler_params=_cparams(("parallel",)),
        name="outproj",
    )(x, oa, obx, w_oa, w_obx, g_ffn, w_rt)


def _route_kernel(aff_ref, loc_ref, vals_ref, cex_ref, thr_ref, *, cap, nchunk, pblk, tchunk):
    n_exp = aff_ref.shape[0]

    def expert_bits(e):
        return pltpu.bitcast(aff_ref[e], jnp.int32)

    def count(mask):
        per_lane = jnp.sum(jnp.where(mask, 1.0, 0.0), axis=0, keepdims=True)
        return jnp.sum(per_lane, axis=1, keepdims=True)

    def search(i, thrs):
        bit = jnp.left_shift(jnp.int32(1), 30 - i)
        out = []
        for e in range(n_exp):
            cand = thrs[e] | bit
            out.append(jnp.where(count(expert_bits(e) >= cand) >= cap, cand, thrs[e]))
        return tuple(out)

    thrs = lax.fori_loop(0, 31, search, tuple(jnp.zeros((1, 1), jnp.int32) for _ in range(n_exp)))
    for e in range(n_exp):
        thr_ref[e] = jnp.broadcast_to(thrs[e], (1, LANES))
    lax.fori_loop(0, n_exp, functools.partial(_route_expert, aff_ref, loc_ref, vals_ref, cex_ref, thr_ref,
                                              count, cap, nchunk, pblk, tchunk), 0)


def _route_expert(aff_ref, loc_ref, vals_ref, cex_ref, thr_ref, count, cap, nchunk, pblk, tchunk, e, carry):
    a = aff_ref[e]
    bits = pltpu.bitcast(a, jnp.int32)
    thr = thr_ref[e]
    gt = bits > thr
    eq = bits == thr
    need = cap - count(gt)

    r128 = lax.broadcasted_iota(jnp.int32, (LANES, LANES), 0)
    c128 = lax.broadcasted_iota(jnp.int32, (LANES, LANES), 1)
    u_incl = jnp.where(r128 <= c128, 1.0, 0.0).astype(BF16)
    ones = jnp.ones((LANES, LANES), BF16)
    rc = lax.broadcasted_iota(jnp.int32, (nchunk, nchunk), 0)
    cc = lax.broadcasted_iota(jnp.int32, (nchunk, nchunk), 1)
    l_strict = jnp.where(cc < rc, 1.0, 0.0).astype(BF16)
    u_strict = jnp.where(rc < cc, 1.0, 0.0).astype(BF16)

    def chunk_scan(mask_bf):
        incl = jnp.dot(mask_bf, u_incl, preferred_element_type=F32)
        tot_b = jnp.dot(mask_bf, ones, preferred_element_type=F32)
        cexcl_b = jnp.dot(l_strict, tot_b.astype(BF16), preferred_element_type=F32)
        return incl, cexcl_b

    eq_bf = jnp.where(eq, 1.0, 0.0).astype(BF16)
    incl_eq, cexcl_eq = chunk_scan(eq_bf)
    rank_eq = cexcl_eq + incl_eq - eq_bf.astype(F32)
    sel = gt | (eq & (rank_eq < need))
    sel_bf = jnp.where(sel, 1.0, 0.0).astype(BF16)
    lc, cexcl_b = chunk_scan(sel_bf)

    tot_row = lax.dot_general(jnp.ones((8, LANES), BF16), sel_bf, (((1,), (1,)), ((), ())),
                              preferred_element_type=F32)
    cexcl_row = jnp.dot(tot_row.astype(BF16), u_strict, preferred_element_type=F32)[0:1]
    cincl_row = cexcl_row + tot_row[0:1]

    a_hi = a.astype(BF16)
    r1 = a - a_hi.astype(F32)
    a_mid = r1.astype(BF16)
    a_lo = (r1 - a_mid.astype(F32)).astype(BF16)
    kidx = lax.broadcasted_iota(jnp.int32, (nchunk, LANES), 0).astype(F32)
    cex_hi = jnp.floor(cexcl_b * (1.0 / LANES))
    cex_lo = cexcl_b - cex_hi * LANES
    table = jnp.concatenate([lc.astype(BF16), a_hi, a_mid, a_lo, kidx.astype(BF16),
                             cex_hi.astype(BF16), cex_lo.astype(BF16)], axis=1)

    lane_f = lax.broadcasted_iota(jnp.int32, (pblk, LANES), 1).astype(F32)
    eye = r128 == c128

    def slot_block(bi, _):
        base = bi * pblk
        pc = (base + lax.broadcasted_iota(jnp.int32, (pblk, nchunk), 0)).astype(F32)
        onehot_k = jnp.where((cexcl_row <= pc) & (pc < cincl_row), 1.0, 0.0).astype(BF16)
        r = jnp.dot(onehot_k, table, preferred_element_type=F32)
        r_lc = r[:, 0:LANES]
        r_aff = (r[:, LANES:2 * LANES] + r[:, 2 * LANES:3 * LANES]) + r[:, 3 * LANES:4 * LANES]
        r_k = r[:, 4 * LANES:5 * LANES]
        r_cex = r[:, 5 * LANES:6 * LANES] * LANES + r[:, 6 * LANES:7 * LANES]
        p_loc = (base + lax.broadcasted_iota(jnp.int32, (pblk, LANES), 0)).astype(F32) - r_cex
        below = jnp.where(r_lc <= p_loc, 1.0, 0.0).astype(BF16)
        t_loc = jnp.dot(below, ones, preferred_element_type=F32)
        val = jnp.sum(jnp.where(lane_f == t_loc, r_aff, 0.0), axis=-1, keepdims=True)
        tok = r_k * LANES + t_loc
        for sb in range(pblk // LANES):
            rows = slice(sb * LANES, (sb + 1) * LANES)
            out = pl.ds(pl.multiple_of(base + sb * LANES, LANES), LANES)
            tok_row = jnp.sum(jnp.where(eye, tok[rows], 0.0), axis=0, keepdims=True).astype(jnp.int32)
            loc_ref[e, :, out] = (tok_row & (tchunk - 1)) * SUBLANES
            vals_ref[e, :, out] = jnp.sum(jnp.where(eye, val[rows], 0.0), axis=0, keepdims=True)
        return 0

    lax.fori_loop(0, cap // pblk, slot_block, 0)
    loc_ref[e, :, pl.ds(cap, SLOT_PAD)] = jnp.zeros((1, SLOT_PAD), jnp.int32)
    cex_ref[e] = cexcl_row.astype(jnp.int32)
    return carry


def _route(aff3, cap, tchunk):
    e, nchunk, _ = aff3.shape
    pblk = min(512, cap)
    kern = functools.partial(_route_kernel, cap=cap, nchunk=nchunk, pblk=pblk, tchunk=tchunk)
    return pl.pallas_call(
        kern,
        out_shape=[
            jax.ShapeDtypeStruct((e, 1, cap + SLOT_PAD), jnp.int32),
            jax.ShapeDtypeStruct((e, 1, cap), F32),
            jax.ShapeDtypeStruct((e, 1, nchunk), jnp.int32),
        ],
        scratch_shapes=[pltpu.VMEM((e, 1, LANES), jnp.int32)],
        compiler_params=pltpu.CompilerParams(vmem_limit_bytes=VMEM_LIMIT),
        name="route",
    )(aff3)


def _expert_kernel(ps_ref, loc_ref, val_ref, h2r_ref, wg_ref, wu_ref, wd_ref, moe_ref, xt_ref, yt_ref,
                   *, rblk, stride):
    c = pl.program_id(0)
    e = pl.program_id(1)

    @pl.when(e == 0)
    def _():
        moe_ref[...] = jnp.zeros(moe_ref.shape, F32)

    p0 = ps_ref[e, c]
    p1 = ps_ref[e, c + 1]
    unroll = SUBLANES

    def tile_row(p):
        return pl.multiple_of(loc_ref[0, 0, p], SUBLANES)

    nblk = (p1 - p0 + rblk - 1) // rblk

    def gather_rows(base, r0, count):
        for i in range(count):
            src = tile_row(base + r0 + i)
            xt_ref[pl.ds(r0 + i, SUBLANES, stride=stride), :] = h2r_ref[pl.ds(src, SUBLANES), :]

    @pl.when(nblk > 0)
    def _():
        def first_gather(g, _):
            gather_rows(p0, g * 2 * unroll, 2 * unroll)
            return 0

        lax.fori_loop(0, rblk // (2 * unroll), first_gather, 0)

    def block(b, _):
        base = p0 + b * rblk
        x = jnp.concatenate([xt_ref[pl.ds(j * stride, rblk), :].astype(BF16) for j in range(SUBLANES)], axis=1)
        g = jnp.dot(x, wg_ref[0], preferred_element_type=F32)
        u = jnp.dot(x, wu_ref[0], preferred_element_type=F32)
        hmid = (g * jax.nn.sigmoid(g) * u).astype(BF16)
        y = jnp.dot(hmid, wd_ref[0], preferred_element_type=F32)
        for j in range(SUBLANES):
            yt_ref[pl.ds(j * stride, rblk), :] = y[:, j * LANES:(j + 1) * LANES]

        nvalid = jnp.minimum(rblk, p1 - base)

        def scatter_rows(r0, count):
            new = []
            for i in range(count):
                dst = tile_row(base + r0 + i)
                contrib = yt_ref[pl.ds(r0 + i, SUBLANES, stride=stride), :] * val_ref[0, 0, base + r0 + i]
                new.append((dst, moe_ref[pl.ds(dst, SUBLANES), :] + contrib))
            for dst, v in new:
                moe_ref[pl.ds(dst, SUBLANES), :] = v

        def scatter_pair(g, _):
            scatter_rows(g * 2 * unroll, unroll)
            scatter_rows(g * 2 * unroll + unroll, unroll)
            return 0

        has_next = b + 1 < nblk

        @pl.when(has_next)
        def _():
            def scatter_and_gather(g, _):
                scatter_pair(g, 0)
                gather_rows(base + rblk, g * 2 * unroll, 2 * unroll)
                return 0

            lax.fori_loop(0, rblk // (2 * unroll), scatter_and_gather, 0)

        @pl.when(jnp.logical_not(has_next))
        def _():
            npairs = nvalid // (2 * unroll)
            lax.fori_loop(0, npairs, scatter_pair, 0)

            def scatter_one(r, _):
                scatter_rows(r, 1)
                return 0

            lax.fori_loop(npairs * 2 * unroll, nvalid, scatter_one, 0)

        return 0

    lax.fori_loop(0, nblk, block, 0)


def _experts(pstart, loc, vals, h2r, wg, wu, wd, tchunk, rblk):
    e, _, cap_pad = loc.shape
    n8 = h2r.shape[0]
    stride = rblk + SUBLANES
    kern = functools.partial(_expert_kernel, rblk=rblk, stride=stride)
    wspec = pl.BlockSpec((1, D_MODEL, D_EXPERT), lambda ci, ei, ps: (ei, 0, 0))
    slot = lambda a: pl.BlockSpec((1, 1, a.shape[2]), lambda ci, ei, ps: (ei, 0, 0), memory_space=pltpu.SMEM)
    chunk = lambda: pl.BlockSpec((tchunk * SUBLANES, LANES), lambda ci, ei, ps: (ci, 0),
                                 pipeline_mode=pl.Buffered(1))
    return pl.pallas_call(
        kern,
        grid_spec=pltpu.PrefetchScalarGridSpec(
            num_scalar_prefetch=1,
            grid=(n8 // (tchunk * SUBLANES), e),
            in_specs=[slot(loc), slot(vals), chunk(), wspec, wspec,
                      pl.BlockSpec((1, D_EXPERT, D_MODEL), lambda ci, ei, ps: (ei, 0, 0))],
            out_specs=chunk(),
            scratch_shapes=[
                pltpu.VMEM((SUBLANES * stride, LANES), F32),
                pltpu.VMEM((SUBLANES * stride, LANES), F32),
            ],
        ),
        out_shape=jax.ShapeDtypeStruct((n8, LANES), F32),
        compiler_params=pltpu.CompilerParams(dimension_semantics=("arbitrary", "arbitrary"),
                                             vmem_limit_bytes=EXPERT_VMEM_LIMIT),
        name="experts",
    )(pstart, loc, vals, h2r, wg, wu, wd)


def _ple_kernel(x_ref, moe_ref, p_ref, g_ref, wg_ref, wp_ref, o_ref):
    tm = x_ref.shape[0]
    moe = jnp.concatenate([moe_ref[pl.ds(j, tm, stride=SUBLANES), :] for j in range(SUBLANES)], axis=1)
    x = x_ref[...] + moe
    ms = jnp.mean(x * x, axis=-1, keepdims=True)
    hn = (x * lax.rsqrt(ms + NORM_EPS) * g_ref[...]).astype(BF16)
    gate = jax.nn.sigmoid(jnp.dot(hn, wg_ref[...], preferred_element_type=F32))
    emb = jnp.dot(p_ref[0].astype(BF16), wp_ref[...], preferred_element_type=F32)
    o_ref[...] = x + gate * emb


def _ple(x, moe_r, p_all, layer, g_ple, w_g, w_p, tm):
    n = x.shape[0]
    tok = lambda i: (i, 0)
    full = lambda i: (0, 0)
    return pl.pallas_call(
        _ple_kernel,
        grid=(n // tm,),
        in_specs=[
            pl.BlockSpec((tm, D_MODEL), tok),
            pl.BlockSpec((tm * SUBLANES, LANES), tok),
            pl.BlockSpec((1, tm, PLE_DIM), lambda i: (layer, i, 0)),
            pl.BlockSpec((1, D_MODEL), full),
            pl.BlockSpec((D_MODEL, D_MODEL), full),
            pl.BlockSpec((PLE_DIM, D_MODEL), full),
        ],
        out_specs=pl.BlockSpec((tm, D_MODEL), tok),
        out_shape=jax.ShapeDtypeStruct((n, D_MODEL), F32),
        compiler_params=_cparams(("parallel",)),
        name="ple",
    )(x, moe_r, p_all, g_ple, w_g, w_p)


def _rope_tables(seq):
    pos = jnp.arange(seq, dtype=F32)
    inv = ROPE_THETA ** (-jnp.arange(0, HEAD_DIM, 2, dtype=F32) / HEAD_DIM)
    ang = pos[:, None] * inv[None, :]
    ang = jnp.concatenate([ang, ang, ang, ang], axis=-1)
    cos, sin = jnp.cos(ang), jnp.sin(ang)
    first_half = (jnp.arange(LANES) % HEAD_DIM) < HEAD_DIM // 2
    sin_a = jnp.where(first_half[None, :], -sin, 0.0)
    sin_b = jnp.where(first_half[None, :], 0.0, sin)
    return cos, sin_a, sin_b


def _layer_params(i, attn_norm, w_in, q_norm_a, k_norm_a, subln_a, q_norm_b, k_norm_b, w_out):
    wi = w_in[i]
    w = jnp.concatenate([wi[:, 0:1024], wi[:, 1536:2304]], axis=1).astype(BF16)
    wvt = wi[:, 1024:1536].T.astype(BF16)
    scale = HEAD_DIM ** -0.5
    log2e = math.log2(math.e)
    t2 = lambda g, n: jnp.tile(g, n)
    gain = jnp.concatenate([
        t2(q_norm_a[i], 8) * (scale * log2e), t2(k_norm_a[i], 8), t2(q_norm_b[i], 8) * (scale * log2e),
        t2(k_norm_b[i], 2), jnp.ones((NX - COL_KVB - LANES,), F32)])[None, :]
    wo = w_out[i].astype(BF16)
    lam_init = 0.8 - 0.6 * math.exp(-0.3 * i)
    return dict(g_attn=attn_norm[i][None, :], w=w, wvt=wvt, gain=gain, w_oa=wo[0:DA_W], w_ob=wo[DA_W:],
                g_sub=subln_a[i][None, :], lam_init=lam_init)


def _pick(n, pref):
    t = pref
    while n % t:
        t //= 2
    return t


def _encoder(x, p, attn_norm, w_in, q_norm_a, k_norm_a, lambda_q1, lambda_k1, lambda_q2, lambda_k2,
             subln_a, q_norm_b, k_norm_b, sink_b, w_out, ffn_norm, w_router, w_gate_e, w_up_e, w_down_e,
             ple_norm, w_ple_gate, w_ple_proj):
    b, s, _ = x.shape
    n = b * s
    cap = EC_CAPACITY_FACTOR * n // N_EXPERTS
    tm = _pick(s, 512)
    tq = _pick(s, 512)
    tk = _pick(s // 2, 512)
    tq_sw = _pick(s, 512)
    tchunk = _pick(n, 4096)
    rblk = EXPERT_ROWS
    cos, sin_a, sin_b = _rope_tables(s)
    r = lax.broadcasted_iota(jnp.int32, (256, 256), 0) // HEAD_DIM
    c = lax.broadcasted_iota(jnp.int32, (256, 256), 1) // HEAD_DIM
    bd = (r == c).astype(BF16)
    xf = x.reshape(n, D_MODEL)
    p_all = p.reshape(DEPTH, n, PLE_DIM)
    for i in range(DEPTH):
        lp = _layer_params(i, attn_norm, w_in, q_norm_a, k_norm_a, subln_a, q_norm_b, k_norm_b, w_out)
        qa, ka, qb, kvb, vat = _inproj(xf, lp["g_attn"], lp["w"], lp["wvt"], lp["gain"], cos, sin_a, sin_b, bd,
                                        s, tm)
        lam4 = jnp.stack([lambda_q1[i], lambda_k1[i], lambda_q2[i], lambda_k2[i]]).astype(F32)
        oa = _diffattn(qa.reshape(b, s, DA_W), ka.reshape(b, s, DA_W), vat, lam4,
                       lp["g_sub"], lp["lam_init"], tq, tk)
        ob = _swa(sink_b[i].astype(F32) * math.log2(math.e), qb.reshape(b, s, SW_W),
                   kvb.reshape(b, s, KVB_W), tq_sw)
        x1, h2r, aff_t = _outproj(xf, oa.reshape(n, DA_W), ob.reshape(n, SW_W), lp["w_oa"], lp["w_ob"],
                                  ffn_norm[i][None, :], w_router[i].T, tm)
        loc, vals, cex = _route(aff_t.reshape(N_EXPERTS, n // LANES, LANES), cap, tchunk)
        pstart = jnp.concatenate([cex[:, 0, ::tchunk // LANES], jnp.full((N_EXPERTS, 1), cap, jnp.int32)], axis=1)
        moe_r = _experts(pstart, loc, vals, h2r, w_gate_e[i].astype(BF16), w_up_e[i].astype(BF16),
                         w_down_e[i].astype(BF16), tchunk, rblk)
        xf = _ple(x1, moe_r, p_all, i, ple_norm[i][None, :], w_ple_gate[i].astype(BF16),
                  w_ple_proj[i].astype(BF16), tm)
    return xf.reshape(b, s, D_MODEL)


def kernel(x_prompt, x_sample, p_prompt, p_sample, attn_norm, w_in, q_norm_a, k_norm_a, lambda_q1, lambda_k1,
           lambda_q2, lambda_k2, subln_a, q_norm_b, k_norm_b, sink_b, w_out, ffn_norm, w_router, w_gate_e,
           w_up_e, w_down_e, ple_norm, w_ple_gate, w_ple_proj):
    ws = (attn_norm, w_in, q_norm_a, k_norm_a, lambda_q1, lambda_k1, lambda_q2, lambda_k2, subln_a, q_norm_b,
          k_norm_b, sink_b, w_out, ffn_norm, w_router, w_gate_e, w_up_e, w_down_e, ple_norm, w_ple_gate,
          w_ple_proj)
    return (_encoder(x_prompt, p_prompt, *ws), _encoder(x_sample, p_sample, *ws))
```

```python
import functools
import math

import jax
import jax.numpy as jnp
from jax import lax
from jax.experimental import pallas as pl
from jax.experimental.pallas import tpu as pltpu

F32 = jnp.float32
BF16 = jnp.bfloat16

D_MODEL = 1024
DEPTH = 4
HEAD_DIM = 64
DA_HEADS = 4
SW_Q_HEADS = 8
SW_KV_HEADS = 2
SW_GROUP = SW_Q_HEADS // SW_KV_HEADS
WINDOW = 128
N_EXPERTS = 16
EC_CAPACITY_FACTOR = 2
D_EXPERT = 1024
PLE_DIM = 256
ROPE_THETA = 10000.0
NORM_EPS = 1e-6
MASK_VALUE = -1e30

LANES = 128
SUBLANES = 8
DA_W = DA_HEADS * 2 * HEAD_DIM
SW_W = SW_Q_HEADS * HEAD_DIM
KVB_W = 2 * LANES
COL_QA, COL_KA, COL_QB, COL_KVB = 0, 512, 1024, 1536
NX = 1792
N_ROPE_TILES = (COL_KVB + LANES) // LANES
VMEM_LIMIT = 48 * 1024 * 1024
EXPERT_VMEM_LIMIT = 56 * 1024 * 1024
EXPERT_ROWS = 288
SLOT_PAD = 512


def _cparams(sem):
    return pltpu.CompilerParams(dimension_semantics=sem, vmem_limit_bytes=VMEM_LIMIT)


def _inproj_kernel(x_ref, gat_ref, w_ref, wvt_ref, gain_ref, cos_ref, sa_ref, sb_ref, bd_ref,
                   qa_ref, ka_ref, qb_ref, kvb_ref, vat_ref, xn_ref):
    x = x_ref[...]
    ms = jnp.mean(x * x, axis=-1, keepdims=True)
    xn_ref[...] = (x * lax.rsqrt(ms + NORM_EPS) * gat_ref[...]).astype(BF16)
    cos = cos_ref[...]
    sa = sa_ref[...]
    sb = sb_ref[...]
    outs = ((qa_ref, COL_QA), (ka_ref, COL_KA), (qb_ref, COL_QB), (kvb_ref, COL_KVB))
    vat_ref[0] = lax.dot_general(wvt_ref[...], xn_ref[...], (((1,), (1,)), ((), ())),
                                 preferred_element_type=F32).astype(BF16)

    def out_for(col):
        for ref, base in reversed(outs):
            if col >= base:
                return ref, col - base
        raise AssertionError

    def project(c):
        return jnp.dot(xn_ref[...], w_ref[:, c * 256:(c + 1) * 256], preferred_element_type=F32)

    nchunks = NX // 256
    p_next = project(0)
    for c in range(nchunks):
        c0 = c * 256
        p = p_next
        if c + 1 < nchunks:
            p_next = project(c + 1)
        normed = [(c0 + t * LANES) // LANES < N_ROPE_TILES for t in range(2)]
        if any(normed):
            ss = jnp.dot((p * p).astype(BF16), bd_ref[...], preferred_element_type=F32)
        for t in range(2):
            col = c0 + t * LANES
            y = p[:, t * LANES:(t + 1) * LANES]
            if normed[t]:
                sst = ss[:, t * LANES:(t + 1) * LANES]
                y = y * lax.rsqrt(sst * (1.0 / HEAD_DIM) + NORM_EPS) * gain_ref[:, col:col + LANES]
                y = y * cos + pltpu.roll(y, 96, 1) * sa + pltpu.roll(y, 32, 1) * sb
            ref, off = out_for(col)
            ref[:, off:off + LANES] = y.astype(BF16)


def _inproj(x, g_attn, w, wvt, gain, cos, sa, sb, bd, seq, tm):
    n = x.shape[0]
    nblk_s = seq // tm
    tok = lambda i: (i, 0)
    full = lambda i: (0, 0)
    rope = lambda i: (i % nblk_s, 0)
    return pl.pallas_call(
        _inproj_kernel,
        grid=(n // tm,),
        in_specs=[
            pl.BlockSpec((tm, D_MODEL), tok),
            pl.BlockSpec((1, D_MODEL), full),
            pl.BlockSpec((D_MODEL, NX), full),
            pl.BlockSpec((DA_W, D_MODEL), full),
            pl.BlockSpec((1, NX), full),
            pl.BlockSpec((tm, LANES), rope),
            pl.BlockSpec((tm, LANES), rope),
            pl.BlockSpec((tm, LANES), rope),
            pl.BlockSpec((256, 256), full),
        ],
        out_specs=[
            pl.BlockSpec((tm, DA_W), tok),
            pl.BlockSpec((tm, DA_W), tok),
            pl.BlockSpec((tm, SW_W), tok),
            pl.BlockSpec((tm, KVB_W), tok),
            pl.BlockSpec((1, DA_W, tm), lambda i: (i // nblk_s, 0, i % nblk_s)),
        ],
        out_shape=[
            jax.ShapeDtypeStruct((n, DA_W), BF16),
            jax.ShapeDtypeStruct((n, DA_W), BF16),
            jax.ShapeDtypeStruct((n, SW_W), BF16),
            jax.ShapeDtypeStruct((n, KVB_W), BF16),
            jax.ShapeDtypeStruct((n // seq, DA_W, seq), BF16),
        ],
        scratch_shapes=[pltpu.VMEM((tm, D_MODEL), BF16)],
        compiler_params=_cparams(("parallel",)),
        name="inproj",
    )(x, g_attn, w, wvt, gain, cos, sa, sb, bd)


def _diffattn_kernel(q_ref, k_ref, vt_ref, lam4_ref, g_ref, o_ref, m_ref, l_ref, acc_ref, qst_ref, s_ref,
                     *, tq, tk, nk, nq, unroll, lam_init):
    def load_queries(i):
        q0 = pl.multiple_of(i * tq, tq)
        qt = q_ref[0, pl.ds(q0, tq), :].astype(F32).T
        feat = lax.broadcasted_iota(jnp.int32, qt.shape, 0)
        zero = jnp.zeros_like(qt)
        qst_ref[...] = jnp.concatenate([jnp.where(feat < HEAD_DIM, qt, zero), jnp.where(feat >= HEAD_DIM, qt, zero)],
                                       axis=1).astype(BF16)

    def reset_state():
        m_ref[...] = jnp.full(m_ref.shape, -jnp.inf, F32)
        l_ref[...] = jnp.zeros(l_ref.shape, F32)
        acc_ref[...] = jnp.zeros(acc_ref.shape, F32)

    def scores(j):
        k0 = pl.multiple_of(j * tk, tk)
        return jnp.dot(k_ref[0, pl.ds(k0, tk), :], qst_ref[...], preferred_element_type=F32)

    def accumulate(s, j):
        k0 = pl.multiple_of(j * tk, tk)
        vtj = vt_ref[0, :, pl.ds(k0, tk)]
        m_old = m_ref[...]
        m_new = jnp.maximum(m_old, jnp.max(s, axis=0, keepdims=True))
        alpha = jnp.exp2(m_old - m_new)
        p = jnp.exp2(s - m_new)
        l_ref[...] = alpha * l_ref[...] + jnp.sum(p, axis=0, keepdims=True)
        acc_ref[...] = alpha * acc_ref[...] + jnp.dot(vtj, p.astype(BF16), preferred_element_type=F32)
        m_ref[...] = m_new

    def finalize(i):
        ot = acc_ref[...] / l_ref[...]
        lam4 = lam4_ref[...]
        lam = (jnp.exp(jnp.sum(lam4[0:1] * lam4[1:2], axis=-1, keepdims=True))
               - jnp.exp(jnp.sum(lam4[2:3] * lam4[3:4], axis=-1, keepdims=True)) + lam_init)
        o = (ot[:, :tq] - lam * ot[:, tq:]).T
        ms = jnp.mean(o * o, axis=-1, keepdims=True)
        o = o * lax.rsqrt(ms + NORM_EPS) * g_ref[...] * (1.0 - lam_init)
        o_ref[0, pl.ds(pl.multiple_of(i * tq, tq), tq), :] = o.astype(BF16)

    def trip(jj, _):
        j0 = unroll * jj
        for u in range(unroll):
            s_ref[(u + 1) % 2] = scores(j0 + u + 1)
            accumulate(s_ref[u % 2], j0 + u)
        return 0

    def query_tile(i, _):
        lax.fori_loop(0, nk // unroll - 1, trip, 0)
        j0 = nk - unroll
        for u in range(unroll - 1):
            s_ref[(u + 1) % 2] = scores(j0 + u + 1)
            accumulate(s_ref[u % 2], j0 + u)
        load_queries(jnp.minimum(i + 1, nq - 1))
        s_ref[0] = scores(0)
        accumulate(s_ref[1], nk - 1)
        finalize(i)
        reset_state()
        return 0

    load_queries(0)
    reset_state()
    s_ref[0] = scores(0)
    lax.fori_loop(0, nq, query_tile, 0)


def _diffattn(qa, ka, vat, lam4, g_sub, lam_init, tq, tk):
    b, s, _ = qa.shape
    nk = s // tk
    unroll = 8 if nk % 8 == 0 else (4 if nk % 4 == 0 else 2)
    kern = functools.partial(_diffattn_kernel, tq=tq, tk=tk, nk=nk, nq=s // tq, unroll=unroll, lam_init=lam_init)
    per_head = lambda bi, h: (bi, 0, h)
    return pl.pallas_call(
        kern,
        grid=(b, DA_HEADS),
        in_specs=[
            pl.BlockSpec((1, s, LANES), per_head),
            pl.BlockSpec((1, s, LANES), per_head),
            pl.BlockSpec((1, LANES, s), lambda bi, h: (bi, h, 0)),
            pl.BlockSpec((4, HEAD_DIM), lambda bi, h: (0, 0)),
            pl.BlockSpec((1, LANES), lambda bi, h: (0, 0)),
        ],
        out_specs=pl.BlockSpec((1, s, LANES), per_head),
        out_shape=jax.ShapeDtypeStruct((b, s, DA_W), BF16),
        scratch_shapes=[
            pltpu.VMEM((1, 2 * tq), F32),
            pltpu.VMEM((1, 2 * tq), F32),
            pltpu.VMEM((LANES, 2 * tq), F32),
            pltpu.VMEM((LANES, 2 * tq), BF16),
            pltpu.VMEM((2, tk, 2 * tq), F32),
        ],
        compiler_params=_cparams(("parallel", "parallel")),
        name="diffattn",
    )(qa, ka, vat, lam4, g_sub)


def _swa_kernel(sink_ref, q_ref, kv_ref, o_ref, *, tq, seq):
    i = pl.program_id(1)
    kw = 3 * WINDOW
    low = lax.broadcasted_iota(jnp.int32, (WINDOW, LANES), 1) < HEAD_DIM
    blocks = {}

    def block_operands(sb):
        if sb not in blocks:
            q0 = i * tq + sb * WINDOW
            g0 = pl.multiple_of(jnp.clip(q0 - WINDOW, 0, seq - kw), WINDOW)
            kv = kv_ref[0, pl.ds(g0, kw), :]
            k2 = kv[:, 0:LANES]
            v2 = kv[:, LANES:2 * LANES]
            qpos = q0 + lax.broadcasted_iota(jnp.int32, (WINDOW, kw), 0)
            kpos = g0 + lax.broadcasted_iota(jnp.int32, (WINDOW, kw), 1)
            blocks[sb] = dict(mask=jnp.abs(kpos - qpos) <= WINDOW,
                              k=(k2, pltpu.roll(k2.astype(F32), HEAD_DIM, 1).astype(BF16)),
                              v=(v2, pltpu.roll(v2.astype(F32), HEAD_DIM, 1).astype(BF16)))
        return blocks[sb]

    def issue_scores(sb, h):
        ops = block_operands(sb)
        rows = slice(sb * WINDOW, (sb + 1) * WINDOW)
        qp = q_ref[0, rows, (h // 2) * LANES:(h // 2 + 1) * LANES]
        qm = jnp.where(low if h % 2 == 0 else jnp.logical_not(low), qp, jnp.zeros_like(qp))
        swapped = int(h % 2 != h // SW_GROUP)
        return lax.dot_general(qm, ops["k"][swapped], (((1,), (1,)), ((), ())), preferred_element_type=F32)

    def finish(sb, h, scores):
        ops = block_operands(sb)
        s = jnp.where(ops["mask"], scores, MASK_VALUE)
        sink = sink_ref[h]
        m = jnp.maximum(jnp.max(s, axis=-1, keepdims=True), sink)
        e = jnp.exp2(s - m)
        den = jnp.sum(e, axis=-1, keepdims=True) + jnp.exp2(sink - m)
        swapped = int(h % 2 != h // SW_GROUP)
        return jnp.dot(e.astype(BF16), ops["v"][swapped], preferred_element_type=F32) * (1.0 / den)

    units = [(sb, h) for sb in range(tq // WINDOW) for h in range(SW_Q_HEADS)]
    ahead = 3
    pending, outs = {}, {}
    for n in range(len(units) + ahead):
        if n < len(units):
            pending[units[n]] = issue_scores(*units[n])
        if n >= ahead:
            sb, h = units[n - ahead]
            outs[h] = finish(sb, h, pending.pop((sb, h)))
            if h % 2 == 1:
                rows = slice(sb * WINDOW, (sb + 1) * WINDOW)
                o = jnp.where(low, outs.pop(h - 1), outs.pop(h))
                o_ref[0, rows, (h // 2) * LANES:(h // 2 + 1) * LANES] = o.astype(BF16)


def _swa(sink, qbx, kvb, tq):
    b, s, _ = qbx.shape
    kern = functools.partial(_swa_kernel, tq=tq, seq=s)
    return pl.pallas_call(
        kern,
        grid_spec=pltpu.PrefetchScalarGridSpec(
            num_scalar_prefetch=1,
            grid=(b, s // tq),
            in_specs=[
                pl.BlockSpec((1, tq, SW_W), lambda bi, i, sk: (bi, i, 0)),
                pl.BlockSpec((1, s, KVB_W), lambda bi, i, sk: (bi, 0, 0)),
            ],
            out_specs=pl.BlockSpec((1, tq, SW_W), lambda bi, i, sk: (bi, i, 0)),
        ),
        out_shape=jax.ShapeDtypeStruct((b, s, SW_W), BF16),
        compiler_params=_cparams(("parallel", "parallel")),
        name="swa",
    )(sink, qbx, kvb)


def _outproj_kernel(x_ref, oa_ref, obx_ref, woa_ref, wobx_ref, g_ref, wrt_ref, x1_ref, h2r_ref, aff_ref):
    x1 = (x_ref[...] + jnp.dot(oa_ref[...], woa_ref[...], preferred_element_type=F32)
          + jnp.dot(obx_ref[...], wobx_ref[...], preferred_element_type=F32))
    x1_ref[...] = x1
    ms = jnp.mean(x1 * x1, axis=-1, keepdims=True)
    h2 = x1 * lax.rsqrt(ms + NORM_EPS) * g_ref[...]
    tm = h2.shape[0]
    for j in range(SUBLANES):
        h2r_ref[pl.ds(j, tm, stride=SUBLANES), :] = h2[:, j * LANES:(j + 1) * LANES]
    h_hi = h2.astype(BF16)
    h_lo = (h2 - h_hi.astype(F32)).astype(BF16)
    w = wrt_ref[...]
    w_hi = w.astype(BF16)
    w_lo = (w - w_hi.astype(F32)).astype(BF16)
    logits = lax.dot_general(jnp.concatenate([w_hi, w_hi, w_lo], axis=1),
                             jnp.concatenate([h_hi, h_lo, h_hi], axis=1),
                             (((1,), (1,)), ((), ())), preferred_element_type=F32)
    mx = jnp.max(logits, axis=0, keepdims=True)
    e = jnp.exp(logits - mx)
    aff_ref[...] = e / jnp.sum(e, axis=0, keepdims=True)


def _outproj(x, oa, obx, w_oa, w_obx, g_ffn, w_rt, tm):
    n = x.shape[0]
    tok = lambda i: (i, 0)
    full = lambda i: (0, 0)
    return pl.pallas_call(
        _outproj_kernel,
        grid=(n // tm,),
        in_specs=[
            pl.BlockSpec((tm, D_MODEL), tok),
            pl.BlockSpec((tm, DA_W), tok),
            pl.BlockSpec((tm, SW_W), tok),
            pl.BlockSpec((DA_W, D_MODEL), full),
            pl.BlockSpec((SW_W, D_MODEL), full),
            pl.BlockSpec((1, D_MODEL), full),
            pl.BlockSpec((N_EXPERTS, D_MODEL), full),
        ],
        out_specs=[
            pl.BlockSpec((tm, D_MODEL), tok),
            pl.BlockSpec((tm * SUBLANES, LANES), tok),
            pl.BlockSpec((N_EXPERTS, tm), lambda i: (0, i)),
        ],
        out_shape=[
            jax.ShapeDtypeStruct((n, D_MODEL), F32),
            jax.ShapeDtypeStruct((n * SUBLANES, LANES), F32),
            jax.ShapeDtypeStruct((N_EXPERTS, n), F32),
        ],
        compiler_params=_cparams(("parallel",)),
        name="outproj",
    )(x, oa, obx, w_oa, w_obx, g_ffn, w_rt)


def _route_kernel(aff_ref, loc_ref, vals_ref, cex_ref, thr_ref, *, cap, nchunk, pblk, tchunk):
    n_exp = aff_ref.shape[0]

    def expert_bits(e):
        return pltpu.bitcast(aff_ref[e], jnp.int32)

    def count(mask):
        per_lane = jnp.sum(jnp.where(mask, 1.0, 0.0), axis=0, keepdims=True)
        return jnp.sum(per_lane, axis=1, keepdims=True)

    def search(i, thrs):
        bit = jnp.left_shift(jnp.int32(1), 30 - i)
        out = []
        for e in range(n_exp):
            cand = thrs[e] | bit
            out.append(jnp.where(count(expert_bits(e) >= cand) >= cap, cand, thrs[e]))
        return tuple(out)

    thrs = lax.fori_loop(0, 31, search, tuple(jnp.zeros((1, 1), jnp.int32) for _ in range(n_exp)))
    for e in range(n_exp):
        thr_ref[e] = jnp.broadcast_to(thrs[e], (1, LANES))
    lax.fori_loop(0, n_exp, functools.partial(_route_expert, aff_ref, loc_ref, vals_ref, cex_ref, thr_ref,
                                              count, cap, nchunk, pblk, tchunk), 0)


def _route_expert(aff_ref, loc_ref, vals_ref, cex_ref, thr_ref, count, cap, nchunk, pblk, tchunk, e, carry):
    a = aff_ref[e]
    bits = pltpu.bitcast(a, jnp.int32)
    thr = thr_ref[e]
    gt = bits > thr
    eq = bits == thr
    need = cap - count(gt)

    r128 = lax.broadcasted_iota(jnp.int32, (LANES, LANES), 0)
    c128 = lax.broadcasted_iota(jnp.int32, (LANES, LANES), 1)
    u_incl = jnp.where(r128 <= c128, 1.0, 0.0).astype(BF16)
    ones = jnp.ones((LANES, LANES), BF16)
    rc = lax.broadcasted_iota(jnp.int32, (nchunk, nchunk), 0)
    cc = lax.broadcasted_iota(jnp.int32, (nchunk, nchunk), 1)
    l_strict = jnp.where(cc < rc, 1.0, 0.0).astype(BF16)
    u_strict = jnp.where(rc < cc, 1.0, 0.0).astype(BF16)

    def chunk_scan(mask_bf):
        incl = jnp.dot(mask_bf, u_incl, preferred_element_type=F32)
        tot_b = jnp.dot(mask_bf, ones, preferred_element_type=F32)
        cexcl_b = jnp.dot(l_strict, tot_b.astype(BF16), preferred_element_type=F32)
        return incl, cexcl_b

    eq_bf = jnp.where(eq, 1.0, 0.0).astype(BF16)
    incl_eq, cexcl_eq = chunk_scan(eq_bf)
    rank_eq = cexcl_eq + incl_eq - eq_bf.astype(F32)
    sel = gt | (eq & (rank_eq < need))
    sel_bf = jnp.where(sel, 1.0, 0.0).astype(BF16)
    lc, cexcl_b = chunk_scan(sel_bf)

    tot_row = lax.dot_general(jnp.ones((8, LANES), BF16), sel_bf, (((1,), (1,)), ((), ())),
                              preferred_element_type=F32)
    cexcl_row = jnp.dot(tot_row.astype(BF16), u_strict, preferred_element_type=F32)[0:1]
    cincl_row = cexcl_row + tot_row[0:1]

    a_hi = a.astype(BF16)
    r1 = a - a_hi.astype(F32)
    a_mid = r1.astype(BF16)
    a_lo = (r1 - a_mid.astype(F32)).astype(BF16)
    kidx = lax.broadcasted_iota(jnp.int32, (nchunk, LANES), 0).astype(F32)
    cex_hi = jnp.floor(cexcl_b * (1.0 / LANES))
    cex_lo = cexcl_b - cex_hi * LANES
    table = jnp.concatenate([lc.astype(BF16), a_hi, a_mid, a_lo, kidx.astype(BF16),
                             cex_hi.astype(BF16), cex_lo.astype(BF16)], axis=1)

    lane_f = lax.broadcasted_iota(jnp.int32, (pblk, LANES), 1).astype(F32)
    eye = r128 == c128

    def slot_block(bi, _):
        base = bi * pblk
        pc = (base + lax.broadcasted_iota(jnp.int32, (pblk, nchunk), 0)).astype(F32)
        onehot_k = jnp.where((cexcl_row <= pc) & (pc < cincl_row), 1.0, 0.0).astype(BF16)
        r = jnp.dot(onehot_k, table, preferred_element_type=F32)
        r_lc = r[:, 0:LANES]
        r_aff = (r[:, LANES:2 * LANES] + r[:, 2 * LANES:3 * LANES]) + r[:, 3 * LANES:4 * LANES]
        r_k = r[:, 4 * LANES:5 * LANES]
        r_cex = r[:, 5 * LANES:6 * LANES] * LANES + r[:, 6 * LANES:7 * LANES]
        p_loc = (base + lax.broadcasted_iota(jnp.int32, (pblk, LANES), 0)).astype(F32) - r_cex
        below = jnp.where(r_lc <= p_loc, 1.0, 0.0).astype(BF16)
        t_loc = jnp.dot(below, ones, preferred_element_type=F32)
        val = jnp.sum(jnp.where(lane_f == t_loc, r_aff, 0.0), axis=-1, keepdims=True)
        tok = r_k * LANES + t_loc
        for sb in range(pblk // LANES):
            rows = slice(sb * LANES, (sb + 1) * LANES)
            out = pl.ds(pl.multiple_of(base + sb * LANES, LANES), LANES)
            tok_row = jnp.sum(jnp.where(eye, tok[rows], 0.0), axis=0, keepdims=True).astype(jnp.int32)
            loc_ref[e, :, out] = (tok_row & (tchunk - 1)) * SUBLANES
            vals_ref[e, :, out] = jnp.sum(jnp.where(eye, val[rows], 0.0), axis=0, keepdims=True)
        return 0

    lax.fori_loop(0, cap // pblk, slot_block, 0)
    loc_ref[e, :, pl.ds(cap, SLOT_PAD)] = jnp.zeros((1, SLOT_PAD), jnp.int32)
    cex_ref[e] = cexcl_row.astype(jnp.int32)
    return carry


def _route(aff3, cap, tchunk):
    e, nchunk, _ = aff3.shape
    pblk = min(1024, cap)
    kern = functools.partial(_route_kernel, cap=cap, nchunk=nchunk, pblk=pblk, tchunk=tchunk)
    return pl.pallas_call(
        kern,
        out_shape=[
            jax.ShapeDtypeStruct((e, 1, cap + SLOT_PAD), jnp.int32),
            jax.ShapeDtypeStruct((e, 1, cap), F32),
            jax.ShapeDtypeStruct((e, 1, nchunk), jnp.int32),
        ],
        scratch_shapes=[pltpu.VMEM((e, 1, LANES), jnp.int32)],
        compiler_params=pltpu.CompilerParams(vmem_limit_bytes=VMEM_LIMIT),
        name="route",
    )(aff3)


def _expert_kernel(ps_ref, loc_ref, val_ref, h2r_ref, wg_ref, wu_ref, wd_ref, moe_ref, xt_ref, yt_ref,
                   *, rblk, stride):
    c = pl.program_id(0)
    e = pl.program_id(1)

    @pl.when(e == 0)
    def _():
        moe_ref[...] = jnp.zeros(moe_ref.shape, F32)

    p0 = ps_ref[e, c]
    p1 = ps_ref[e, c + 1]
    unroll = SUBLANES

    def tile_row(p):
        return pl.multiple_of(loc_ref[0, 0, p], SUBLANES)

    nblk = (p1 - p0 + rblk - 1) // rblk

    def gather_rows(base, r0, count):
        for i in range(count):
            src = tile_row(base + r0 + i)
            xt_ref[pl.ds(r0 + i, SUBLANES, stride=stride), :] = h2r_ref[pl.ds(src, SUBLANES), :]

    @pl.when(nblk > 0)
    def _():
        def first_gather(g, _):
            gather_rows(p0, g * 2 * unroll, 2 * unroll)
            return 0

        lax.fori_loop(0, rblk // (2 * unroll), first_gather, 0)

    def block(b, _):
        base = p0 + b * rblk
        x = jnp.concatenate([xt_ref[pl.ds(j * stride, rblk), :].astype(BF16) for j in range(SUBLANES)], axis=1)
        g = jnp.dot(x, wg_ref[0], preferred_element_type=F32)
        u = jnp.dot(x, wu_ref[0], preferred_element_type=F32)
        hmid = (g * jax.nn.sigmoid(g) * u).astype(BF16)
        y = jnp.dot(hmid, wd_ref[0], preferred_element_type=F32)
        for j in range(SUBLANES):
            yt_ref[pl.ds(j * stride, rblk), :] = y[:, j * LANES:(j + 1) * LANES]

        nvalid = jnp.minimum(rblk, p1 - base)

        def scatter_rows(r0, count):
            new = []
            for i in range(count):
                dst = tile_row(base + r0 + i)
                contrib = yt_ref[pl.ds(r0 + i, SUBLANES, stride=stride), :] * val_ref[0, 0, base + r0 + i]
                new.append((dst, moe_ref[pl.ds(dst, SUBLANES), :] + contrib))
            for dst, v in new:
                moe_ref[pl.ds(dst, SUBLANES), :] = v

        def scatter_pair(g, _):
            scatter_rows(g * 2 * unroll, unroll)
            scatter_rows(g * 2 * unroll + unroll, unroll)
            return 0

        has_next = b + 1 < nblk

        @pl.when(has_next)
        def _():
            def scatter_and_gather(g, _):
                scatter_pair(g, 0)
                gather_rows(base + rblk, g * 2 * unroll, 2 * unroll)
                return 0

            lax.fori_loop(0, rblk // (2 * unroll), scatter_and_gather, 0)

        @pl.when(jnp.logical_not(has_next))
        def _():
            npairs = nvalid // (2 * unroll)
            lax.fori_loop(0, npairs, scatter_pair, 0)

            def scatter_one(r, _):
                scatter_rows(r, 1)
                return 0

            lax.fori_loop(npairs * 2 * unroll, nvalid, scatter_one, 0)

        return 0

    lax.fori_loop(0, nblk, block, 0)


def _experts(pstart, loc, vals, h2r, wg, wu, wd, tchunk, rblk):
    e, _, cap_pad = loc.shape
    n8 = h2r.shape[0]
    stride = rblk + SUBLANES
    kern = functools.partial(_expert_kernel, rblk=rblk, stride=stride)
    wspec = pl.BlockSpec((1, D_MODEL, D_EXPERT), lambda ci, ei, ps: (ei, 0, 0))
    slot = lambda a: pl.BlockSpec((1, 1, a.shape[2]), lambda ci, ei, ps: (ei, 0, 0), memory_space=pltpu.SMEM)
    chunk = lambda: pl.BlockSpec((tchunk * SUBLANES, LANES), lambda ci, ei, ps: (ci, 0),
                                 pipeline_mode=pl.Buffered(1))
    return pl.pallas_call(
        kern,
        grid_spec=pltpu.PrefetchScalarGridSpec(
            num_scalar_prefetch=1,
            grid=(n8 // (tchunk * SUBLANES), e),
            in_specs=[slot(loc), slot(vals), chunk(), wspec, wspec,
                      pl.BlockSpec((1, D_EXPERT, D_MODEL), lambda ci, ei, ps: (ei, 0, 0))],
            out_specs=chunk(),
            scratch_shapes=[
                pltpu.VMEM((SUBLANES * stride, LANES), F32),
                pltpu.VMEM((SUBLANES * stride, LANES), F32),
            ],
        ),
        out_shape=jax.ShapeDtypeStruct((n8, LANES), F32),
        compiler_params=pltpu.CompilerParams(dimension_semantics=("arbitrary", "arbitrary"),
                                             vmem_limit_bytes=EXPERT_VMEM_LIMIT),
        name="experts",
    )(pstart, loc, vals, h2r, wg, wu, wd)


def _ple_kernel(x_ref, moe_ref, p_ref, g_ref, wg_ref, wp_ref, o_ref):
    tm = x_ref.shape[0]
    moe = jnp.concatenate([moe_ref[pl.ds(j, tm, stride=SUBLANES), :] for j in range(SUBLANES)], axis=1)
    x = x_ref[...] + moe
    ms = jnp.mean(x * x, axis=-1, keepdims=True)
    hn = (x * lax.rsqrt(ms + NORM_EPS) * g_ref[...]).astype(BF16)
    gate = jax.nn.sigmoid(jnp.dot(hn, wg_ref[...], preferred_element_type=F32))
    emb = jnp.dot(p_ref[0].astype(BF16), wp_ref[...], preferred_element_type=F32)
    o_ref[...] = x + gate * emb


def _ple(x, moe_r, p_all, layer, g_ple, w_g, w_p, tm):
    n = x.shape[0]
    tok = lambda i: (i, 0)
    full = lambda i: (0, 0)
    return pl.pallas_call(
        _ple_kernel,
        grid=(n // tm,),
        in_specs=[
            pl.BlockSpec((tm, D_MODEL), tok),
            pl.BlockSpec((tm * SUBLANES, LANES), tok),
            pl.BlockSpec((1, tm, PLE_DIM), lambda i: (layer, i, 0)),
            pl.BlockSpec((1, D_MODEL), full),
            pl.BlockSpec((D_MODEL, D_MODEL), full),
            pl.BlockSpec((PLE_DIM, D_MODEL), full),
        ],
        out_specs=pl.BlockSpec((tm, D_MODEL), tok),
        out_shape=jax.ShapeDtypeStruct((n, D_MODEL), F32),
        compiler_params=_cparams(("parallel",)),
        name="ple",
    )(x, moe_r, p_all, g_ple, w_g, w_p)


def _rope_tables(seq):
    pos = jnp.arange(seq, dtype=F32)
    inv = ROPE_THETA ** (-jnp.arange(0, HEAD_DIM, 2, dtype=F32) / HEAD_DIM)
    ang = pos[:, None] * inv[None, :]
    ang = jnp.concatenate([ang, ang, ang, ang], axis=-1)
    cos, sin = jnp.cos(ang), jnp.sin(ang)
    first_half = (jnp.arange(LANES) % HEAD_DIM) < HEAD_DIM // 2
    sin_a = jnp.where(first_half[None, :], -sin, 0.0)
    sin_b = jnp.where(first_half[None, :], 0.0, sin)
    return cos, sin_a, sin_b


def _layer_params(i, attn_norm, w_in, q_norm_a, k_norm_a, subln_a, q_norm_b, k_norm_b, w_out):
    wi = w_in[i]
    w = jnp.concatenate([wi[:, 0:1024], wi[:, 1536:2304]], axis=1).astype(BF16)
    wvt = wi[:, 1024:1536].T.astype(BF16)
    scale = HEAD_DIM ** -0.5
    log2e = math.log2(math.e)
    t2 = lambda g, n: jnp.tile(g, n)
    gain = jnp.concatenate([
        t2(q_norm_a[i], 8) * (scale * log2e), t2(k_norm_a[i], 8), t2(q_norm_b[i], 8) * (scale * log2e),
        t2(k_norm_b[i], 2), jnp.ones((NX - COL_KVB - LANES,), F32)])[None, :]
    wo = w_out[i].astype(BF16)
    lam_init = 0.8 - 0.6 * math.exp(-0.3 * i)
    return dict(g_attn=attn_norm[i][None, :], w=w, wvt=wvt, gain=gain, w_oa=wo[0:DA_W], w_ob=wo[DA_W:],
                g_sub=subln_a[i][None, :], lam_init=lam_init)


def _pick(n, pref):
    t = pref
    while n % t:
        t //= 2
    return t


def _encoder(x, p, attn_norm, w_in, q_norm_a, k_norm_a, lambda_q1, lambda_k1, lambda_q2, lambda_k2,
             subln_a, q_norm_b, k_norm_b, sink_b, w_out, ffn_norm, w_router, w_gate_e, w_up_e, w_down_e,
             ple_norm, w_ple_gate, w_ple_proj):
    b, s, _ = x.shape
    n = b * s
    cap = EC_CAPACITY_FACTOR * n // N_EXPERTS
    tm = _pick(s, 1024)
    tq = _pick(s, 512)
    tk = _pick(s // 2, 512)
    tq_sw = _pick(s, 512)
    tchunk = _pick(n, 4096)
    rblk = EXPERT_ROWS
    cos, sin_a, sin_b = _rope_tables(s)
    r = lax.broadcasted_iota(jnp.int32, (256, 256), 0) // HEAD_DIM
    c = lax.broadcasted_iota(jnp.int32, (256, 256), 1) // HEAD_DIM
    bd = (r == c).astype(BF16)
    xf = x.reshape(n, D_MODEL)
    p_all = p.reshape(DEPTH, n, PLE_DIM)
    for i in range(DEPTH):
        lp = _layer_params(i, attn_norm, w_in, q_norm_a, k_norm_a, subln_a, q_norm_b, k_norm_b, w_out)
        qa, ka, qb, kvb, vat = _inproj(xf, lp["g_attn"], lp["w"], lp["wvt"], lp["gain"], cos, sin_a, sin_b, bd,
                                        s, tm)
        lam4 = jnp.stack([lambda_q1[i], lambda_k1[i], lambda_q2[i], lambda_k2[i]]).astype(F32)
        oa = _diffattn(qa.reshape(b, s, DA_W), ka.reshape(b, s, DA_W), vat, lam4,
                       lp["g_sub"], lp["lam_init"], tq, tk)
        ob = _swa(sink_b[i].astype(F32) * math.log2(math.e), qb.reshape(b, s, SW_W),
                   kvb.reshape(b, s, KVB_W), tq_sw)
        x1, h2r, aff_t = _outproj(xf, oa.reshape(n, DA_W), ob.reshape(n, SW_W), lp["w_oa"], lp["w_ob"],
                                  ffn_norm[i][None, :], w_router[i].T, tm)
        loc, vals, cex = _route(aff_t.reshape(N_EXPERTS, n // LANES, LANES), cap, tchunk)
        pstart = jnp.concatenate([cex[:, 0, ::tchunk // LANES], jnp.full((N_EXPERTS, 1), cap, jnp.int32)], axis=1)
        moe_r = _experts(pstart, loc, vals, h2r, w_gate_e[i].astype(BF16), w_up_e[i].astype(BF16),
                         w_down_e[i].astype(BF16), tchunk, rblk)
        xf = _ple(x1, moe_r, p_all, i, ple_norm[i][None, :], w_ple_gate[i].astype(BF16),
                  w_ple_proj[i].astype(BF16), tm)
    return xf.reshape(b, s, D_MODEL)


def kernel(x_prompt, x_sample, p_prompt, p_sample, attn_norm, w_in, q_norm_a, k_norm_a, lambda_q1, lambda_k1,
           lambda_q2, lambda_k2, subln_a, q_norm_b, k_norm_b, sink_b, w_out, ffn_norm, w_router, w_gate_e,
           w_up_e, w_down_e, ple_norm, w_ple_gate, w_ple_proj):
    ws = (attn_norm, w_in, q_norm_a, k_norm_a, lambda_q1, lambda_k1, lambda_q2, lambda_k2, subln_a, q_norm_b,
          k_norm_b, sink_b, w_out, ffn_norm, w_router, w_gate_e, w_up_e, w_down_e, ple_norm, w_ple_gate,
          w_ple_proj)
    return (_encoder(x_prompt, p_prompt, *ws), _encoder(x_sample, p_sample, *ws))
```

```python
import functools
import math

import jax
import jax.numpy as jnp
from jax import lax
from jax.experimental import pallas as pl
from jax.experimental.pallas import tpu as pltpu

F32 = jnp.float32
BF16 = jnp.bfloat16

D_MODEL = 1024
DEPTH = 4
HEAD_DIM = 64
DA_HEADS = 4
SW_Q_HEADS = 8
SW_KV_HEADS = 2
SW_GROUP = SW_Q_HEADS // SW_KV_HEADS
WINDOW = 128
N_EXPERTS = 16
EC_CAPACITY_FACTOR = 2
D_EXPERT = 1024
PLE_DIM = 256
ROPE_THETA = 10000.0
NORM_EPS = 1e-6
MASK_VALUE = -1e30

LANES = 128
SUBLANES = 8
DA_W = DA_HEADS * 2 * HEAD_DIM
SW_W = SW_Q_HEADS * HEAD_DIM
KVB_W = 2 * LANES
COL_QA, COL_KA, COL_QB, COL_KVB = 0, 512, 1024, 1536
NX = 1792
N_ROPE_TILES = (COL_KVB + LANES) // LANES
VMEM_LIMIT = 48 * 1024 * 1024
EXPERT_VMEM_LIMIT = 56 * 1024 * 1024
EXPERT_ROWS = 288
SLOT_PAD = 512


def _cparams(sem):
    return pltpu.CompilerParams(dimension_semantics=sem, vmem_limit_bytes=VMEM_LIMIT)


def _inproj_kernel(x_ref, gat_ref, w_ref, wvt_ref, gain_ref, cos_ref, sa_ref, sb_ref, bd_ref,
                   qa_ref, ka_ref, qb_ref, kvb_ref, vat_ref, xn_ref):
    x = x_ref[...]
    ms = jnp.mean(x * x, axis=-1, keepdims=True)
    xn_ref[...] = (x * lax.rsqrt(ms + NORM_EPS) * gat_ref[...]).astype(BF16)
    cos = cos_ref[...]
    sa = sa_ref[...]
    sb = sb_ref[...]
    outs = ((qa_ref, COL_QA), (ka_ref, COL_KA), (qb_ref, COL_QB), (kvb_ref, COL_KVB))
    vat_ref[0] = lax.dot_general(wvt_ref[...], xn_ref[...], (((1,), (1,)), ((), ())),
                                 preferred_element_type=F32).astype(BF16)

    def out_for(col):
        for ref, base in reversed(outs):
            if col >= base:
                return ref, col - base
        raise AssertionError

    def project(c):
        return jnp.dot(xn_ref[...], w_ref[:, c * 256:(c + 1) * 256], preferred_element_type=F32)

    nchunks = NX // 256
    p_next = project(0)
    for c in range(nchunks):
        c0 = c * 256
        p = p_next
        if c + 1 < nchunks:
            p_next = project(c + 1)
        normed = [(c0 + t * LANES) // LANES < N_ROPE_TILES for t in range(2)]
        if any(normed):
            ss = jnp.dot((p * p).astype(BF16), bd_ref[...], preferred_element_type=F32)
        for t in range(2):
            col = c0 + t * LANES
            y = p[:, t * LANES:(t + 1) * LANES]
            if normed[t]:
                sst = ss[:, t * LANES:(t + 1) * LANES]
                y = y * lax.rsqrt(sst * (1.0 / HEAD_DIM) + NORM_EPS) * gain_ref[:, col:col + LANES]
                y = y * cos + pltpu.roll(y, 96, 1) * sa + pltpu.roll(y, 32, 1) * sb
            ref, off = out_for(col)
            ref[:, off:off + LANES] = y.astype(BF16)


def _inproj(x, g_attn, w, wvt, gain, cos, sa, sb, bd, seq, tm):
    n = x.shape[0]
    nblk_s = seq // tm
    tok = lambda i: (i, 0)
    full = lambda i: (0, 0)
    rope = lambda i: (i % nblk_s, 0)
    return pl.pallas_call(
        _inproj_kernel,
        grid=(n // tm,),
        in_specs=[
            pl.BlockSpec((tm, D_MODEL), tok),
            pl.BlockSpec((1, D_MODEL), full),
            pl.BlockSpec((D_MODEL, NX), full),
            pl.BlockSpec((DA_W, D_MODEL), full),
            pl.BlockSpec((1, NX), full),
            pl.BlockSpec((tm, LANES), rope),
            pl.BlockSpec((tm, LANES), rope),
            pl.BlockSpec((tm, LANES), rope),
            pl.BlockSpec((256, 256), full),
        ],
        out_specs=[
            pl.BlockSpec((tm, DA_W), tok),
            pl.BlockSpec((tm, DA_W), tok),
            pl.BlockSpec((tm, SW_W), tok),
            pl.BlockSpec((tm, KVB_W), tok),
            pl.BlockSpec((1, DA_W, tm), lambda i: (i // nblk_s, 0, i % nblk_s)),
        ],
        out_shape=[
            jax.ShapeDtypeStruct((n, DA_W), BF16),
            jax.ShapeDtypeStruct((n, DA_W), BF16),
            jax.ShapeDtypeStruct((n, SW_W), BF16),
            jax.ShapeDtypeStruct((n, KVB_W), BF16),
            jax.ShapeDtypeStruct((n // seq, DA_W, seq), BF16),
        ],
        scratch_shapes=[pltpu.VMEM((tm, D_MODEL), BF16)],
        compiler_params=_cparams(("parallel",)),
        name="inproj",
    )(x, g_attn, w, wvt, gain, cos, sa, sb, bd)


def _diffattn_kernel(q_ref, k_ref, vt_ref, lam4_ref, g_ref, o_ref, m_ref, l_ref, acc_ref, qst_ref, s_ref,
                     *, tq, tk, nk, nq, unroll, lam_init):
    def load_queries(i):
        q0 = pl.multiple_of(i * tq, tq)
        qt = q_ref[0, pl.ds(q0, tq), :].astype(F32).T
        feat = lax.broadcasted_iota(jnp.int32, qt.shape, 0)
        zero = jnp.zeros_like(qt)
        qst_ref[...] = jnp.concatenate([jnp.where(feat < HEAD_DIM, qt, zero), jnp.where(feat >= HEAD_DIM, qt, zero)],
                                       axis=1).astype(BF16)

    def reset_state():
        m_ref[...] = jnp.full(m_ref.shape, -jnp.inf, F32)
        l_ref[...] = jnp.zeros(l_ref.shape, F32)
        acc_ref[...] = jnp.zeros(acc_ref.shape, F32)

    def scores(j):
        k0 = pl.multiple_of(j * tk, tk)
        return jnp.dot(k_ref[0, pl.ds(k0, tk), :], qst_ref[...], preferred_element_type=F32)

    def accumulate(s, j):
        k0 = pl.multiple_of(j * tk, tk)
        vtj = vt_ref[0, :, pl.ds(k0, tk)]
        m_old = m_ref[...]
        m_new = jnp.maximum(m_old, jnp.max(s, axis=0, keepdims=True))
        alpha = jnp.exp2(m_old - m_new)
        p = jnp.exp2(s - m_new)
        l_ref[...] = alpha * l_ref[...] + jnp.sum(p, axis=0, keepdims=True)
        acc_ref[...] = alpha * acc_ref[...] + jnp.dot(vtj, p.astype(BF16), preferred_element_type=F32)
        m_ref[...] = m_new

    def finalize(i):
        ot = acc_ref[...] / l_ref[...]
        lam4 = lam4_ref[...]
        lam = (jnp.exp(jnp.sum(lam4[0:1] * lam4[1:2], axis=-1, keepdims=True))
               - jnp.exp(jnp.sum(lam4[2:3] * lam4[3:4], axis=-1, keepdims=True)) + lam_init)
        o = (ot[:, :tq] - lam * ot[:, tq:]).T
        ms = jnp.mean(o * o, axis=-1, keepdims=True)
        o = o * lax.rsqrt(ms + NORM_EPS) * g_ref[...] * (1.0 - lam_init)
        o_ref[0, pl.ds(pl.multiple_of(i * tq, tq), tq), :] = o.astype(BF16)

    def trip(jj, _):
        j0 = unroll * jj
        for u in range(unroll):
            s_ref[(u + 1) % 2] = scores(j0 + u + 1)
            accumulate(s_ref[u % 2], j0 + u)
        return 0

    def query_tile(i, _):
        lax.fori_loop(0, nk // unroll - 1, trip, 0)
        j0 = nk - unroll
        for u in range(unroll - 1):
            s_ref[(u + 1) % 2] = scores(j0 + u + 1)
            accumulate(s_ref[u % 2], j0 + u)
        load_queries(jnp.minimum(i + 1, nq - 1))
        s_ref[0] = scores(0)
        accumulate(s_ref[1], nk - 1)
        finalize(i)
        reset_state()
        return 0

    load_queries(0)
    reset_state()
    s_ref[0] = scores(0)
    lax.fori_loop(0, nq, query_tile, 0)


def _diffattn(qa, ka, vat, lam4, g_sub, lam_init, tq, tk):
    b, s, _ = qa.shape
    nk = s // tk
    unroll = 8 if nk % 8 == 0 else (4 if nk % 4 == 0 else 2)
    kern = functools.partial(_diffattn_kernel, tq=tq, tk=tk, nk=nk, nq=s // tq, unroll=unroll, lam_init=lam_init)
    per_head = lambda bi, h: (bi, 0, h)
    return pl.pallas_call(
        kern,
        grid=(b, DA_HEADS),
        in_specs=[
            pl.BlockSpec((1, s, LANES), per_head),
            pl.BlockSpec((1, s, LANES), per_head),
            pl.BlockSpec((1, LANES, s), lambda bi, h: (bi, h, 0)),
            pl.BlockSpec((4, HEAD_DIM), lambda bi, h: (0, 0)),
            pl.BlockSpec((1, LANES), lambda bi, h: (0, 0)),
        ],
        out_specs=pl.BlockSpec((1, s, LANES), per_head),
        out_shape=jax.ShapeDtypeStruct((b, s, DA_W), BF16),
        scratch_shapes=[
            pltpu.VMEM((1, 2 * tq), F32),
            pltpu.VMEM((1, 2 * tq), F32),
            pltpu.VMEM((LANES, 2 * tq), F32),
            pltpu.VMEM((LANES, 2 * tq), BF16),
            pltpu.VMEM((2, tk, 2 * tq), F32),
        ],
        compiler_params=_cparams(("parallel", "parallel")),
        name="diffattn",
    )(qa, ka, vat, lam4, g_sub)


def _swa_kernel(sink_ref, q_ref, kv_ref, o_ref, *, tq, seq):
    i = pl.program_id(1)
    kw = 3 * WINDOW
    low = lax.broadcasted_iota(jnp.int32, (WINDOW, LANES), 1) < HEAD_DIM
    blocks = {}

    def block_operands(sb):
        if sb not in blocks:
            q0 = i * tq + sb * WINDOW
            g0 = pl.multiple_of(jnp.clip(q0 - WINDOW, 0, seq - kw), WINDOW)
            kv = kv_ref[0, pl.ds(g0, kw), :]
            k2 = kv[:, 0:LANES]
            v2 = kv[:, LANES:2 * LANES]
            qpos = q0 + lax.broadcasted_iota(jnp.int32, (WINDOW, kw), 0)
            kpos = g0 + lax.broadcasted_iota(jnp.int32, (WINDOW, kw), 1)
            blocks[sb] = dict(mask=jnp.abs(kpos - qpos) <= WINDOW,
                              k=(k2, pltpu.roll(k2.astype(F32), HEAD_DIM, 1).astype(BF16)),
                              v=(v2, pltpu.roll(v2.astype(F32), HEAD_DIM, 1).astype(BF16)))
        return blocks[sb]

    def issue_scores(sb, h):
        ops = block_operands(sb)
        rows = slice(sb * WINDOW, (sb + 1) * WINDOW)
        qp = q_ref[0, rows, (h // 2) * LANES:(h // 2 + 1) * LANES]
        qm = jnp.where(low if h % 2 == 0 else jnp.logical_not(low), qp, jnp.zeros_like(qp))
        swapped = int(h % 2 != h // SW_GROUP)
        return lax.dot_general(qm, ops["k"][swapped], (((1,), (1,)), ((), ())), preferred_element_type=F32)

    def finish(sb, h, scores):
        ops = block_operands(sb)
        s = jnp.where(ops["mask"], scores, MASK_VALUE)
        sink = sink_ref[h]
        m = jnp.maximum(jnp.max(s, axis=-1, keepdims=True), sink)
        e = jnp.exp2(s - m)
        den = jnp.sum(e, axis=-1, keepdims=True) + jnp.exp2(sink - m)
        swapped = int(h % 2 != h // SW_GROUP)
        return jnp.dot(e.astype(BF16), ops["v"][swapped], preferred_element_type=F32) * (1.0 / den)

    units = [(sb, h) for sb in range(tq // WINDOW) for h in range(SW_Q_HEADS)]
    ahead = 3
    pending, outs = {}, {}
    for n in range(len(units) + ahead):
        if n < len(units):
            pending[units[n]] = issue_scores(*units[n])
        if n >= ahead:
            sb, h = units[n - ahead]
            outs[h] = finish(sb, h, pending.pop((sb, h)))
            if h % 2 == 1:
                rows = slice(sb * WINDOW, (sb + 1) * WINDOW)
                o = jnp.where(low, outs.pop(h - 1), outs.pop(h))
                o_ref[0, rows, (h // 2) * LANES:(h // 2 + 1) * LANES] = o.astype(BF16)


def _swa(sink, qbx, kvb, tq):
    b, s, _ = qbx.shape
    kern = functools.partial(_swa_kernel, tq=tq, seq=s)
    return pl.pallas_call(
        kern,
        grid_spec=pltpu.PrefetchScalarGridSpec(
            num_scalar_prefetch=1,
            grid=(b, s // tq),
            in_specs=[
                pl.BlockSpec((1, tq, SW_W), lambda bi, i, sk: (bi, i, 0)),
                pl.BlockSpec((1, s, KVB_W), lambda bi, i, sk: (bi, 0, 0)),
            ],
            out_specs=pl.BlockSpec((1, tq, SW_W), lambda bi, i, sk: (bi, i, 0)),
        ),
        out_shape=jax.ShapeDtypeStruct((b, s, SW_W), BF16),
        compiler_params=_cparams(("parallel", "parallel")),
        name="swa",
    )(sink, qbx, kvb)


def _outproj_kernel(x_ref, oa_ref, obx_ref, woa_ref, wobx_ref, g_ref, wrt_ref, x1_ref, h2r_ref, aff_ref):
    x1 = (x_ref[...] + jnp.dot(oa_ref[...], woa_ref[...], preferred_element_type=F32)
          + jnp.dot(obx_ref[...], wobx_ref[...], preferred_element_type=F32))
    x1_ref[...] = x1
    ms = jnp.mean(x1 * x1, axis=-1, keepdims=True)
    h2 = x1 * lax.rsqrt(ms + NORM_EPS) * g_ref[...]
    tm = h2.shape[0]
    for j in range(SUBLANES):
        h2r_ref[pl.ds(j, tm, stride=SUBLANES), :] = h2[:, j * LANES:(j + 1) * LANES]
    h_hi = h2.astype(BF16)
    h_lo = (h2 - h_hi.astype(F32)).astype(BF16)
    w = wrt_ref[...]
    w_hi = w.astype(BF16)
    w_lo = (w - w_hi.astype(F32)).astype(BF16)
    logits = lax.dot_general(jnp.concatenate([w_hi, w_hi, w_lo], axis=1),
                             jnp.concatenate([h_hi, h_lo, h_hi], axis=1),
                             (((1,), (1,)), ((), ())), preferred_element_type=F32)
    mx = jnp.max(logits, axis=0, keepdims=True)
    e = jnp.exp(logits - mx)
    aff_ref[...] = e / jnp.sum(e, axis=0, keepdims=True)


def _outproj(x, oa, obx, w_oa, w_obx, g_ffn, w_rt, tm):
    n = x.shape[0]
    tok = lambda i: (i, 0)
    full = lambda i: (0, 0)
    return pl.pallas_call(
        _outproj_kernel,
        grid=(n // tm,),
        in_specs=[
            pl.BlockSpec((tm, D_MODEL), tok),
            pl.BlockSpec((tm, DA_W), tok),
            pl.BlockSpec((tm, SW_W), tok),
            pl.BlockSpec((DA_W, D_MODEL), full),
            pl.BlockSpec((SW_W, D_MODEL), full),
            pl.BlockSpec((1, D_MODEL), full),
            pl.BlockSpec((N_EXPERTS, D_MODEL), full),
        ],
        out_specs=[
            pl.BlockSpec((tm, D_MODEL), tok),
            pl.BlockSpec((tm * SUBLANES, LANES), tok),
            pl.BlockSpec((N_EXPERTS, tm), lambda i: (0, i)),
        ],
        out_shape=[
            jax.ShapeDtypeStruct((n, D_MODEL), F32),
            jax.ShapeDtypeStruct((n * SUBLANES, LANES), F32),
            jax.ShapeDtypeStruct((N_EXPERTS, n), F32),
        ],
        compiler_params=_cparams(("parallel",)),
        name="outproj",
    )(x, oa, obx, w_oa, w_obx, g_ffn, w_rt)


def _route_kernel(aff_ref, loc_ref, vals_ref, cex_ref, thr_ref, *, cap, nchunk, pblk, tchunk):
    n_exp = aff_ref.shape[0]

    def expert_bits(e):
        return pltpu.bitcast(aff_ref[e], jnp.int32)

    def count(mask):
        per_lane = jnp.sum(jnp.where(mask, 1.0, 0.0), axis=0, keepdims=True)
        return jnp.sum(per_lane, axis=1, keepdims=True)

    def search(i, thrs):
        bit = jnp.left_shift(jnp.int32(1), 30 - i)
        out = []
        for e in range(n_exp):
            cand = thrs[e] | bit
            out.append(jnp.where(count(expert_bits(e) >= cand) >= cap, cand, thrs[e]))
        return tuple(out)

    thrs = lax.fori_loop(0, 31, search, tuple(jnp.zeros((1, 1), jnp.int32) for _ in range(n_exp)))
    for e in range(n_exp):
        thr_ref[e] = jnp.broadcast_to(thrs[e], (1, LANES))
    lax.fori_loop(0, n_exp, functools.partial(_route_expert, aff_ref, loc_ref, vals_ref, cex_ref, thr_ref,
                                              count, cap, nchunk, pblk, tchunk), 0)


def _route_expert(aff_ref, loc_ref, vals_ref, cex_ref, thr_ref, count, cap, nchunk, pblk, tchunk, e, carry):
    a = aff_ref[e]
    bits = pltpu.bitcast(a, jnp.int32)
    thr = thr_ref[e]
    gt = bits > thr
    eq = bits == thr
    need = cap - count(gt)

    r128 = lax.broadcasted_iota(jnp.int32, (LANES, LANES), 0)
    c128 = lax.broadcasted_iota(jnp.int32, (LANES, LANES), 1)
    u_incl = jnp.where(r128 <= c128, 1.0, 0.0).astype(BF16)
    ones = jnp.ones((LANES, LANES), BF16)
    rc = lax.broadcasted_iota(jnp.int32, (nchunk, nchunk), 0)
    cc = lax.broadcasted_iota(jnp.int32, (nchunk, nchunk), 1)
    l_strict = jnp.where(cc < rc, 1.0, 0.0).astype(BF16)
    u_strict = jnp.where(rc < cc, 1.0, 0.0).astype(BF16)

    def chunk_scan(mask_bf):
        incl = jnp.dot(mask_bf, u_incl, preferred_element_type=F32)
        tot_b = jnp.dot(mask_bf, ones, preferred_element_type=F32)
        cexcl_b = jnp.dot(l_strict, tot_b.astype(BF16), preferred_element_type=F32)
        return incl, cexcl_b

    eq_bf = jnp.where(eq, 1.0, 0.0).astype(BF16)
    incl_eq, cexcl_eq = chunk_scan(eq_bf)
    rank_eq = cexcl_eq + incl_eq - eq_bf.astype(F32)
    sel = gt | (eq & (rank_eq < need))
    sel_bf = jnp.where(sel, 1.0, 0.0).astype(BF16)
    lc, cexcl_b = chunk_scan(sel_bf)

    tot_row = lax.dot_general(jnp.ones((8, LANES), BF16), sel_bf, (((1,), (1,)), ((), ())),
                              preferred_element_type=F32)
    cexcl_row = jnp.dot(tot_row.astype(BF16), u_strict, preferred_element_type=F32)[0:1]
    cincl_row = cexcl_row + tot_row[0:1]

    a_hi = a.astype(BF16)
    r1 = a - a_hi.astype(F32)
    a_mid = r1.astype(BF16)
    a_lo = (r1 - a_mid.astype(F32)).astype(BF16)
    kidx = lax.broadcasted_iota(jnp.int32, (nchunk, LANES), 0).astype(F32)
    cex_hi = jnp.floor(cexcl_b * (1.0 / LANES))
    cex_lo = cexcl_b - cex_hi * LANES
    table = jnp.concatenate([lc.astype(BF16), a_hi, a_mid, a_lo, kidx.astype(BF16),
                             cex_hi.astype(BF16), cex_lo.astype(BF16)], axis=1)

    lane_f = lax.broadcasted_iota(jnp.int32, (pblk, LANES), 1).astype(F32)
    eye = r128 == c128

    def slot_block(bi, _):
        base = bi * pblk
        pc = (base + lax.broadcasted_iota(jnp.int32, (pblk, nchunk), 0)).astype(F32)
        onehot_k = jnp.where((cexcl_row <= pc) & (pc < cincl_row), 1.0, 0.0).astype(BF16)
        r = jnp.dot(onehot_k, table, preferred_element_type=F32)
        r_lc = r[:, 0:LANES]
        r_aff = (r[:, LANES:2 * LANES] + r[:, 2 * LANES:3 * LANES]) + r[:, 3 * LANES:4 * LANES]
        r_k = r[:, 4 * LANES:5 * LANES]
        r_cex = r[:, 5 * LANES:6 * LANES] * LANES + r[:, 6 * LANES:7 * LANES]
        p_loc = (base + lax.broadcasted_iota(jnp.int32, (pblk, LANES), 0)).astype(F32) - r_cex
        below = jnp.where(r_lc <= p_loc, 1.0, 0.0).astype(BF16)
        t_loc = jnp.dot(below, ones, preferred_element_type=F32)
        val = jnp.sum(jnp.where(lane_f == t_loc, r_aff, 0.0), axis=-1, keepdims=True)
        tok = r_k * LANES + t_loc
        for sb in range(pblk // LANES):
            rows = slice(sb * LANES, (sb + 1) * LANES)
            out = pl.ds(pl.multiple_of(base + sb * LANES, LANES), LANES)
            tok_row = jnp.sum(jnp.where(eye, tok[rows], 0.0), axis=0, keepdims=True).astype(jnp.int32)
            loc_ref[e, :, out] = (tok_row & (tchunk - 1)) * SUBLANES
            vals_ref[e, :, out] = jnp.sum(jnp.where(eye, val[rows], 0.0), axis=0, keepdims=True)
        return 0

    lax.fori_loop(0, cap // pblk, slot_block, 0)
    loc_ref[e, :, pl.ds(cap, SLOT_PAD)] = jnp.zeros((1, SLOT_PAD), jnp.int32)
    cex_ref[e] = cexcl_row.astype(jnp.int32)
    return carry


def _route(aff3, cap, tchunk):
    e, nchunk, _ = aff3.shape
    pblk = min(1024, cap)
    kern = functools.partial(_route_kernel, cap=cap, nchunk=nchunk, pblk=pblk, tchunk=tchunk)
    return pl.pallas_call(
        kern,
        out_shape=[
            jax.ShapeDtypeStruct((e, 1, cap + SLOT_PAD), jnp.int32),
            jax.ShapeDtypeStruct((e, 1, cap), F32),
            jax.ShapeDtypeStruct((e, 1, nchunk), jnp.int32),
        ],
        scratch_shapes=[pltpu.VMEM((e, 1, LANES), jnp.int32)],
        compiler_params=pltpu.CompilerParams(vmem_limit_bytes=VMEM_LIMIT),
        name="route",
    )(aff3)


def _expert_kernel(ps_ref, loc_ref, val_ref, h2r_ref, wg_ref, wu_ref, wd_ref, moe_ref, xt_ref, yt_ref,
                   *, rblk, stride):
    c = pl.program_id(0)
    e = pl.program_id(1)

    @pl.when(e == 0)
    def _():
        moe_ref[...] = jnp.zeros(moe_ref.shape, F32)

    p0 = ps_ref[e, c]
    p1 = ps_ref[e, c + 1]
    unroll = SUBLANES

    def tile_row(p):
        return pl.multiple_of(loc_ref[0, 0, p], SUBLANES)

    nblk = (p1 - p0 + rblk - 1) // rblk

    def gather_rows(base, r0, count):
        for i in range(count):
            src = tile_row(base + r0 + i)
            xt_ref[pl.ds(r0 + i, SUBLANES, stride=stride), :] = h2r_ref[pl.ds(src, SUBLANES), :]

    @pl.when(nblk > 0)
    def _():
        def first_gather(g, _):
            gather_rows(p0, g * 2 * unroll, 2 * unroll)
            return 0

        lax.fori_loop(0, rblk // (2 * unroll), first_gather, 0)

    def block(b, _):
        base = p0 + b * rblk
        x = jnp.concatenate([xt_ref[pl.ds(j * stride, rblk), :].astype(BF16) for j in range(SUBLANES)], axis=1)
        g = jnp.dot(x, wg_ref[0], preferred_element_type=F32)
        u = jnp.dot(x, wu_ref[0], preferred_element_type=F32)
        hmid = (g * jax.nn.sigmoid(g) * u).astype(BF16)
        y = jnp.dot(hmid, wd_ref[0], preferred_element_type=F32)
        for j in range(SUBLANES):
            yt_ref[pl.ds(j * stride, rblk), :] = y[:, j * LANES:(j + 1) * LANES]

        nvalid = jnp.minimum(rblk, p1 - base)

        def scatter_rows(r0, count):
            new = []
            for i in range(count):
                dst = tile_row(base + r0 + i)
                contrib = yt_ref[pl.ds(r0 + i, SUBLANES, stride=stride), :] * val_ref[0, 0, base + r0 + i]
                new.append((dst, moe_ref[pl.ds(dst, SUBLANES), :] + contrib))
            for dst, v in new:
                moe_ref[pl.ds(dst, SUBLANES), :] = v

        def scatter_pair(g, _):
            scatter_rows(g * 2 * unroll, unroll)
            scatter_rows(g * 2 * unroll + unroll, unroll)
            return 0

        has_next = b + 1 < nblk

        @pl.when(has_next)
        def _():
            def scatter_and_gather(g, _):
                scatter_pair(g, 0)
                gather_rows(base + rblk, g * 2 * unroll, 2 * unroll)
                return 0

            lax.fori_loop(0, rblk // (2 * unroll), scatter_and_gather, 0)

        @pl.when(jnp.logical_not(has_next))
        def _():
            npairs = nvalid // (2 * unroll)
            lax.fori_loop(0, npairs, scatter_pair, 0)

            def scatter_one(r, _):
                scatter_rows(r, 1)
                return 0

            lax.fori_loop(npairs * 2 * unroll, nvalid, scatter_one, 0)

        return 0

    lax.fori_loop(0, nblk, block, 0)


def _experts(pstart, loc, vals, h2r, wg, wu, wd, layer, tchunk, rblk):
    e, _, cap_pad = loc.shape
    n8 = h2r.shape[0]
    stride = rblk + SUBLANES
    kern = functools.partial(_expert_kernel, rblk=rblk, stride=stride)
    expert = lambda ci, ei, ps: (layer * e + ei, 0, 0)
    wspec = pl.BlockSpec((1, D_MODEL, D_EXPERT), expert)
    slot = lambda a: pl.BlockSpec((1, 1, a.shape[2]), lambda ci, ei, ps: (ei, 0, 0), memory_space=pltpu.SMEM)
    chunk = lambda: pl.BlockSpec((tchunk * SUBLANES, LANES), lambda ci, ei, ps: (ci, 0),
                                 pipeline_mode=pl.Buffered(1))
    return pl.pallas_call(
        kern,
        grid_spec=pltpu.PrefetchScalarGridSpec(
            num_scalar_prefetch=1,
            grid=(n8 // (tchunk * SUBLANES), e),
            in_specs=[slot(loc), slot(vals), chunk(), wspec, wspec,
                      pl.BlockSpec((1, D_EXPERT, D_MODEL), expert)],
            out_specs=chunk(),
            scratch_shapes=[
                pltpu.VMEM((SUBLANES * stride, LANES), F32),
                pltpu.VMEM((SUBLANES * stride, LANES), F32),
            ],
        ),
        out_shape=jax.ShapeDtypeStruct((n8, LANES), F32),
        compiler_params=pltpu.CompilerParams(dimension_semantics=("arbitrary", "arbitrary"),
                                             vmem_limit_bytes=EXPERT_VMEM_LIMIT),
        name="experts",
    )(pstart, loc, vals, h2r, wg, wu, wd)


def _ple_kernel(x_ref, moe_ref, p_ref, g_ref, wg_ref, wp_ref, o_ref):
    tm = x_ref.shape[0]
    moe = jnp.concatenate([moe_ref[pl.ds(j, tm, stride=SUBLANES), :] for j in range(SUBLANES)], axis=1)
    x = x_ref[...] + moe
    ms = jnp.mean(x * x, axis=-1, keepdims=True)
    hn = (x * lax.rsqrt(ms + NORM_EPS) * g_ref[...]).astype(BF16)
    gate = jax.nn.sigmoid(jnp.dot(hn, wg_ref[...], preferred_element_type=F32))
    emb = jnp.dot(p_ref[0].astype(BF16), wp_ref[...], preferred_element_type=F32)
    o_ref[...] = x + gate * emb


def _ple(x, moe_r, p_all, layer, g_ple, w_g, w_p, tm):
    n = x.shape[0]
    tok = lambda i: (i, 0)
    full = lambda i: (0, 0)
    return pl.pallas_call(
        _ple_kernel,
        grid=(n // tm,),
        in_specs=[
            pl.BlockSpec((tm, D_MODEL), tok),
            pl.BlockSpec((tm * SUBLANES, LANES), tok),
            pl.BlockSpec((1, tm, PLE_DIM), lambda i: (layer, i, 0)),
            pl.BlockSpec((1, D_MODEL), full),
            pl.BlockSpec((D_MODEL, D_MODEL), full),
            pl.BlockSpec((PLE_DIM, D_MODEL), full),
        ],
        out_specs=pl.BlockSpec((tm, D_MODEL), tok),
        out_shape=jax.ShapeDtypeStruct((n, D_MODEL), F32),
        compiler_params=_cparams(("parallel",)),
        name="ple",
    )(x, moe_r, p_all, g_ple, w_g, w_p)


def _rope_tables(seq):
    pos = jnp.arange(seq, dtype=F32)
    inv = ROPE_THETA ** (-jnp.arange(0, HEAD_DIM, 2, dtype=F32) / HEAD_DIM)
    ang = pos[:, None] * inv[None, :]
    ang = jnp.concatenate([ang, ang, ang, ang], axis=-1)
    cos, sin = jnp.cos(ang), jnp.sin(ang)
    first_half = (jnp.arange(LANES) % HEAD_DIM) < HEAD_DIM // 2
    sin_a = jnp.where(first_half[None, :], -sin, 0.0)
    sin_b = jnp.where(first_half[None, :], 0.0, sin)
    return cos, sin_a, sin_b


def _layer_params(i, attn_norm, w_in, q_norm_a, k_norm_a, subln_a, q_norm_b, k_norm_b, w_out):
    wi = w_in[i]
    w = jnp.concatenate([wi[:, 0:1024], wi[:, 1536:2304]], axis=1).astype(BF16)
    wvt = wi[:, 1024:1536].T.astype(BF16)
    scale = HEAD_DIM ** -0.5
    log2e = math.log2(math.e)
    t2 = lambda g, n: jnp.tile(g, n)
    gain = jnp.concatenate([
        t2(q_norm_a[i], 8) * (scale * log2e), t2(k_norm_a[i], 8), t2(q_norm_b[i], 8) * (scale * log2e),
        t2(k_norm_b[i], 2), jnp.ones((NX - COL_KVB - LANES,), F32)])[None, :]
    wo = w_out[i].astype(BF16)
    lam_init = 0.8 - 0.6 * math.exp(-0.3 * i)
    return dict(g_attn=attn_norm[i][None, :], w=w, wvt=wvt, gain=gain, w_oa=wo[0:DA_W], w_ob=wo[DA_W:],
                g_sub=subln_a[i][None, :], lam_init=lam_init)


def _pick(n, pref):
    t = pref
    while n % t:
        t //= 2
    return t


def _encoder(x, p, attn_norm, w_in, q_norm_a, k_norm_a, lambda_q1, lambda_k1, lambda_q2, lambda_k2,
             subln_a, q_norm_b, k_norm_b, sink_b, w_out, ffn_norm, w_router, w_gate_e, w_up_e, w_down_e,
             ple_norm, w_ple_gate, w_ple_proj):
    b, s, _ = x.shape
    n = b * s
    cap = EC_CAPACITY_FACTOR * n // N_EXPERTS
    tm = _pick(s, 1024)
    tq = _pick(s, 512)
    tk = _pick(s // 2, 512)
    tq_sw = _pick(s, 512)
    tchunk = _pick(n, 4096)
    rblk = EXPERT_ROWS
    cos, sin_a, sin_b = _rope_tables(s)
    r = lax.broadcasted_iota(jnp.int32, (256, 256), 0) // HEAD_DIM
    c = lax.broadcasted_iota(jnp.int32, (256, 256), 1) // HEAD_DIM
    bd = (r == c).astype(BF16)
    xf = x.reshape(n, D_MODEL)
    p_all = p.reshape(DEPTH, n, PLE_DIM)
    wg_all = w_gate_e.astype(BF16).reshape(DEPTH * N_EXPERTS, D_MODEL, D_EXPERT)
    wu_all = w_up_e.astype(BF16).reshape(DEPTH * N_EXPERTS, D_MODEL, D_EXPERT)
    wd_all = w_down_e.astype(BF16).reshape(DEPTH * N_EXPERTS, D_EXPERT, D_MODEL)
    for i in range(DEPTH):
        lp = _layer_params(i, attn_norm, w_in, q_norm_a, k_norm_a, subln_a, q_norm_b, k_norm_b, w_out)
        qa, ka, qb, kvb, vat = _inproj(xf, lp["g_attn"], lp["w"], lp["wvt"], lp["gain"], cos, sin_a, sin_b, bd,
                                        s, tm)
        lam4 = jnp.stack([lambda_q1[i], lambda_k1[i], lambda_q2[i], lambda_k2[i]]).astype(F32)
        oa = _diffattn(qa.reshape(b, s, DA_W), ka.reshape(b, s, DA_W), vat, lam4,
                       lp["g_sub"], lp["lam_init"], tq, tk)
        ob = _swa(sink_b[i].astype(F32) * math.log2(math.e), qb.reshape(b, s, SW_W),
                   kvb.reshape(b, s, KVB_W), tq_sw)
        x1, h2r, aff_t = _outproj(xf, oa.reshape(n, DA_W), ob.reshape(n, SW_W), lp["w_oa"], lp["w_ob"],
                                  ffn_norm[i][None, :], w_router[i].T, tm)
        loc, vals, cex = _route(aff_t.reshape(N_EXPERTS, n // LANES, LANES), cap, tchunk)
        pstart = jnp.concatenate([cex[:, 0, ::tchunk // LANES], jnp.full((N_EXPERTS, 1), cap, jnp.int32)], axis=1)
        moe_r = _experts(pstart, loc, vals, h2r, wg_all, wu_all, wd_all, i, tchunk, rblk)
        xf = _ple(x1, moe_r, p_all, i, ple_norm[i][None, :], w_ple_gate[i].astype(BF16),
                  w_ple_proj[i].astype(BF16), tm)
    return xf.reshape(b, s, D_MODEL)


def kernel(x_prompt, x_sample, p_prompt, p_sample, attn_norm, w_in, q_norm_a, k_norm_a, lambda_q1, lambda_k1,
           lambda_q2, lambda_k2, subln_a, q_norm_b, k_norm_b, sink_b, w_out, ffn_norm, w_router, w_gate_e,
           w_up_e, w_down_e, ple_norm, w_ple_gate, w_ple_proj):
    ws = (attn_norm, w_in, q_norm_a, k_norm_a, lambda_q1, lambda_k1, lambda_q2, lambda_k2, subln_a, q_norm_b,
          k_norm_b, sink_b, w_out, ffn_norm, w_router, w_gate_e, w_up_e, w_down_e, ple_norm, w_ple_gate,
          w_ple_proj)
    return (_encoder(x_prompt, p_prompt, *ws), _encoder(x_sample, p_sample, *ws))
```

```python
import functools
import math

import jax
import jax.numpy as jnp
from jax import lax
from jax.experimental import pallas as pl
from jax.experimental.pallas import tpu as pltpu

F32 = jnp.float32
BF16 = jnp.bfloat16

D_MODEL = 1024
DEPTH = 4
HEAD_DIM = 64
DA_HEADS = 4
SW_Q_HEADS = 8
SW_KV_HEADS = 2
SW_GROUP = SW_Q_HEADS // SW_KV_HEADS
WINDOW = 128
N_EXPERTS = 16
EC_CAPACITY_FACTOR = 2
D_EXPERT = 1024
PLE_DIM = 256
ROPE_THETA = 10000.0
NORM_EPS = 1e-6
MASK_VALUE = -1e30

LANES = 128
SUBLANES = 8
DA_W = DA_HEADS * 2 * HEAD_DIM
SW_W = SW_Q_HEADS * HEAD_DIM
KVB_W = 2 * LANES
COL_QA, COL_KA, COL_QB, COL_KVB = 0, 512, 1024, 1536
NX = 1792
N_ROPE_TILES = (COL_KVB + LANES) // LANES
VMEM_LIMIT = 48 * 1024 * 1024
EXPERT_VMEM_LIMIT = 56 * 1024 * 1024
EXPERT_ROWS = 288
SLOT_PAD = 512


def _cparams(sem):
    return pltpu.CompilerParams(dimension_semantics=sem, vmem_limit_bytes=VMEM_LIMIT)


def _inproj_kernel(x_ref, gat_ref, w_ref, wvt_ref, gain_ref, cos_ref, sa_ref, sb_ref, bd_ref,
                   qa_ref, ka_ref, qb_ref, kvb_ref, vat_ref, xn_ref):
    x = x_ref[...]
    ms = jnp.mean(x * x, axis=-1, keepdims=True)
    xn_ref[...] = (x * lax.rsqrt(ms + NORM_EPS) * gat_ref[...]).astype(BF16)
    cos = cos_ref[...]
    sa = sa_ref[...]
    sb = sb_ref[...]
    outs = ((qa_ref, COL_QA), (ka_ref, COL_KA), (qb_ref, COL_QB), (kvb_ref, COL_KVB))
    vat_ref[0] = lax.dot_general(wvt_ref[...], xn_ref[...], (((1,), (1,)), ((), ())),
                                 preferred_element_type=F32).astype(BF16)

    def out_for(col):
        for ref, base in reversed(outs):
            if col >= base:
                return ref, col - base
        raise AssertionError

    def project(c):
        return jnp.dot(xn_ref[...], w_ref[:, c * 256:(c + 1) * 256], preferred_element_type=F32)

    nchunks = NX // 256
    p_next = project(0)
    for c in range(nchunks):
        c0 = c * 256
        p = p_next
        if c + 1 < nchunks:
            p_next = project(c + 1)
        normed = [(c0 + t * LANES) // LANES < N_ROPE_TILES for t in range(2)]
        if any(normed):
            ss = jnp.dot((p * p).astype(BF16), bd_ref[...], preferred_element_type=F32)
        for t in range(2):
            col = c0 + t * LANES
            y = p[:, t * LANES:(t + 1) * LANES]
            if normed[t]:
                sst = ss[:, t * LANES:(t + 1) * LANES]
                y = y * lax.rsqrt(sst * (1.0 / HEAD_DIM) + NORM_EPS) * gain_ref[:, col:col + LANES]
                y = y * cos + pltpu.roll(y, 96, 1) * sa + pltpu.roll(y, 32, 1) * sb
            ref, off = out_for(col)
            ref[:, off:off + LANES] = y.astype(BF16)


def _inproj(x, g_attn, w, wvt, gain, cos, sa, sb, bd, seq, tm):
    n = x.shape[0]
    nblk_s = seq // tm
    tok = lambda i: (i, 0)
    full = lambda i: (0, 0)
    rope = lambda i: (i % nblk_s, 0)
    return pl.pallas_call(
        _inproj_kernel,
        grid=(n // tm,),
        in_specs=[
            pl.BlockSpec((tm, D_MODEL), tok),
            pl.BlockSpec((1, D_MODEL), full),
            pl.BlockSpec((D_MODEL, NX), full),
            pl.BlockSpec((DA_W, D_MODEL), full),
            pl.BlockSpec((1, NX), full),
            pl.BlockSpec((tm, LANES), rope),
            pl.BlockSpec((tm, LANES), rope),
            pl.BlockSpec((tm, LANES), rope),
            pl.BlockSpec((256, 256), full),
        ],
        out_specs=[
            pl.BlockSpec((tm, DA_W), tok),
            pl.BlockSpec((tm, DA_W), tok),
            pl.BlockSpec((tm, SW_W), tok),
            pl.BlockSpec((tm, KVB_W), tok),
            pl.BlockSpec((1, DA_W, tm), lambda i: (i // nblk_s, 0, i % nblk_s)),
        ],
        out_shape=[
            jax.ShapeDtypeStruct((n, DA_W), BF16),
            jax.ShapeDtypeStruct((n, DA_W), BF16),
            jax.ShapeDtypeStruct((n, SW_W), BF16),
            jax.ShapeDtypeStruct((n, KVB_W), BF16),
            jax.ShapeDtypeStruct((n // seq, DA_W, seq), BF16),
        ],
        scratch_shapes=[pltpu.VMEM((tm, D_MODEL), BF16)],
        compiler_params=_cparams(("parallel",)),
        name="inproj",
    )(x, g_attn, w, wvt, gain, cos, sa, sb, bd)


def _diffattn_kernel(q_ref, k_ref, vt_ref, lam4_ref, g_ref, o_ref, m_ref, l_ref, acc_ref, qst_ref, s_ref,
                     *, tq, tk, nk, nq, unroll, lam_init):
    def load_queries(i):
        q0 = pl.multiple_of(i * tq, tq)
        qt = q_ref[0, pl.ds(q0, tq), :].astype(F32).T
        feat = lax.broadcasted_iota(jnp.int32, qt.shape, 0)
        zero = jnp.zeros_like(qt)
        qst_ref[...] = jnp.concatenate([jnp.where(feat < HEAD_DIM, qt, zero), jnp.where(feat >= HEAD_DIM, qt, zero)],
                                       axis=1).astype(BF16)

    def reset_state():
        m_ref[...] = jnp.full(m_ref.shape, -jnp.inf, F32)
        acc_ref[...] = jnp.zeros(acc_ref.shape, F32)

    def scores(j):
        k0 = pl.multiple_of(j * tk, tk)
        return jnp.dot(k_ref[0, pl.ds(k0, tk), :], qst_ref[...], preferred_element_type=F32)

    def accumulate(s, j):
        k0 = pl.multiple_of(j * tk, tk)
        vtj = jnp.concatenate([vt_ref[0, :, pl.ds(k0, tk)], jnp.ones((16, tk), BF16)], axis=0)
        m_old = m_ref[...]
        m_new = jnp.maximum(m_old, jnp.max(s, axis=0, keepdims=True))
        alpha = jnp.exp2(m_old - m_new)
        p = jnp.exp2(s - m_new)
        acc_ref[...] = alpha * acc_ref[...] + jnp.dot(vtj, p.astype(BF16), preferred_element_type=F32)
        m_ref[...] = m_new

    def finalize(i):
        ot = acc_ref[0:LANES, :] / acc_ref[LANES:LANES + 1, :]
        lam4 = lam4_ref[...]
        lam = (jnp.exp(jnp.sum(lam4[0:1] * lam4[1:2], axis=-1, keepdims=True))
               - jnp.exp(jnp.sum(lam4[2:3] * lam4[3:4], axis=-1, keepdims=True)) + lam_init)
        o = (ot[:, :tq] - lam * ot[:, tq:]).T
        ms = jnp.mean(o * o, axis=-1, keepdims=True)
        o = o * lax.rsqrt(ms + NORM_EPS) * g_ref[...] * (1.0 - lam_init)
        o_ref[0, pl.ds(pl.multiple_of(i * tq, tq), tq), :] = o.astype(BF16)

    def trip(jj, _):
        j0 = unroll * jj
        for u in range(unroll):
            s_ref[(u + 1) % 2] = scores(j0 + u + 1)
            accumulate(s_ref[u % 2], j0 + u)
        return 0

    def query_tile(i, _):
        lax.fori_loop(0, nk // unroll - 1, trip, 0)
        j0 = nk - unroll
        for u in range(unroll - 1):
            s_ref[(u + 1) % 2] = scores(j0 + u + 1)
            accumulate(s_ref[u % 2], j0 + u)
        load_queries(jnp.minimum(i + 1, nq - 1))
        s_ref[0] = scores(0)
        accumulate(s_ref[1], nk - 1)
        finalize(i)
        reset_state()
        return 0

    load_queries(0)
    reset_state()
    s_ref[0] = scores(0)
    lax.fori_loop(0, nq, query_tile, 0)


def _diffattn(qa, ka, vat, lam4, g_sub, lam_init, tq, tk):
    b, s, _ = qa.shape
    nk = s // tk
    unroll = 8 if nk % 8 == 0 else (4 if nk % 4 == 0 else 2)
    kern = functools.partial(_diffattn_kernel, tq=tq, tk=tk, nk=nk, nq=s // tq, unroll=unroll, lam_init=lam_init)
    per_head = lambda bi, h: (bi, 0, h)
    return pl.pallas_call(
        kern,
        grid=(b, DA_HEADS),
        in_specs=[
            pl.BlockSpec((1, s, LANES), per_head),
            pl.BlockSpec((1, s, LANES), per_head),
            pl.BlockSpec((1, LANES, s), lambda bi, h: (bi, h, 0)),
            pl.BlockSpec((4, HEAD_DIM), lambda bi, h: (0, 0)),
            pl.BlockSpec((1, LANES), lambda bi, h: (0, 0)),
        ],
        out_specs=pl.BlockSpec((1, s, LANES), per_head),
        out_shape=jax.ShapeDtypeStruct((b, s, DA_W), BF16),
        scratch_shapes=[
            pltpu.VMEM((1, 2 * tq), F32),
            pltpu.VMEM((1, 2 * tq), F32),
            pltpu.VMEM((LANES + 16, 2 * tq), F32),
            pltpu.VMEM((LANES, 2 * tq), BF16),
            pltpu.VMEM((2, tk, 2 * tq), F32),
        ],
        compiler_params=_cparams(("parallel", "parallel")),
        name="diffattn",
    )(qa, ka, vat, lam4, g_sub)


def _swa_kernel(sink_ref, q_ref, kv_ref, o_ref, *, tq, seq):
    i = pl.program_id(1)
    kw = 3 * WINDOW
    low = lax.broadcasted_iota(jnp.int32, (WINDOW, LANES), 1) < HEAD_DIM
    blocks = {}

    def block_operands(sb):
        if sb not in blocks:
            q0 = i * tq + sb * WINDOW
            g0 = pl.multiple_of(jnp.clip(q0 - WINDOW, 0, seq - kw), WINDOW)
            kv = kv_ref[0, pl.ds(g0, kw), :]
            k2 = kv[:, 0:LANES]
            v2 = kv[:, LANES:2 * LANES]
            qpos = q0 + lax.broadcasted_iota(jnp.int32, (WINDOW, kw), 0)
            kpos = g0 + lax.broadcasted_iota(jnp.int32, (WINDOW, kw), 1)
            blocks[sb] = dict(mask=jnp.abs(kpos - qpos) <= WINDOW,
                              k=(k2, pltpu.roll(k2.astype(F32), HEAD_DIM, 1).astype(BF16)),
                              v=(v2, pltpu.roll(v2.astype(F32), HEAD_DIM, 1).astype(BF16)))
        return blocks[sb]

    def issue_scores(sb, h):
        ops = block_operands(sb)
        rows = slice(sb * WINDOW, (sb + 1) * WINDOW)
        qp = q_ref[0, rows, (h // 2) * LANES:(h // 2 + 1) * LANES]
        qm = jnp.where(low if h % 2 == 0 else jnp.logical_not(low), qp, jnp.zeros_like(qp))
        swapped = int(h % 2 != h // SW_GROUP)
        return lax.dot_general(qm, ops["k"][swapped], (((1,), (1,)), ((), ())), preferred_element_type=F32)

    def finish(sb, h, scores):
        ops = block_operands(sb)
        s = jnp.where(ops["mask"], scores, MASK_VALUE)
        sink = sink_ref[h]
        m = jnp.maximum(jnp.max(s, axis=-1, keepdims=True), sink)
        e = jnp.exp2(s - m)
        den = jnp.sum(e, axis=-1, keepdims=True) + jnp.exp2(sink - m)
        swapped = int(h % 2 != h // SW_GROUP)
        return jnp.dot(e.astype(BF16), ops["v"][swapped], preferred_element_type=F32) * (1.0 / den)

    units = [(sb, h) for sb in range(tq // WINDOW) for h in range(SW_Q_HEADS)]
    ahead = 3
    pending, outs = {}, {}
    for n in range(len(units) + ahead):
        if n < len(units):
            pending[units[n]] = issue_scores(*units[n])
        if n >= ahead:
            sb, h = units[n - ahead]
            outs[h] = finish(sb, h, pending.pop((sb, h)))
            if h % 2 == 1:
                rows = slice(sb * WINDOW, (sb + 1) * WINDOW)
                o = jnp.where(low, outs.pop(h - 1), outs.pop(h))
                o_ref[0, rows, (h // 2) * LANES:(h // 2 + 1) * LANES] = o.astype(BF16)


def _swa(sink, qbx, kvb, tq):
    b, s, _ = qbx.shape
    kern = functools.partial(_swa_kernel, tq=tq, seq=s)
    return pl.pallas_call(
        kern,
        grid_spec=pltpu.PrefetchScalarGridSpec(
            num_scalar_prefetch=1,
            grid=(b, s // tq),
            in_specs=[
                pl.BlockSpec((1, tq, SW_W), lambda bi, i, sk: (bi, i, 0)),
                pl.BlockSpec((1, s, KVB_W), lambda bi, i, sk: (bi, 0, 0)),
            ],
            out_specs=pl.BlockSpec((1, tq, SW_W), lambda bi, i, sk: (bi, i, 0)),
        ),
        out_shape=jax.ShapeDtypeStruct((b, s, SW_W), BF16),
        compiler_params=_cparams(("parallel", "parallel")),
        name="swa",
    )(sink, qbx, kvb)


def _outproj_kernel(x_ref, oa_ref, obx_ref, woa_ref, wobx_ref, g_ref, wrt_ref, x1_ref, h2r_ref, aff_ref):
    x1 = (x_ref[...] + jnp.dot(oa_ref[...], woa_ref[...], preferred_element_type=F32)
          + jnp.dot(obx_ref[...], wobx_ref[...], preferred_element_type=F32))
    x1_ref[...] = x1
    ms = jnp.mean(x1 * x1, axis=-1, keepdims=True)
    h2 = x1 * lax.rsqrt(ms + NORM_EPS) * g_ref[...]
    tm = h2.shape[0]
    for j in range(SUBLANES):
        h2r_ref[pl.ds(j, tm, stride=SUBLANES), :] = h2[:, j * LANES:(j + 1) * LANES]
    h_hi = h2.astype(BF16)
    h_lo = (h2 - h_hi.astype(F32)).astype(BF16)
    w = wrt_ref[...]
    w_hi = w.astype(BF16)
    w_lo = (w - w_hi.astype(F32)).astype(BF16)
    logits = lax.dot_general(jnp.concatenate([w_hi, w_hi, w_lo], axis=1),
                             jnp.concatenate([h_hi, h_lo, h_hi], axis=1),
                             (((1,), (1,)), ((), ())), preferred_element_type=F32)
    mx = jnp.max(logits, axis=0, keepdims=True)
    e = jnp.exp(logits - mx)
    aff_ref[...] = e / jnp.sum(e, axis=0, keepdims=True)


def _outproj(x, oa, obx, w_oa, w_obx, g_ffn, w_rt, tm):
    n = x.shape[0]
    tok = lambda i: (i, 0)
    full = lambda i: (0, 0)
    return pl.pallas_call(
        _outproj_kernel,
        grid=(n // tm,),
        in_specs=[
            pl.BlockSpec((tm, D_MODEL), tok),
            pl.BlockSpec((tm, DA_W), tok),
            pl.BlockSpec((tm, SW_W), tok),
            pl.BlockSpec((DA_W, D_MODEL), full),
            pl.BlockSpec((SW_W, D_MODEL), full),
            pl.BlockSpec((1, D_MODEL), full),
            pl.BlockSpec((N_EXPERTS, D_MODEL), full),
        ],
        out_specs=[
            pl.BlockSpec((tm, D_MODEL), tok),
            pl.BlockSpec((tm * SUBLANES, LANES), tok),
            pl.BlockSpec((N_EXPERTS, tm), lambda i: (0, i)),
        ],
        out_shape=[
            jax.ShapeDtypeStruct((n, D_MODEL), F32),
            jax.ShapeDtypeStruct((n * SUBLANES, LANES), F32),
            jax.ShapeDtypeStruct((N_EXPERTS, n), F32),
        ],
        compiler_params=_cparams(("parallel",)),
        name="outproj",
    )(x, oa, obx, w_oa, w_obx, g_ffn, w_rt)


def _route_kernel(aff_ref, loc_ref, vals_ref, cex_ref, thr_ref, *, cap, nchunk, pblk, tchunk):
    n_exp = aff_ref.shape[0]

    def expert_bits(e):
        return pltpu.bitcast(aff_ref[e], jnp.int32)

    def count(mask):
        per_lane = jnp.sum(jnp.where(mask, 1.0, 0.0), axis=0, keepdims=True)
        return jnp.sum(per_lane, axis=1, keepdims=True)

    def search(i, thrs):
        bit = jnp.left_shift(jnp.int32(1), 30 - i)
        out = []
        for e in range(n_exp):
            cand = thrs[e] | bit
            out.append(jnp.where(count(expert_bits(e) >= cand) >= cap, cand, thrs[e]))
        return tuple(out)

    thrs = lax.fori_loop(0, 31, search, tuple(jnp.zeros((1, 1), jnp.int32) for _ in range(n_exp)))
    for e in range(n_exp):
        thr_ref[e] = jnp.broadcast_to(thrs[e], (1, LANES))
    lax.fori_loop(0, n_exp, functools.partial(_route_expert, aff_ref, loc_ref, vals_ref, cex_ref, thr_ref,
                                              count, cap, nchunk, pblk, tchunk), 0)


def _route_expert(aff_ref, loc_ref, vals_ref, cex_ref, thr_ref, count, cap, nchunk, pblk, tchunk, e, carry):
    a = aff_ref[e]
    bits = pltpu.bitcast(a, jnp.int32)
    thr = thr_ref[e]
    gt = bits > thr
    eq = bits == thr
    need = cap - count(gt)

    r128 = lax.broadcasted_iota(jnp.int32, (LANES, LANES), 0)
    c128 = lax.broadcasted_iota(jnp.int32, (LANES, LANES), 1)
    u_incl = jnp.where(r128 <= c128, 1.0, 0.0).astype(BF16)
    ones = jnp.ones((LANES, LANES), BF16)
    rc = lax.broadcasted_iota(jnp.int32, (nchunk, nchunk), 0)
    cc = lax.broadcasted_iota(jnp.int32, (nchunk, nchunk), 1)
    l_strict = jnp.where(cc < rc, 1.0, 0.0).astype(BF16)
    u_strict = jnp.where(rc < cc, 1.0, 0.0).astype(BF16)

    def chunk_scan(mask_bf):
        incl = jnp.dot(mask_bf, u_incl, preferred_element_type=F32)
        tot_b = jnp.dot(mask_bf, ones, preferred_element_type=F32)
        cexcl_b = jnp.dot(l_strict, tot_b.astype(BF16), preferred_element_type=F32)
        return incl, cexcl_b

    eq_bf = jnp.where(eq, 1.0, 0.0).astype(BF16)
    incl_eq, cexcl_eq = chunk_scan(eq_bf)
    rank_eq = cexcl_eq + incl_eq - eq_bf.astype(F32)
    sel = gt | (eq & (rank_eq < need))
    sel_bf = jnp.where(sel, 1.0, 0.0).astype(BF16)
    lc, cexcl_b = chunk_scan(sel_bf)

    tot_row = lax.dot_general(jnp.ones((8, LANES), BF16), sel_bf, (((1,), (1,)), ((), ())),
                              preferred_element_type=F32)
    cexcl_row = jnp.dot(tot_row.astype(BF16), u_strict, preferred_element_type=F32)[0:1]
    cincl_row = cexcl_row + tot_row[0:1]

    a_hi = a.astype(BF16)
    r1 = a - a_hi.astype(F32)
    a_mid = r1.astype(BF16)
    a_lo = (r1 - a_mid.astype(F32)).astype(BF16)
    kidx = lax.broadcasted_iota(jnp.int32, (nchunk, LANES), 0).astype(F32)
    cex_hi = jnp.floor(cexcl_b * (1.0 / LANES))
    cex_lo = cexcl_b - cex_hi * LANES
    table = jnp.concatenate([lc.astype(BF16), a_hi, a_mid, a_lo, kidx.astype(BF16),
                             cex_hi.astype(BF16), cex_lo.astype(BF16)], axis=1)

    lane_f = lax.broadcasted_iota(jnp.int32, (pblk, LANES), 1).astype(F32)
    eye = r128 == c128

    def slot_block(bi, _):
        base = bi * pblk
        pc = (base + lax.broadcasted_iota(jnp.int32, (pblk, nchunk), 0)).astype(F32)
        onehot_k = jnp.where((cexcl_row <= pc) & (pc < cincl_row), 1.0, 0.0).astype(BF16)
        r = jnp.dot(onehot_k, table, preferred_element_type=F32)
        r_lc = r[:, 0:LANES]
        r_aff = (r[:, LANES:2 * LANES] + r[:, 2 * LANES:3 * LANES]) + r[:, 3 * LANES:4 * LANES]
        r_k = r[:, 4 * LANES:5 * LANES]
        r_cex = r[:, 5 * LANES:6 * LANES] * LANES + r[:, 6 * LANES:7 * LANES]
        p_loc = (base + lax.broadcasted_iota(jnp.int32, (pblk, LANES), 0)).astype(F32) - r_cex
        below = jnp.where(r_lc <= p_loc, 1.0, 0.0).astype(BF16)
        t_loc = jnp.dot(below, ones, preferred_element_type=F32)
        val = jnp.sum(jnp.where(lane_f == t_loc, r_aff, 0.0), axis=-1, keepdims=True)
        tok = r_k * LANES + t_loc
        for sb in range(pblk // LANES):
            rows = slice(sb * LANES, (sb + 1) * LANES)
            out = pl.ds(pl.multiple_of(base + sb * LANES, LANES), LANES)
            tok_row = jnp.sum(jnp.where(eye, tok[rows], 0.0), axis=0, keepdims=True).astype(jnp.int32)
            loc_ref[e, :, out] = (tok_row & (tchunk - 1)) * SUBLANES
            vals_ref[e, :, out] = jnp.sum(jnp.where(eye, val[rows], 0.0), axis=0, keepdims=True)
        return 0

    lax.fori_loop(0, cap // pblk, slot_block, 0)
    loc_ref[e, :, pl.ds(cap, SLOT_PAD)] = jnp.zeros((1, SLOT_PAD), jnp.int32)
    cex_ref[e] = cexcl_row.astype(jnp.int32)
    return carry


def _route(aff3, cap, tchunk):
    e, nchunk, _ = aff3.shape
    pblk = min(1024, cap)
    kern = functools.partial(_route_kernel, cap=cap, nchunk=nchunk, pblk=pblk, tchunk=tchunk)
    return pl.pallas_call(
        kern,
        out_shape=[
            jax.ShapeDtypeStruct((e, 1, cap + SLOT_PAD), jnp.int32),
            jax.ShapeDtypeStruct((e, 1, cap), F32),
            jax.ShapeDtypeStruct((e, 1, nchunk), jnp.int32),
        ],
        scratch_shapes=[pltpu.VMEM((e, 1, LANES), jnp.int32)],
        compiler_params=pltpu.CompilerParams(vmem_limit_bytes=VMEM_LIMIT),
        name="route",
    )(aff3)


def _expert_kernel(ps_ref, loc_ref, val_ref, h2r_ref, wg_ref, wu_ref, wd_ref, moe_ref, xt_ref, yt_ref,
                   *, rblk, stride):
    c = pl.program_id(0)
    e = pl.program_id(1)

    @pl.when(e == 0)
    def _():
        moe_ref[...] = jnp.zeros(moe_ref.shape, F32)

    p0 = ps_ref[e, c]
    p1 = ps_ref[e, c + 1]
    unroll = SUBLANES

    def tile_row(p):
        return pl.multiple_of(loc_ref[0, 0, p], SUBLANES)

    nblk = (p1 - p0 + rblk - 1) // rblk

    def gather_rows(base, r0, count):
        for i in range(count):
            src = tile_row(base + r0 + i)
            xt_ref[pl.ds(r0 + i, SUBLANES, stride=stride), :] = h2r_ref[pl.ds(src, SUBLANES), :]

    @pl.when(nblk > 0)
    def _():
        def first_gather(g, _):
            gather_rows(p0, g * 2 * unroll, 2 * unroll)
            return 0

        lax.fori_loop(0, rblk // (2 * unroll), first_gather, 0)

    def block(b, _):
        base = p0 + b * rblk
        x = jnp.concatenate([xt_ref[pl.ds(j * stride, rblk), :].astype(BF16) for j in range(SUBLANES)], axis=1)
        g = jnp.dot(x, wg_ref[0], preferred_element_type=F32)
        u = jnp.dot(x, wu_ref[0], preferred_element_type=F32)
        hmid = (g * jax.nn.sigmoid(g) * u).astype(BF16)
        y = jnp.dot(hmid, wd_ref[0], preferred_element_type=F32)
        for j in range(SUBLANES):
            yt_ref[pl.ds(j * stride, rblk), :] = y[:, j * LANES:(j + 1) * LANES]

        nvalid = jnp.minimum(rblk, p1 - base)

        def scatter_rows(r0, count):
            new = []
            for i in range(count):
                dst = tile_row(base + r0 + i)
                contrib = yt_ref[pl.ds(r0 + i, SUBLANES, stride=stride), :] * val_ref[0, 0, base + r0 + i]
                new.append((dst, moe_ref[pl.ds(dst, SUBLANES), :] + contrib))
            for dst, v in new:
                moe_ref[pl.ds(dst, SUBLANES), :] = v

        def scatter_pair(g, _):
            scatter_rows(g * 2 * unroll, unroll)
            scatter_rows(g * 2 * unroll + unroll, unroll)
            return 0

        has_next = b + 1 < nblk

        @pl.when(has_next)
        def _():
            def scatter_and_gather(g, _):
                scatter_pair(g, 0)
                gather_rows(base + rblk, g * 2 * unroll, 2 * unroll)
                return 0

            lax.fori_loop(0, rblk // (2 * unroll), scatter_and_gather, 0)

        @pl.when(jnp.logical_not(has_next))
        def _():
            npairs = nvalid // (2 * unroll)
            lax.fori_loop(0, npairs, scatter_pair, 0)

            def scatter_one(r, _):
                scatter_rows(r, 1)
                return 0

            lax.fori_loop(npairs * 2 * unroll, nvalid, scatter_one, 0)

        return 0

    lax.fori_loop(0, nblk, block, 0)


def _experts(pstart, loc, vals, h2r, wg, wu, wd, layer, tchunk, rblk):
    e, _, cap_pad = loc.shape
    n8 = h2r.shape[0]
    stride = rblk + SUBLANES
    kern = functools.partial(_expert_kernel, rblk=rblk, stride=stride)
    expert = lambda ci, ei, ps: (layer * e + ei, 0, 0)
    wspec = pl.BlockSpec((1, D_MODEL, D_EXPERT), expert)
    slot = lambda a: pl.BlockSpec((1, 1, a.shape[2]), lambda ci, ei, ps: (ei, 0, 0), memory_space=pltpu.SMEM)
    chunk = lambda: pl.BlockSpec((tchunk * SUBLANES, LANES), lambda ci, ei, ps: (ci, 0),
                                 pipeline_mode=pl.Buffered(1))
    return pl.pallas_call(
        kern,
        grid_spec=pltpu.PrefetchScalarGridSpec(
            num_scalar_prefetch=1,
            grid=(n8 // (tchunk * SUBLANES), e),
            in_specs=[slot(loc), slot(vals), chunk(), wspec, wspec,
                      pl.BlockSpec((1, D_EXPERT, D_MODEL), expert)],
            out_specs=chunk(),
            scratch_shapes=[
                pltpu.VMEM((SUBLANES * stride, LANES), F32),
                pltpu.VMEM((SUBLANES * stride, LANES), F32),
            ],
        ),
        out_shape=jax.ShapeDtypeStruct((n8, LANES), F32),
        compiler_params=pltpu.CompilerParams(dimension_semantics=("arbitrary", "arbitrary"),
                                             vmem_limit_bytes=EXPERT_VMEM_LIMIT),
        name="experts",
    )(pstart, loc, vals, h2r, wg, wu, wd)


def _ple_kernel(x_ref, moe_ref, p_ref, g_ref, wg_ref, wp_ref, o_ref):
    tm = x_ref.shape[0]
    moe = jnp.concatenate([moe_ref[pl.ds(j, tm, stride=SUBLANES), :] for j in range(SUBLANES)], axis=1)
    x = x_ref[...] + moe
    ms = jnp.mean(x * x, axis=-1, keepdims=True)
    hn = (x * lax.rsqrt(ms + NORM_EPS) * g_ref[...]).astype(BF16)
    gate = jax.nn.sigmoid(jnp.dot(hn, wg_ref[...], preferred_element_type=F32))
    emb = jnp.dot(p_ref[0].astype(BF16), wp_ref[...], preferred_element_type=F32)
    o_ref[...] = x + gate * emb


def _ple(x, moe_r, p_all, layer, g_ple, w_g, w_p, tm):
    n = x.shape[0]
    tok = lambda i: (i, 0)
    full = lambda i: (0, 0)
    return pl.pallas_call(
        _ple_kernel,
        grid=(n // tm,),
        in_specs=[
            pl.BlockSpec((tm, D_MODEL), tok),
            pl.BlockSpec((tm * SUBLANES, LANES), tok),
            pl.BlockSpec((1, tm, PLE_DIM), lambda i: (layer, i, 0)),
            pl.BlockSpec((1, D_MODEL), full),
            pl.BlockSpec((D_MODEL, D_MODEL), full),
            pl.BlockSpec((PLE_DIM, D_MODEL), full),
        ],
        out_specs=pl.BlockSpec((tm, D_MODEL), tok),
        out_shape=jax.ShapeDtypeStruct((n, D_MODEL), F32),
        compiler_params=_cparams(("parallel",)),
        name="ple",
    )(x, moe_r, p_all, g_ple, w_g, w_p)


def _rope_tables(seq):
    pos = jnp.arange(seq, dtype=F32)
    inv = ROPE_THETA ** (-jnp.arange(0, HEAD_DIM, 2, dtype=F32) / HEAD_DIM)
    ang = pos[:, None] * inv[None, :]
    ang = jnp.concatenate([ang, ang, ang, ang], axis=-1)
    cos, sin = jnp.cos(ang), jnp.sin(ang)
    first_half = (jnp.arange(LANES) % HEAD_DIM) < HEAD_DIM // 2
    sin_a = jnp.where(first_half[None, :], -sin, 0.0)
    sin_b = jnp.where(first_half[None, :], 0.0, sin)
    return cos, sin_a, sin_b


def _layer_params(i, attn_norm, w_in, q_norm_a, k_norm_a, subln_a, q_norm_b, k_norm_b, w_out):
    wi = w_in[i]
    w = jnp.concatenate([wi[:, 0:1024], wi[:, 1536:2304]], axis=1).astype(BF16)
    wvt = wi[:, 1024:1536].T.astype(BF16)
    scale = HEAD_DIM ** -0.5
    log2e = math.log2(math.e)
    t2 = lambda g, n: jnp.tile(g, n)
    gain = jnp.concatenate([
        t2(q_norm_a[i], 8) * (scale * log2e), t2(k_norm_a[i], 8), t2(q_norm_b[i], 8) * (scale * log2e),
        t2(k_norm_b[i], 2), jnp.ones((NX - COL_KVB - LANES,), F32)])[None, :]
    wo = w_out[i].astype(BF16)
    lam_init = 0.8 - 0.6 * math.exp(-0.3 * i)
    return dict(g_attn=attn_norm[i][None, :], w=w, wvt=wvt, gain=gain, w_oa=wo[0:DA_W], w_ob=wo[DA_W:],
                g_sub=subln_a[i][None, :], lam_init=lam_init)


def _pick(n, pref):
    t = pref
    while n % t:
        t //= 2
    return t


def _encoder(x, p, attn_norm, w_in, q_norm_a, k_norm_a, lambda_q1, lambda_k1, lambda_q2, lambda_k2,
             subln_a, q_norm_b, k_norm_b, sink_b, w_out, ffn_norm, w_router, w_gate_e, w_up_e, w_down_e,
             ple_norm, w_ple_gate, w_ple_proj):
    b, s, _ = x.shape
    n = b * s
    cap = EC_CAPACITY_FACTOR * n // N_EXPERTS
    tm = _pick(s, 1024)
    tq = _pick(s, 512)
    tk = _pick(s // 2, 512)
    tq_sw = _pick(s, 512)
    tchunk = _pick(n, 4096)
    rblk = EXPERT_ROWS
    cos, sin_a, sin_b = _rope_tables(s)
    r = lax.broadcasted_iota(jnp.int32, (256, 256), 0) // HEAD_DIM
    c = lax.broadcasted_iota(jnp.int32, (256, 256), 1) // HEAD_DIM
    bd = (r == c).astype(BF16)
    xf = x.reshape(n, D_MODEL)
    p_all = p.reshape(DEPTH, n, PLE_DIM)
    wg_all = w_gate_e.astype(BF16).reshape(DEPTH * N_EXPERTS, D_MODEL, D_EXPERT)
    wu_all = w_up_e.astype(BF16).reshape(DEPTH * N_EXPERTS, D_MODEL, D_EXPERT)
    wd_all = w_down_e.astype(BF16).reshape(DEPTH * N_EXPERTS, D_EXPERT, D_MODEL)
    for i in range(DEPTH):
        lp = _layer_params(i, attn_norm, w_in, q_norm_a, k_norm_a, subln_a, q_norm_b, k_norm_b, w_out)
        qa, ka, qb, kvb, vat = _inproj(xf, lp["g_attn"], lp["w"], lp["wvt"], lp["gain"], cos, sin_a, sin_b, bd,
                                        s, tm)
        lam4 = jnp.stack([lambda_q1[i], lambda_k1[i], lambda_q2[i], lambda_k2[i]]).astype(F32)
        oa = _diffattn(qa.reshape(b, s, DA_W), ka.reshape(b, s, DA_W), vat, lam4,
                       lp["g_sub"], lp["lam_init"], tq, tk)
        ob = _swa(sink_b[i].astype(F32) * math.log2(math.e), qb.reshape(b, s, SW_W),
                   kvb.reshape(b, s, KVB_W), tq_sw)
        x1, h2r, aff_t = _outproj(xf, oa.reshape(n, DA_W), ob.reshape(n, SW_W), lp["w_oa"], lp["w_ob"],
                                  ffn_norm[i][None, :], w_router[i].T, tm)
        loc, vals, cex = _route(aff_t.reshape(N_EXPERTS, n // LANES, LANES), cap, tchunk)
        pstart = jnp.concatenate([cex[:, 0, ::tchunk // LANES], jnp.full((N_EXPERTS, 1), cap, jnp.int32)], axis=1)
        moe_r = _experts(pstart, loc, vals, h2r, wg_all, wu_all, wd_all, i, tchunk, rblk)
        xf = _ple(x1, moe_r, p_all, i, ple_norm[i][None, :], w_ple_gate[i].astype(BF16),
                  w_ple_proj[i].astype(BF16), tm)
    return xf.reshape(b, s, D_MODEL)


def kernel(x_prompt, x_sample, p_prompt, p_sample, attn_norm, w_in, q_norm_a, k_norm_a, lambda_q1, lambda_k1,
           lambda_q2, lambda_k2, subln_a, q_norm_b, k_norm_b, sink_b, w_out, ffn_norm, w_router, w_gate_e,
           w_up_e, w_down_e, ple_norm, w_ple_gate, w_ple_proj):
    ws = (attn_norm, w_in, q_norm_a, k_norm_a, lambda_q1, lambda_k1, lambda_q2, lambda_k2, subln_a, q_norm_b,
          k_norm_b, sink_b, w_out, ffn_norm, w_router, w_gate_e, w_up_e, w_down_e, ple_norm, w_ple_gate,
          w_ple_proj)
    return (_encoder(x_prompt, p_prompt, *ws), _encoder(x_sample, p_sample, *ws))
```

```python
import functools
import math

import jax
import jax.numpy as jnp
from jax import lax
from jax.experimental import pallas as pl
from jax.experimental.pallas import tpu as pltpu

F32 = jnp.float32
BF16 = jnp.bfloat16

D_MODEL = 1024
DEPTH = 4
HEAD_DIM = 64
DA_HEADS = 4
SW_Q_HEADS = 8
SW_KV_HEADS = 2
SW_GROUP = SW_Q_HEADS // SW_KV_HEADS
WINDOW = 128
N_EXPERTS = 16
EC_CAPACITY_FACTOR = 2
D_EXPERT = 1024
PLE_DIM = 256
ROPE_THETA = 10000.0
NORM_EPS = 1e-6
MASK_VALUE = -1e30

LANES = 128
SUBLANES = 8
BF16_ROWS = 16
MXU_N = 256
DA_W = DA_HEADS * 2 * HEAD_DIM
SW_W = SW_Q_HEADS * HEAD_DIM
KVB_W = 2 * LANES
COL_QA, COL_KA, COL_QB, COL_KVB = 0, 512, 1024, 1536
NX = 1792
N_ROPE_TILES = (COL_KVB + LANES) // LANES
VMEM_LIMIT = 48 * 1024 * 1024
EXPERT_VMEM_LIMIT = 56 * 1024 * 1024
EXPERT_ROWS = 288
SLOT_PAD = 512


def _cparams(sem):
    return pltpu.CompilerParams(dimension_semantics=sem, vmem_limit_bytes=VMEM_LIMIT)


def _inproj_kernel(x_ref, gat_ref, w_ref, wvt_ref, gain_ref, cos_ref, sa_ref, sb_ref, bd_ref,
                   qa_ref, ka_ref, qb_ref, kvb_ref, vat_ref, xn_ref):
    x = x_ref[...]
    ms = jnp.mean(x * x, axis=-1, keepdims=True)
    xn_ref[...] = (x * lax.rsqrt(ms + NORM_EPS) * gat_ref[...]).astype(BF16)
    cos = cos_ref[...]
    sa = sa_ref[...]
    sb = sb_ref[...]
    outs = ((qa_ref, COL_QA), (ka_ref, COL_KA), (qb_ref, COL_QB), (kvb_ref, COL_KVB))
    vat_ref[0] = lax.dot_general(wvt_ref[...], xn_ref[...], (((1,), (1,)), ((), ())),
                                 preferred_element_type=F32).astype(BF16)

    def out_for(col):
        for ref, base in reversed(outs):
            if col >= base:
                return ref, col - base
        raise AssertionError

    def project(c):
        return jnp.dot(xn_ref[...], w_ref[:, c * MXU_N:(c + 1) * MXU_N], preferred_element_type=F32)

    nchunks = NX // MXU_N
    p_next = project(0)
    for c in range(nchunks):
        c0 = c * MXU_N
        p = p_next
        if c + 1 < nchunks:
            p_next = project(c + 1)
        normed = [(c0 + t * LANES) // LANES < N_ROPE_TILES for t in range(MXU_N // LANES)]
        if any(normed):
            ss = jnp.dot((p * p).astype(BF16), bd_ref[...], preferred_element_type=F32)
        for t in range(MXU_N // LANES):
            col = c0 + t * LANES
            y = p[:, t * LANES:(t + 1) * LANES]
            if normed[t]:
                sst = ss[:, t * LANES:(t + 1) * LANES]
                y = y * lax.rsqrt(sst * (1.0 / HEAD_DIM) + NORM_EPS) * gain_ref[:, col:col + LANES]
                y = y * cos + pltpu.roll(y, 96, 1) * sa + pltpu.roll(y, 32, 1) * sb
            ref, off = out_for(col)
            ref[:, off:off + LANES] = y.astype(BF16)


def _inproj(x, g_attn, w, wvt, gain, cos, sa, sb, bd, seq, tm):
    n = x.shape[0]
    nblk_s = seq // tm
    tok = lambda i: (i, 0)
    full = lambda i: (0, 0)
    rope = lambda i: (i % nblk_s, 0)
    return pl.pallas_call(
        _inproj_kernel,
        grid=(n // tm,),
        in_specs=[
            pl.BlockSpec((tm, D_MODEL), tok),
            pl.BlockSpec((1, D_MODEL), full),
            pl.BlockSpec((D_MODEL, NX), full),
            pl.BlockSpec((DA_W, D_MODEL), full),
            pl.BlockSpec((1, NX), full),
            pl.BlockSpec((tm, LANES), rope),
            pl.BlockSpec((tm, LANES), rope),
            pl.BlockSpec((tm, LANES), rope),
            pl.BlockSpec((MXU_N, MXU_N), full),
        ],
        out_specs=[
            pl.BlockSpec((tm, DA_W), tok),
            pl.BlockSpec((tm, DA_W), tok),
            pl.BlockSpec((tm, SW_W), tok),
            pl.BlockSpec((tm, KVB_W), tok),
            pl.BlockSpec((1, DA_W, tm), lambda i: (i // nblk_s, 0, i % nblk_s)),
        ],
        out_shape=[
            jax.ShapeDtypeStruct((n, DA_W), BF16),
            jax.ShapeDtypeStruct((n, DA_W), BF16),
            jax.ShapeDtypeStruct((n, SW_W), BF16),
            jax.ShapeDtypeStruct((n, KVB_W), BF16),
            jax.ShapeDtypeStruct((n // seq, DA_W, seq), BF16),
        ],
        scratch_shapes=[pltpu.VMEM((tm, D_MODEL), BF16)],
        compiler_params=_cparams(("parallel",)),
        name="inproj",
    )(x, g_attn, w, wvt, gain, cos, sa, sb, bd)


def _diffattn_kernel(q_ref, k_ref, vt_ref, lam4_ref, g_ref, o_ref, m_ref, acc_ref, qst_ref, s_ref,
                     *, tq, tk, nk, nq, unroll, lam_init):
    def load_queries(i):
        q0 = pl.multiple_of(i * tq, tq)
        qt = q_ref[0, pl.ds(q0, tq), :].astype(F32).T
        feat = lax.broadcasted_iota(jnp.int32, qt.shape, 0)
        zero = jnp.zeros_like(qt)
        qst_ref[...] = jnp.concatenate([jnp.where(feat < HEAD_DIM, qt, zero), jnp.where(feat >= HEAD_DIM, qt, zero)],
                                       axis=1).astype(BF16)

    def reset_state():
        m_ref[...] = jnp.full(m_ref.shape, -jnp.inf, F32)
        acc_ref[...] = jnp.zeros(acc_ref.shape, F32)

    def scores(j):
        k0 = pl.multiple_of(j * tk, tk)
        return jnp.dot(k_ref[0, pl.ds(k0, tk), :], qst_ref[...], preferred_element_type=F32)

    def accumulate(s, j):
        k0 = pl.multiple_of(j * tk, tk)
        vtj = jnp.concatenate([vt_ref[0, :, pl.ds(k0, tk)], jnp.ones((BF16_ROWS, tk), BF16)], axis=0)
        m_old = m_ref[...]
        m_new = jnp.maximum(m_old, jnp.max(s, axis=0, keepdims=True))
        alpha = jnp.exp2(m_old - m_new)
        p = jnp.exp2(s - m_new)
        acc_ref[...] = alpha * acc_ref[...] + jnp.dot(vtj, p.astype(BF16), preferred_element_type=F32)
        m_ref[...] = m_new

    def finalize(i):
        ot = acc_ref[0:LANES, :] / acc_ref[LANES:LANES + 1, :]
        lam4 = lam4_ref[...]
        lam = (jnp.exp(jnp.sum(lam4[0:1] * lam4[1:2], axis=-1, keepdims=True))
               - jnp.exp(jnp.sum(lam4[2:3] * lam4[3:4], axis=-1, keepdims=True)) + lam_init)
        o = (ot[:, :tq] - lam * ot[:, tq:]).T
        ms = jnp.mean(o * o, axis=-1, keepdims=True)
        o = o * lax.rsqrt(ms + NORM_EPS) * g_ref[...] * (1.0 - lam_init)
        o_ref[0, pl.ds(pl.multiple_of(i * tq, tq), tq), :] = o.astype(BF16)

    def trip(jj, _):
        j0 = unroll * jj
        for u in range(unroll):
            s_ref[(u + 1) % 2] = scores(j0 + u + 1)
            accumulate(s_ref[u % 2], j0 + u)
        return 0

    def query_tile(i, _):
        lax.fori_loop(0, nk // unroll - 1, trip, 0)
        j0 = nk - unroll
        for u in range(unroll - 1):
            s_ref[(u + 1) % 2] = scores(j0 + u + 1)
            accumulate(s_ref[u % 2], j0 + u)
        load_queries(jnp.minimum(i + 1, nq - 1))
        s_ref[0] = scores(0)
        accumulate(s_ref[1], nk - 1)
        finalize(i)
        reset_state()
        return 0

    load_queries(0)
    reset_state()
    s_ref[0] = scores(0)
    lax.fori_loop(0, nq, query_tile, 0)


def _diffattn(qa, ka, vat, lam4, g_sub, lam_init, tq, tk):
    b, s, _ = qa.shape
    nk = s // tk
    unroll = 8 if nk % 8 == 0 else (4 if nk % 4 == 0 else 2)
    kern = functools.partial(_diffattn_kernel, tq=tq, tk=tk, nk=nk, nq=s // tq, unroll=unroll, lam_init=lam_init)
    per_head = lambda bi, h: (bi, 0, h)
    return pl.pallas_call(
        kern,
        grid=(b, DA_HEADS),
        in_specs=[
            pl.BlockSpec((1, s, LANES), per_head),
            pl.BlockSpec((1, s, LANES), per_head),
            pl.BlockSpec((1, LANES, s), lambda bi, h: (bi, h, 0)),
            pl.BlockSpec((4, HEAD_DIM), lambda bi, h: (0, 0)),
            pl.BlockSpec((1, LANES), lambda bi, h: (0, 0)),
        ],
        out_specs=pl.BlockSpec((1, s, LANES), per_head),
        out_shape=jax.ShapeDtypeStruct((b, s, DA_W), BF16),
        scratch_shapes=[
            pltpu.VMEM((1, 2 * tq), F32),
            pltpu.VMEM((LANES + BF16_ROWS, 2 * tq), F32),
            pltpu.VMEM((LANES, 2 * tq), BF16),
            pltpu.VMEM((2, tk, 2 * tq), F32),
        ],
        compiler_params=_cparams(("parallel", "parallel")),
        name="diffattn",
    )(qa, ka, vat, lam4, g_sub)


def _swa_kernel(sink_ref, q_ref, kv_ref, o_ref, *, tq, seq):
    i = pl.program_id(1)
    kw = 3 * WINDOW
    low = lax.broadcasted_iota(jnp.int32, (WINDOW, LANES), 1) < HEAD_DIM
    blocks = {}

    def block_operands(sb):
        if sb not in blocks:
            q0 = i * tq + sb * WINDOW
            g0 = pl.multiple_of(jnp.clip(q0 - WINDOW, 0, seq - kw), WINDOW)
            kv = kv_ref[0, pl.ds(g0, kw), :]
            k2 = kv[:, 0:LANES]
            v2 = kv[:, LANES:2 * LANES]
            qpos = q0 + lax.broadcasted_iota(jnp.int32, (WINDOW, kw), 0)
            kpos = g0 + lax.broadcasted_iota(jnp.int32, (WINDOW, kw), 1)
            blocks[sb] = dict(mask=jnp.abs(kpos - qpos) <= WINDOW,
                              k=(k2, pltpu.roll(k2.astype(F32), HEAD_DIM, 1).astype(BF16)),
                              v=(v2, pltpu.roll(v2.astype(F32), HEAD_DIM, 1).astype(BF16)))
        return blocks[sb]

    def issue_scores(sb, h):
        ops = block_operands(sb)
        rows = slice(sb * WINDOW, (sb + 1) * WINDOW)
        qp = q_ref[0, rows, (h // 2) * LANES:(h // 2 + 1) * LANES]
        qm = jnp.where(low if h % 2 == 0 else jnp.logical_not(low), qp, jnp.zeros_like(qp))
        swapped = int(h % 2 != h // SW_GROUP)
        return lax.dot_general(qm, ops["k"][swapped], (((1,), (1,)), ((), ())), preferred_element_type=F32)

    def finish(sb, h, scores):
        ops = block_operands(sb)
        s = jnp.where(ops["mask"], scores, MASK_VALUE)
        sink = sink_ref[h]
        m = jnp.maximum(jnp.max(s, axis=-1, keepdims=True), sink)
        e = jnp.exp2(s - m)
        den = jnp.sum(e, axis=-1, keepdims=True) + jnp.exp2(sink - m)
        swapped = int(h % 2 != h // SW_GROUP)
        return jnp.dot(e.astype(BF16), ops["v"][swapped], preferred_element_type=F32) * (1.0 / den)

    units = [(sb, h) for sb in range(tq // WINDOW) for h in range(SW_Q_HEADS)]
    ahead = 3
    pending, outs = {}, {}
    for n in range(len(units) + ahead):
        if n < len(units):
            pending[units[n]] = issue_scores(*units[n])
        if n >= ahead:
            sb, h = units[n - ahead]
            outs[h] = finish(sb, h, pending.pop((sb, h)))
            if h % 2 == 1:
                rows = slice(sb * WINDOW, (sb + 1) * WINDOW)
                o = jnp.where(low, outs.pop(h - 1), outs.pop(h))
                o_ref[0, rows, (h // 2) * LANES:(h // 2 + 1) * LANES] = o.astype(BF16)


def _swa(sink, qbx, kvb, tq):
    b, s, _ = qbx.shape
    kern = functools.partial(_swa_kernel, tq=tq, seq=s)
    return pl.pallas_call(
        kern,
        grid_spec=pltpu.PrefetchScalarGridSpec(
            num_scalar_prefetch=1,
            grid=(b, s // tq),
            in_specs=[
                pl.BlockSpec((1, tq, SW_W), lambda bi, i, sk: (bi, i, 0)),
                pl.BlockSpec((1, s, KVB_W), lambda bi, i, sk: (bi, 0, 0)),
            ],
            out_specs=pl.BlockSpec((1, tq, SW_W), lambda bi, i, sk: (bi, i, 0)),
        ),
        out_shape=jax.ShapeDtypeStruct((b, s, SW_W), BF16),
        compiler_params=_cparams(("parallel", "parallel")),
        name="swa",
    )(sink, qbx, kvb)


def _outproj_kernel(x_ref, oa_ref, obx_ref, woa_ref, wobx_ref, g_ref, wrt_ref, x1_ref, h2r_ref, aff_ref):
    x1 = (x_ref[...] + jnp.dot(oa_ref[...], woa_ref[...], preferred_element_type=F32)
          + jnp.dot(obx_ref[...], wobx_ref[...], preferred_element_type=F32))
    x1_ref[...] = x1
    ms = jnp.mean(x1 * x1, axis=-1, keepdims=True)
    h2 = x1 * lax.rsqrt(ms + NORM_EPS) * g_ref[...]
    tm = h2.shape[0]
    for j in range(SUBLANES):
        h2r_ref[pl.ds(j, tm, stride=SUBLANES), :] = h2[:, j * LANES:(j + 1) * LANES]
    h_hi = h2.astype(BF16)
    h_lo = (h2 - h_hi.astype(F32)).astype(BF16)
    w = wrt_ref[...]
    w_hi = w.astype(BF16)
    w_lo = (w - w_hi.astype(F32)).astype(BF16)
    logits = lax.dot_general(jnp.concatenate([w_hi, w_hi, w_lo], axis=1),
                             jnp.concatenate([h_hi, h_lo, h_hi], axis=1),
                             (((1,), (1,)), ((), ())), preferred_element_type=F32)
    mx = jnp.max(logits, axis=0, keepdims=True)
    e = jnp.exp(logits - mx)
    aff_ref[...] = e / jnp.sum(e, axis=0, keepdims=True)


def _outproj(x, oa, obx, w_oa, w_obx, g_ffn, w_rt, tm):
    n = x.shape[0]
    tok = lambda i: (i, 0)
    full = lambda i: (0, 0)
    return pl.pallas_call(
        _outproj_kernel,
        grid=(n // tm,),
        in_specs=[
            pl.BlockSpec((tm, D_MODEL), tok),
            pl.BlockSpec((tm, DA_W), tok),
            pl.BlockSpec((tm, SW_W), tok),
            pl.BlockSpec((DA_W, D_MODEL), full),
            pl.BlockSpec((SW_W, D_MODEL), full),
            pl.BlockSpec((1, D_MODEL), full),
            pl.BlockSpec((N_EXPERTS, D_MODEL), full),
        ],
        out_specs=[
            pl.BlockSpec((tm, D_MODEL), tok),
            pl.BlockSpec((tm * SUBLANES, LANES), tok),
            pl.BlockSpec((N_EXPERTS, tm), lambda i: (0, i)),
        ],
        out_shape=[
            jax.ShapeDtypeStruct((n, D_MODEL), F32),
            jax.ShapeDtypeStruct((n * SUBLANES, LANES), F32),
            jax.ShapeDtypeStruct((N_EXPERTS, n), F32),
        ],
        compiler_params=_cparams(("parallel",)),
        name="outproj",
    )(x, oa, obx, w_oa, w_obx, g_ffn, w_rt)


def _route_kernel(aff_ref, loc_ref, vals_ref, cex_ref, thr_ref, *, cap, nchunk, pblk, tchunk):
    n_exp = aff_ref.shape[0]

    def expert_bits(e):
        return pltpu.bitcast(aff_ref[e], jnp.int32)

    def count(mask):
        per_lane = jnp.sum(jnp.where(mask, 1.0, 0.0), axis=0, keepdims=True)
        return jnp.sum(per_lane, axis=1, keepdims=True)

    def search(i, thrs):
        bit = jnp.left_shift(jnp.int32(1), 30 - i)
        out = []
        for e in range(n_exp):
            cand = thrs[e] | bit
            out.append(jnp.where(count(expert_bits(e) >= cand) >= cap, cand, thrs[e]))
        return tuple(out)

    thrs = lax.fori_loop(0, 31, search, tuple(jnp.zeros((1, 1), jnp.int32) for _ in range(n_exp)))
    for e in range(n_exp):
        thr_ref[e] = jnp.broadcast_to(thrs[e], (1, LANES))
    lax.fori_loop(0, n_exp, functools.partial(_route_expert, aff_ref, loc_ref, vals_ref, cex_ref, thr_ref,
                                              count, cap, nchunk, pblk, tchunk), 0)


def _route_expert(aff_ref, loc_ref, vals_ref, cex_ref, thr_ref, count, cap, nchunk, pblk, tchunk, e, carry):
    a = aff_ref[e]
    bits = pltpu.bitcast(a, jnp.int32)
    thr = thr_ref[e]
    gt = bits > thr
    eq = bits == thr
    need = cap - count(gt)

    r128 = lax.broadcasted_iota(jnp.int32, (LANES, LANES), 0)
    c128 = lax.broadcasted_iota(jnp.int32, (LANES, LANES), 1)
    u_incl = jnp.where(r128 <= c128, 1.0, 0.0).astype(BF16)
    ones = jnp.ones((LANES, LANES), BF16)
    rc = lax.broadcasted_iota(jnp.int32, (nchunk, nchunk), 0)
    cc = lax.broadcasted_iota(jnp.int32, (nchunk, nchunk), 1)
    l_strict = jnp.where(cc < rc, 1.0, 0.0).astype(BF16)
    u_strict = jnp.where(rc < cc, 1.0, 0.0).astype(BF16)

    def chunk_scan(mask_bf):
        incl = jnp.dot(mask_bf, u_incl, preferred_element_type=F32)
        tot_b = jnp.dot(mask_bf, ones, preferred_element_type=F32)
        cexcl_b = jnp.dot(l_strict, tot_b.astype(BF16), preferred_element_type=F32)
        return incl, cexcl_b

    eq_bf = jnp.where(eq, 1.0, 0.0).astype(BF16)
    incl_eq, cexcl_eq = chunk_scan(eq_bf)
    rank_eq = cexcl_eq + incl_eq - eq_bf.astype(F32)
    sel = gt | (eq & (rank_eq < need))
    sel_bf = jnp.where(sel, 1.0, 0.0).astype(BF16)
    lc, cexcl_b = chunk_scan(sel_bf)

    tot_row = lax.dot_general(jnp.ones((8, LANES), BF16), sel_bf, (((1,), (1,)), ((), ())),
                              preferred_element_type=F32)
    cexcl_row = jnp.dot(tot_row.astype(BF16), u_strict, preferred_element_type=F32)[0:1]
    cincl_row = cexcl_row + tot_row[0:1]

    a_hi = a.astype(BF16)
    r1 = a - a_hi.astype(F32)
    a_mid = r1.astype(BF16)
    a_lo = (r1 - a_mid.astype(F32)).astype(BF16)
    kidx = lax.broadcasted_iota(jnp.int32, (nchunk, LANES), 0).astype(F32)
    cex_hi = jnp.floor(cexcl_b * (1.0 / LANES))
    cex_lo = cexcl_b - cex_hi * LANES
    table = jnp.concatenate([lc.astype(BF16), a_hi, a_mid, a_lo, kidx.astype(BF16),
                             cex_hi.astype(BF16), cex_lo.astype(BF16)], axis=1)

    lane_f = lax.broadcasted_iota(jnp.int32, (pblk, LANES), 1).astype(F32)
    eye = r128 == c128

    def slot_block(bi, _):
        base = bi * pblk
        pc = (base + lax.broadcasted_iota(jnp.int32, (pblk, nchunk), 0)).astype(F32)
        onehot_k = jnp.where((cexcl_row <= pc) & (pc < cincl_row), 1.0, 0.0).astype(BF16)
        r = jnp.dot(onehot_k, table, preferred_element_type=F32)
        r_lc = r[:, 0:LANES]
        r_aff = (r[:, LANES:2 * LANES] + r[:, 2 * LANES:3 * LANES]) + r[:, 3 * LANES:4 * LANES]
        r_k = r[:, 4 * LANES:5 * LANES]
        r_cex = r[:, 5 * LANES:6 * LANES] * LANES + r[:, 6 * LANES:7 * LANES]
        p_loc = (base + lax.broadcasted_iota(jnp.int32, (pblk, LANES), 0)).astype(F32) - r_cex
        below = jnp.where(r_lc <= p_loc, 1.0, 0.0).astype(BF16)
        t_loc = jnp.dot(below, ones, preferred_element_type=F32)
        val = jnp.sum(jnp.where(lane_f == t_loc, r_aff, 0.0), axis=-1, keepdims=True)
        tok = r_k * LANES + t_loc
        for sb in range(pblk // LANES):
            rows = slice(sb * LANES, (sb + 1) * LANES)
            out = pl.ds(pl.multiple_of(base + sb * LANES, LANES), LANES)
            tok_row = jnp.sum(jnp.where(eye, tok[rows], 0.0), axis=0, keepdims=True).astype(jnp.int32)
            loc_ref[e, :, out] = (tok_row & (tchunk - 1)) * SUBLANES
            vals_ref[e, :, out] = jnp.sum(jnp.where(eye, val[rows], 0.0), axis=0, keepdims=True)
        return 0

    lax.fori_loop(0, cap // pblk, slot_block, 0)
    loc_ref[e, :, pl.ds(cap, SLOT_PAD)] = jnp.zeros((1, SLOT_PAD), jnp.int32)
    cex_ref[e] = cexcl_row.astype(jnp.int32)
    return carry


def _route(aff3, cap, tchunk):
    e, nchunk, _ = aff3.shape
    pblk = min(1024, cap)
    kern = functools.partial(_route_kernel, cap=cap, nchunk=nchunk, pblk=pblk, tchunk=tchunk)
    return pl.pallas_call(
        kern,
        out_shape=[
            jax.ShapeDtypeStruct((e, 1, cap + SLOT_PAD), jnp.int32),
            jax.ShapeDtypeStruct((e, 1, cap), F32),
            jax.ShapeDtypeStruct((e, 1, nchunk), jnp.int32),
        ],
        scratch_shapes=[pltpu.VMEM((e, 1, LANES), jnp.int32)],
        compiler_params=pltpu.CompilerParams(vmem_limit_bytes=VMEM_LIMIT),
        name="route",
    )(aff3)


def _expert_kernel(ps_ref, loc_ref, val_ref, h2r_ref, wg_ref, wu_ref, wd_ref, moe_ref, xt_ref, yt_ref,
                   *, rblk, stride):
    c = pl.program_id(0)
    e = pl.program_id(1)

    @pl.when(e == 0)
    def _():
        moe_ref[...] = jnp.zeros(moe_ref.shape, F32)

    p0 = ps_ref[e, c]
    p1 = ps_ref[e, c + 1]
    unroll = SUBLANES

    def tile_row(p):
        return pl.multiple_of(loc_ref[0, 0, p], SUBLANES)

    nblk = (p1 - p0 + rblk - 1) // rblk

    def gather_rows(base, r0, count):
        for i in range(count):
            src = tile_row(base + r0 + i)
            xt_ref[pl.ds(r0 + i, SUBLANES, stride=stride), :] = h2r_ref[pl.ds(src, SUBLANES), :]

    @pl.when(nblk > 0)
    def _():
        def first_gather(g, _):
            gather_rows(p0, g * 2 * unroll, 2 * unroll)
            return 0

        lax.fori_loop(0, rblk // (2 * unroll), first_gather, 0)

    def block(b, _):
        base = p0 + b * rblk
        x = jnp.concatenate([xt_ref[pl.ds(j * stride, rblk), :].astype(BF16) for j in range(SUBLANES)], axis=1)
        g = jnp.dot(x, wg_ref[0], preferred_element_type=F32)
        u = jnp.dot(x, wu_ref[0], preferred_element_type=F32)
        hmid = (g * jax.nn.sigmoid(g) * u).astype(BF16)
        y = jnp.dot(hmid, wd_ref[0], preferred_element_type=F32)
        for j in range(SUBLANES):
            yt_ref[pl.ds(j * stride, rblk), :] = y[:, j * LANES:(j + 1) * LANES]

        nvalid = jnp.minimum(rblk, p1 - base)

        def scatter_rows(r0, count):
            new = []
            for i in range(count):
                dst = tile_row(base + r0 + i)
                contrib = yt_ref[pl.ds(r0 + i, SUBLANES, stride=stride), :] * val_ref[0, 0, base + r0 + i]
                new.append((dst, moe_ref[pl.ds(dst, SUBLANES), :] + contrib))
            for dst, v in new:
                moe_ref[pl.ds(dst, SUBLANES), :] = v

        def scatter_pair(g, _):
            scatter_rows(g * 2 * unroll, unroll)
            scatter_rows(g * 2 * unroll + unroll, unroll)
            return 0

        has_next = b + 1 < nblk

        @pl.when(has_next)
        def _():
            def scatter_and_gather(g, _):
                scatter_pair(g, 0)
                gather_rows(base + rblk, g * 2 * unroll, 2 * unroll)
                return 0

            lax.fori_loop(0, rblk // (2 * unroll), scatter_and_gather, 0)

        @pl.when(jnp.logical_not(has_next))
        def _():
            npairs = nvalid // (2 * unroll)
            lax.fori_loop(0, npairs, scatter_pair, 0)

            def scatter_one(r, _):
                scatter_rows(r, 1)
                return 0

            lax.fori_loop(npairs * 2 * unroll, nvalid, scatter_one, 0)

        return 0

    lax.fori_loop(0, nblk, block, 0)


def _experts(pstart, loc, vals, h2r, wg, wu, wd, layer, tchunk, rblk):
    e, _, cap_pad = loc.shape
    n8 = h2r.shape[0]
    stride = rblk + SUBLANES
    kern = functools.partial(_expert_kernel, rblk=rblk, stride=stride)
    expert = lambda ci, ei, ps: (layer * e + ei, 0, 0)
    wspec = pl.BlockSpec((1, D_MODEL, D_EXPERT), expert)
    slot = lambda a: pl.BlockSpec((1, 1, a.shape[2]), lambda ci, ei, ps: (ei, 0, 0), memory_space=pltpu.SMEM)
    chunk = lambda: pl.BlockSpec((tchunk * SUBLANES, LANES), lambda ci, ei, ps: (ci, 0),
                                 pipeline_mode=pl.Buffered(1))
    return pl.pallas_call(
        kern,
        grid_spec=pltpu.PrefetchScalarGridSpec(
            num_scalar_prefetch=1,
            grid=(n8 // (tchunk * SUBLANES), e),
            in_specs=[slot(loc), slot(vals), chunk(), wspec, wspec,
                      pl.BlockSpec((1, D_EXPERT, D_MODEL), expert)],
            out_specs=chunk(),
            scratch_shapes=[
                pltpu.VMEM((SUBLANES * stride, LANES), F32),
                pltpu.VMEM((SUBLANES * stride, LANES), F32),
            ],
        ),
        out_shape=jax.ShapeDtypeStruct((n8, LANES), F32),
        compiler_params=pltpu.CompilerParams(dimension_semantics=("arbitrary", "arbitrary"),
                                             vmem_limit_bytes=EXPERT_VMEM_LIMIT),
        name="experts",
    )(pstart, loc, vals, h2r, wg, wu, wd)


def _ple_kernel(x_ref, moe_ref, p_ref, g_ref, wg_ref, wp_ref, o_ref):
    tm = x_ref.shape[0]
    moe = jnp.concatenate([moe_ref[pl.ds(j, tm, stride=SUBLANES), :] for j in range(SUBLANES)], axis=1)
    x = x_ref[...] + moe
    ms = jnp.mean(x * x, axis=-1, keepdims=True)
    hn = (x * lax.rsqrt(ms + NORM_EPS) * g_ref[...]).astype(BF16)
    gate = jax.nn.sigmoid(jnp.dot(hn, wg_ref[...], preferred_element_type=F32))
    emb = jnp.dot(p_ref[0].astype(BF16), wp_ref[...], preferred_element_type=F32)
    o_ref[...] = x + gate * emb


def _ple(x, moe_r, p_all, layer, g_ple, w_g, w_p, tm):
    n = x.shape[0]
    tok = lambda i: (i, 0)
    full = lambda i: (0, 0)
    return pl.pallas_call(
        _ple_kernel,
        grid=(n // tm,),
        in_specs=[
            pl.BlockSpec((tm, D_MODEL), tok),
            pl.BlockSpec((tm * SUBLANES, LANES), tok),
            pl.BlockSpec((1, tm, PLE_DIM), lambda i: (layer, i, 0)),
            pl.BlockSpec((1, D_MODEL), full),
            pl.BlockSpec((D_MODEL, D_MODEL), full),
            pl.BlockSpec((PLE_DIM, D_MODEL), full),
        ],
        out_specs=pl.BlockSpec((tm, D_MODEL), tok),
        out_shape=jax.ShapeDtypeStruct((n, D_MODEL), F32),
        compiler_params=_cparams(("parallel",)),
        name="ple",
    )(x, moe_r, p_all, g_ple, w_g, w_p)


def _rope_tables(seq):
    pos = jnp.arange(seq, dtype=F32)
    inv = ROPE_THETA ** (-jnp.arange(0, HEAD_DIM, 2, dtype=F32) / HEAD_DIM)
    ang = pos[:, None] * inv[None, :]
    ang = jnp.concatenate([ang, ang, ang, ang], axis=-1)
    cos, sin = jnp.cos(ang), jnp.sin(ang)
    first_half = (jnp.arange(LANES) % HEAD_DIM) < HEAD_DIM // 2
    sin_a = jnp.where(first_half[None, :], -sin, 0.0)
    sin_b = jnp.where(first_half[None, :], 0.0, sin)
    return cos, sin_a, sin_b


def _layer_params(i, attn_norm, w_in, q_norm_a, k_norm_a, subln_a, q_norm_b, k_norm_b, w_out):
    wi = w_in[i]
    w = jnp.concatenate([wi[:, 0:2 * DA_W], wi[:, 3 * DA_W:]], axis=1).astype(BF16)
    wvt = wi[:, 2 * DA_W:3 * DA_W].T.astype(BF16)
    scale = HEAD_DIM ** -0.5
    log2e = math.log2(math.e)
    t2 = lambda g, n: jnp.tile(g, n)
    gain = jnp.concatenate([
        t2(q_norm_a[i], 8) * (scale * log2e), t2(k_norm_a[i], 8), t2(q_norm_b[i], 8) * (scale * log2e),
        t2(k_norm_b[i], 2), jnp.ones((NX - COL_KVB - LANES,), F32)])[None, :]
    wo = w_out[i].astype(BF16)
    lam_init = 0.8 - 0.6 * math.exp(-0.3 * i)
    return dict(g_attn=attn_norm[i][None, :], w=w, wvt=wvt, gain=gain, w_oa=wo[0:DA_W], w_ob=wo[DA_W:],
                g_sub=subln_a[i][None, :], lam_init=lam_init)


def _pick(n, pref):
    t = pref
    while n % t:
        t //= 2
    return t


def _encoder(x, p, attn_norm, w_in, q_norm_a, k_norm_a, lambda_q1, lambda_k1, lambda_q2, lambda_k2,
             subln_a, q_norm_b, k_norm_b, sink_b, w_out, ffn_norm, w_router, w_gate_e, w_up_e, w_down_e,
             ple_norm, w_ple_gate, w_ple_proj):
    b, s, _ = x.shape
    n = b * s
    cap = EC_CAPACITY_FACTOR * n // N_EXPERTS
    tm = _pick(s, 1024)
    tq = _pick(s, 512)
    tk = _pick(s // 2, 512)
    tq_sw = _pick(s, 512)
    tchunk = _pick(n, 4096)
    rblk = EXPERT_ROWS
    cos, sin_a, sin_b = _rope_tables(s)
    r = lax.broadcasted_iota(jnp.int32, (MXU_N, MXU_N), 0) // HEAD_DIM
    c = lax.broadcasted_iota(jnp.int32, (MXU_N, MXU_N), 1) // HEAD_DIM
    bd = (r == c).astype(BF16)
    xf = x.reshape(n, D_MODEL)
    p_all = p.reshape(DEPTH, n, PLE_DIM)
    wg_all = w_gate_e.astype(BF16).reshape(DEPTH * N_EXPERTS, D_MODEL, D_EXPERT)
    wu_all = w_up_e.astype(BF16).reshape(DEPTH * N_EXPERTS, D_MODEL, D_EXPERT)
    wd_all = w_down_e.astype(BF16).reshape(DEPTH * N_EXPERTS, D_EXPERT, D_MODEL)
    for i in range(DEPTH):
        lp = _layer_params(i, attn_norm, w_in, q_norm_a, k_norm_a, subln_a, q_norm_b, k_norm_b, w_out)
        qa, ka, qb, kvb, vat = _inproj(xf, lp["g_attn"], lp["w"], lp["wvt"], lp["gain"], cos, sin_a, sin_b, bd,
                                        s, tm)
        lam4 = jnp.stack([lambda_q1[i], lambda_k1[i], lambda_q2[i], lambda_k2[i]]).astype(F32)
        oa = _diffattn(qa.reshape(b, s, DA_W), ka.reshape(b, s, DA_W), vat, lam4,
                       lp["g_sub"], lp["lam_init"], tq, tk)
        ob = _swa(sink_b[i].astype(F32) * math.log2(math.e), qb.reshape(b, s, SW_W),
                   kvb.reshape(b, s, KVB_W), tq_sw)
        x1, h2r, aff_t = _outproj(xf, oa.reshape(n, DA_W), ob.reshape(n, SW_W), lp["w_oa"], lp["w_ob"],
                                  ffn_norm[i][None, :], w_router[i].T, tm)
        loc, vals, cex = _route(aff_t.reshape(N_EXPERTS, n // LANES, LANES), cap, tchunk)
        pstart = jnp.concatenate([cex[:, 0, ::tchunk // LANES], jnp.full((N_EXPERTS, 1), cap, jnp.int32)], axis=1)
        moe_r = _experts(pstart, loc, vals, h2r, wg_all, wu_all, wd_all, i, tchunk, rblk)
        xf = _ple(x1, moe_r, p_all, i, ple_norm[i][None, :], w_ple_gate[i].astype(BF16),
                  w_ple_proj[i].astype(BF16), tm)
    return xf.reshape(b, s, D_MODEL)


def kernel(x_prompt, x_sample, p_prompt, p_sample, attn_norm, w_in, q_norm_a, k_norm_a, lambda_q1, lambda_k1,
           lambda_q2, lambda_k2, subln_a, q_norm_b, k_norm_b, sink_b, w_out, ffn_norm, w_router, w_gate_e,
           w_up_e, w_down_e, ple_norm, w_ple_gate, w_ple_proj):
    ws = (attn_norm, w_in, q_norm_a, k_norm_a, lambda_q1, lambda_k1, lambda_q2, lambda_k2, subln_a, q_norm_b,
          k_norm_b, sink_b, w_out, ffn_norm, w_router, w_gate_e, w_up_e, w_down_e, ple_norm, w_ple_gate,
          w_ple_proj)
    return (_encoder(x_prompt, p_prompt, *ws), _encoder(x_sample, p_sample, *ws))
```
